```python
import jax, jax.numpy as jnp
from jax import lax
import numpy as np

D_MODEL = 1024
BATCH = 32
SEQ = 256
DEPTH = 2
DEC_BATCH = 8
DEC_SEQ = 4096
PAST_LEN = 256

GRID_W = 64
N_EVEN = (DEPTH + 1) // 2
N_ODD = DEPTH // 2
N_SUB = 3
HEAD_DIM = 64
N_RET = 8
RET_DK = 64
RET_DV = 64
RET_CHUNK = 128
RET_GN_EPS = 1e-5
N_Q = 8
N_KV = 2
Q_GROUP = N_Q // N_KV
Q_BLOCK = 128
ROPE_THETA = 10000.0
N_NA = 16
NA_KR_MAX = 8
NA_KC = 16
D_FF = 2816
EPS = 1e-6
RET_W = N_RET * RET_DK
RET_VW = N_RET * RET_DV
GQA_W = N_Q * HEAD_DIM
KV_W = N_KV * HEAD_DIM
AB_IN = 2 * RET_W + 2 * RET_VW + GQA_W + 2 * KV_W
AB_SPLITS = (RET_W, 2 * RET_W, 2 * RET_W + RET_VW, 2 * RET_W + 2 * RET_VW,
             2 * RET_W + 2 * RET_VW + GQA_W, 2 * RET_W + 2 * RET_VW + GQA_W + KV_W)
AB_OUT = RET_VW + GQA_W
NA_W = N_NA * HEAD_DIM

kernel_name = 'hybrid_flow_retention_gqa_natten'


def rmsnorm(x, g):
    xf = x.astype(jnp.float32)
    y = xf * lax.rsqrt(jnp.mean(xf * xf, axis=-1, keepdims=True) + EPS)
    return (y * g.astype(jnp.float32)).astype(x.dtype)


def softmax_f32(s, dtype):
    return jax.nn.softmax(s.astype(jnp.float32), axis=-1).astype(dtype)


def swiglu(h, w_in, w_out):
    gate, up = jnp.split(h @ w_in, 2, axis=-1)
    return (jax.nn.silu(gate) * up) @ w_out


def adaln_sublayer(x, mod, s, g_pre, g_post, fn, weight):
    shift = mod[:, 3 * s, None]
    scale = mod[:, 3 * s + 1, None]
    gate = mod[:, 3 * s + 2, None]
    h = rmsnorm(x, g_pre) * (1 + scale) + shift
    y, aux = fn(h)
    return x + weight * gate * rmsnorm(y, g_post), aux


def axial_rope(x):
    n = x.shape[1]
    t = jnp.arange(n)
    rows, cols = t // GRID_W, t % GRID_W
    half = HEAD_DIM // 2
    quarter = half // 2
    inv = ROPE_THETA ** (-jnp.arange(quarter, dtype=jnp.float32) / quarter)

    def rot(xp, pos):
        ang = pos.astype(jnp.float32)[:, None] * inv[None, :]
        cos = jnp.cos(ang)[None, :, None, :].astype(x.dtype)
        sin = jnp.sin(ang)[None, :, None, :].astype(x.dtype)
        x1, x2 = xp[..., :quarter], xp[..., quarter:]
        return jnp.concatenate([x1 * cos - x2 * sin, x2 * cos + x1 * sin], axis=-1)

    return jnp.concatenate([rot(x[..., :half], rows), rot(x[..., half:], cols)], axis=-1)


def attend(q, k, v):
    s = jnp.einsum('bqhgd,bkhd->bhgqk', q, k) * (HEAD_DIM ** -0.5)
    p = softmax_f32(s, v.dtype)
    return jnp.einsum('bhgqk,bkhd->bqhgd', p, v)


def attend_query_blocks(q, k, v):
    b, n = q.shape[0], q.shape[1]
    nb = n // Q_BLOCK
    qb = q.reshape((b, nb, Q_BLOCK) + q.shape[2:]).swapaxes(0, 1)
    ob = lax.map(lambda qq: attend(qq, k, v), qb)
    return ob.swapaxes(0, 1).reshape(q.shape)


def retention_chunkwise(q, k, v, log_gamma, init_state, strict):
    b, nh, seq_len, dk = q.shape
    dv = v.shape[-1]
    c = RET_CHUNK
    nc = seq_len // c
    qc = q.reshape(b, nh, nc, c, dk)
    kc = k.reshape(b, nh, nc, c, dk)
    vc = v.reshape(b, nh, nc, c, dv)
    idx = jnp.arange(c, dtype=jnp.float32)
    lg = log_gamma.astype(jnp.float32)[:, None]
    diff = idx[:, None] - idx[None, :]
    mask = (diff > 0) if strict else (diff >= 0)
    dmat = jnp.where(mask[None], jnp.exp(lg[:, :, None] * jnp.maximum(diff, 0.0)[None]), 0.0).astype(q.dtype)
    q_dec = jnp.exp(lg * (idx + 1.0)).astype(q.dtype)
    k_dec = jnp.exp(lg * (c - 1.0 - idx)).astype(q.dtype)
    c_dec = jnp.exp(lg[:, 0] * c).astype(q.dtype)[None, :, None, None]
    s = jnp.einsum('bhnid,bhnjd->bhnij', qc, kc) * dmat[None, :, None]
    intra = jnp.einsum('bhnij,bhnje->bhnie', s, vc)
    kv = jnp.einsum('bhncd,bhnce,hc->nbhde', kc, vc, k_dec)

    def step(state, kv_n):
        return c_dec * state + kv_n, state

    final_state, prev_states = lax.scan(step, init_state.astype(q.dtype), kv)
    cross = jnp.einsum('bhncd,nbhde,hc->bhnce', qc, prev_states, q_dec)
    return (intra + cross).reshape(b, nh, seq_len, dv), final_state


def bidirectional_retention(q, k, v, lg_f, lg_b, s_f, s_b):
    o_f, fin_f = retention_chunkwise(q, k, v, lg_f, s_f, False)
    o_b, fin_b = retention_chunkwise(jnp.flip(q, 2), jnp.flip(k, 2), jnp.flip(v, 2), lg_b, s_b, True)
    return o_f + jnp.flip(o_b, 2), fin_f, fin_b


def head_groupnorm(o, g):
    of = o.astype(jnp.float32)
    mu = jnp.mean(of, axis=-1, keepdims=True)
    var = jnp.mean(jnp.square(of - mu), axis=-1, keepdims=True)
    y = (of - mu) * lax.rsqrt(var + RET_GN_EPS)
    b, nh, seq_len, dv = o.shape
    y = y.transpose(0, 2, 1, 3).reshape(b, seq_len, nh * dv)
    return (y * g.astype(jnp.float32)).astype(o.dtype)


def mixer_ab(h, w_in, w_out, dec_f, dec_b, gn, qn, kn, ctx):
    b, n, _ = h.shape
    rq, rk, rv, rg, gq, gk, gv = jnp.split(h @ w_in, AB_SPLITS, axis=-1)
    to_heads = lambda a, nh: a.reshape(b, n, nh, -1)
    rq_h = to_heads(rq, N_RET).transpose(0, 2, 1, 3)
    rk_h = to_heads(rk, N_RET).transpose(0, 2, 1, 3) * (RET_DK ** -0.5)
    rv_h = to_heads(rv, N_RET).transpose(0, 2, 1, 3)
    lg_f = jax.nn.log_sigmoid(dec_f.astype(jnp.float32))
    lg_b = jax.nn.log_sigmoid(dec_b.astype(jnp.float32))
    if ctx is None:
        s_f = jnp.zeros((b, N_RET, RET_DK, RET_DV), h.dtype)
        s_b = jnp.zeros((b, N_RET, RET_DK, RET_DV), h.dtype)
    else:
        s_f, s_b = ctx[2], ctx[3]
    o_r, fin_f, fin_b = bidirectional_retention(rq_h, rk_h, rv_h, lg_f, lg_b, s_f, s_b)
    y_ret = head_groupnorm(o_r, gn) * jax.nn.silu(rg)
    q = rmsnorm(to_heads(gq, N_Q), qn)
    k = rmsnorm(to_heads(gk, N_KV), kn)
    v = to_heads(gv, N_KV)
    if ctx is None:
        y_att = attend(q.reshape(b, n, N_KV, Q_GROUP, HEAD_DIM), k, v)
        aux = (k, v, fin_f, fin_b)
    else:
        q = axial_rope(q)
        k_all = jnp.concatenate([axial_rope(k), ctx[0]], axis=1)
        v_all = jnp.concatenate([v, ctx[1]], axis=1)
        y_att = attend_query_blocks(q.reshape(b, n, N_KV, Q_GROUP, HEAD_DIM), k_all, v_all)
        aux = None
    y = jnp.concatenate([y_ret, y_att.reshape(b, n, GQA_W)], axis=-1) @ w_out
    return y, aux


def neighbourhood_attention(q, k, v, k_ctx, v_ctx, rpb):
    b, n, nh, d = q.shape
    rows = n // GRID_W
    kr = min(NA_KR_MAX, rows)
    qg = q.reshape(b, rows, GRID_W, nh, d)
    kg = k.reshape(b, rows, GRID_W, nh, d)
    vg = v.reshape(b, rows, GRID_W, nh, d)
    cols = jnp.arange(GRID_W)
    c0 = jnp.clip(cols - NA_KC // 2, 0, GRID_W - NA_KC)
    col_idx = c0[:, None] + jnp.arange(NA_KC)[None, :]
    dc = col_idx - cols[:, None] + (NA_KC - 1)
    scale = HEAD_DIM ** -0.5
    rpb = rpb.astype(q.dtype)

    def row_fn(args):
        r, q_row = args
        r0 = jnp.clip(r - kr // 2, 0, rows - kr)
        k_win = lax.dynamic_slice_in_dim(kg, r0, kr, axis=1)[:, :, col_idx]
        v_win = lax.dynamic_slice_in_dim(vg, r0, kr, axis=1)[:, :, col_idx]
        dr = r0 + jnp.arange(kr) - r + (NA_KR_MAX - 1)
        bias = rpb[:, dr[None, :, None], dc[:, None, :]]
        s_win = jnp.einsum('bqhd,baqchd->bhqac', q_row, k_win) * scale + bias
        s_ctx = jnp.einsum('bqhd,bkhd->bhqk', q_row, k_ctx) * scale
        logits = jnp.concatenate([s_win.reshape(b, nh, GRID_W, kr * NA_KC), s_ctx], axis=-1)
        p = softmax_f32(logits, v.dtype)
        p_win = p[..., :kr * NA_KC].reshape(b, nh, GRID_W, kr, NA_KC)
        p_ctx = p[..., kr * NA_KC:]
        return (jnp.einsum('bhqac,baqchd->bqhd', p_win, v_win)
                + jnp.einsum('bhqk,bkhd->bqhd', p_ctx, v_ctx))

    out = lax.map(row_fn, (jnp.arange(rows), qg.transpose(1, 0, 2, 3, 4)))
    return out.transpose(1, 0, 2, 3, 4).reshape(b, n, nh, d)


def mixer_na(h, w_qkv, w_out, rpb, ctx):
    b, n, _ = h.shape
    q, k, v = [a.reshape(b, n, N_NA, HEAD_DIM) for a in jnp.split(h @ w_qkv, 3, axis=-1)]
    if ctx is None:
        y = attend(q[:, :, :, None, :], k, v)
        aux = (k, v)
    else:
        y = neighbourhood_attention(q, k, v, ctx[0], ctx[1], rpb)
        aux = None
    return y.reshape(b, n, NA_W) @ w_out, aux


def setup_inputs(seed: int = 0) -> dict:
    key = jax.random.key(seed)
    ks = jax.random.split(key, 26)

    def nrm(i, shape, scale):
        return jax.random.normal(ks[i], shape, jnp.float32) * scale

    D = D_MODEL
    decay_logit = jnp.log(2.0 ** (5.0 + jnp.arange(N_RET, dtype=jnp.float32)) - 1.0)
    return {
        'x_prompt': nrm(0, (BATCH, SEQ, D), 1.0),
        'x_sample': nrm(1, (DEC_BATCH, DEC_SEQ, D), 1.0),
        'cache_gqa_k': nrm(2, (DEC_BATCH, N_EVEN, PAST_LEN, N_KV, HEAD_DIM), 1.0),
        'cache_gqa_v': nrm(3, (DEC_BATCH, N_EVEN, PAST_LEN, N_KV, HEAD_DIM), 1.0),
        'state_ret_fwd': nrm(4, (DEC_BATCH, N_EVEN, N_RET, RET_DK, RET_DV), 0.5),
        'state_ret_bwd': nrm(5, (DEC_BATCH, N_EVEN, N_RET, RET_DK, RET_DV), 0.5),
        'cache_na_k': nrm(6, (DEC_BATCH, N_ODD, PAST_LEN, N_NA, HEAD_DIM), 1.0),
        'cache_na_v': nrm(7, (DEC_BATCH, N_ODD, PAST_LEN, N_NA, HEAD_DIM), 1.0),
        'c': nrm(8, (DEC_BATCH, D), 1.0),
        'c_ctx': nrm(9, (D,), 1.0),
        'mod_w': nrm(10, (DEPTH, D, 3 * N_SUB * D), 0.5 * D ** -0.5),
        'mod_b': nrm(11, (DEPTH, 3 * N_SUB * D), 0.02),
        'norm_pre': 1.0 + nrm(12, (DEPTH, N_SUB, D), 0.02),
        'norm_post': 1.0 + nrm(13, (DEPTH, N_SUB, D), 0.02),
        'ffn_w_in': nrm(14, (DEPTH, 2, D, 2 * D_FF), D ** -0.5),
        'ffn_w_out': nrm(15, (DEPTH, 2, D_FF, D), D_FF ** -0.5),
        'ab_w_in': nrm(16, (N_EVEN, D, AB_IN), D ** -0.5),
        'ab_w_out': nrm(17, (N_EVEN, AB_OUT, D), AB_OUT ** -0.5),
        'ret_decay_fwd': decay_logit[None, :] + nrm(18, (N_EVEN, N_RET), 0.1),
        'ret_decay_bwd': decay_logit[None, :] + nrm(19, (N_EVEN, N_RET), 0.1),
        'ret_gn': 1.0 + nrm(20, (N_EVEN, RET_VW), 0.02),
        'gqa_q_norm': 1.0 + nrm(21, (N_EVEN, HEAD_DIM), 0.02),
        'gqa_k_norm': 1.0 + nrm(22, (N_EVEN, HEAD_DIM), 0.02),
        'na_w_qkv': nrm(23, (N_ODD, D, 3 * NA_W), D ** -0.5),
        'na_w_out': nrm(24, (N_ODD, NA_W, D), NA_W ** -0.5),
        'na_rpb': nrm(25, (N_ODD, N_NA, 2 * NA_KR_MAX - 1, 2 * NA_KC - 1), 0.1),
    }


def reference(x_prompt, x_sample, cache_gqa_k, cache_gqa_v, state_ret_fwd, state_ret_bwd, cache_na_k,
              cache_na_v, c, c_ctx, mod_w, mod_b, norm_pre, norm_post, ffn_w_in, ffn_w_out, ab_w_in,
              ab_w_out, ret_decay_fwd, ret_decay_bwd, ret_gn, gqa_q_norm, gqa_k_norm, na_w_qkv, na_w_out,
              na_rpb):

    def trunk(x, cond, ctx_layers):
        aux_out = []
        for l in range(DEPTH):
            mod = (jax.nn.silu(cond) @ mod_w[l] + mod_b[l]).reshape(cond.shape[0], 3 * N_SUB, D_MODEL)
            ctx = None if ctx_layers is None else ctx_layers[l]
            i = l // 2
            ffn1 = lambda h, l=l: (swiglu(h, ffn_w_in[l, 0], ffn_w_out[l, 0]), None)
            ffn2 = lambda h, l=l: (swiglu(h, ffn_w_in[l, 1], ffn_w_out[l, 1]), None)
            if l % 2 == 0:
                mix = lambda h, i=i, ctx=ctx: mixer_ab(h, ab_w_in[i], ab_w_out[i], ret_decay_fwd[i],
                                                       ret_decay_bwd[i], ret_gn[i], gqa_q_norm[i],
                                                       gqa_k_norm[i], ctx)
            else:
                mix = lambda h, i=i, ctx=ctx: mixer_na(h, na_w_qkv[i], na_w_out[i], na_rpb[i], ctx)
            x, _ = adaln_sublayer(x, mod, 0, norm_pre[l, 0], norm_post[l, 0], ffn1, 0.5)
            x, aux = adaln_sublayer(x, mod, 1, norm_pre[l, 1], norm_post[l, 1], mix, 1.0)
            x, _ = adaln_sublayer(x, mod, 2, norm_pre[l, 2], norm_post[l, 2], ffn2, 0.5)
            aux_out.append(aux)
        return x, aux_out

    y_prompt, ctx_aux = trunk(x_prompt, c_ctx[None, :], None)

    ctx_layers = []
    for l in range(DEPTH):
        i = l // 2
        if l % 2 == 0:
            ctx_layers.append((cache_gqa_k[:, i], cache_gqa_v[:, i], state_ret_fwd[:, i], state_ret_bwd[:, i]))
        else:
            ctx_layers.append((cache_na_k[:, i], cache_na_v[:, i]))
    y_sample, _ = trunk(x_sample, c, ctx_layers)

    new_gqa_k = jnp.stack([ctx_aux[l][0] for l in range(0, DEPTH, 2)], axis=1)
    new_gqa_v = jnp.stack([ctx_aux[l][1] for l in range(0, DEPTH, 2)], axis=1)
    new_ret_fwd = jnp.stack([ctx_aux[l][2] for l in range(0, DEPTH, 2)], axis=1)
    new_ret_bwd = jnp.stack([ctx_aux[l][3] for l in range(0, DEPTH, 2)], axis=1)
    new_na_k = jnp.stack([ctx_aux[l][0] for l in range(1, DEPTH, 2)], axis=1)
    new_na_v = jnp.stack([ctx_aux[l][1] for l in range(1, DEPTH, 2)], axis=1)
    return (y_prompt, y_sample, new_gqa_k, new_gqa_v, new_ret_fwd, new_ret_bwd, new_na_k, new_na_v)
```

```python
import functools

import numpy as np
import jax
import jax.numpy as jnp
from jax import lax
from jax.experimental import pallas as pl
from jax.experimental.pallas import tpu as pltpu

F32 = jnp.float32
BF16 = jnp.bfloat16

GRID_W = 64
N_SUB = 3
HEAD_DIM = 64
N_RET = 8
RET_DK = 64
RET_DV = 64
RET_GN_EPS = 1e-5
N_Q = 8
N_KV = 2
Q_GROUP = N_Q // N_KV
ROPE_THETA = 10000.0
N_NA = 16
NA_KR_MAX = 8
NA_KC = 16
EPS = 1e-6
RET_W = N_RET * RET_DK
RET_VW = N_RET * RET_DV
GQA_W = N_Q * HEAD_DIM
KV_W = N_KV * HEAD_DIM
NA_W = N_NA * HEAD_DIM

LANES = 128
VMEM_LIMIT = 56 * 1024 * 1024
RET_CHUNK = 128
NA_ROWS_PER_STEP = 4
MASK_VALUE = -1e30


def _cparams(n_grid):
    return pltpu.CompilerParams(dimension_semantics=("parallel",) * n_grid,
                                vmem_limit_bytes=VMEM_LIMIT)


def _const_spec(shape):
    nd = len(shape)
    return pl.BlockSpec(shape, lambda *_: (0,) * nd, pipeline_mode=pl.Buffered(1))


def _silu(x):
    return x * jax.nn.sigmoid(x)


def _rms(x, g):
    ms = jnp.mean(x * x, axis=-1, keepdims=True)
    return x * lax.rsqrt(ms + EPS) * g


def _modulated(x, m, sub, g_pre):
    shift = m[3 * sub:3 * sub + 1]
    scale = m[3 * sub + 1:3 * sub + 2]
    return _rms(x, g_pre) * (1.0 + scale) + shift


def _dot(a, b):
    return jnp.dot(a, b, preferred_element_type=F32)


def _dot_nt(a, b):
    return lax.dot_general(a, b, (((1,), (1,)), ((), ())), preferred_element_type=F32)


def _dot_tn(a, b):
    return lax.dot_general(a, b, (((0,), (0,)), ((), ())), preferred_element_type=F32)


def _low_half(shape):
    return lax.broadcasted_iota(jnp.int32, shape, len(shape) - 1) < HEAD_DIM


def _mod_body(c_ref, w_ref, b_ref, o_ref):
    a = _silu(c_ref[...]).astype(BF16)
    o_ref[0] = _dot(a, w_ref[0].astype(BF16)) + b_ref[0]


def _modulation(cond, mod_w, mod_b):
    depth, d, n = mod_w.shape
    rows = cond.shape[0]
    tn = 1152 if n % 1152 == 0 else n
    return pl.pallas_call(
        _mod_body,
        grid=(depth, n // tn),
        in_specs=[pl.BlockSpec((rows, d), lambda l, j: (0, 0)),
                  pl.BlockSpec((1, d, tn), lambda l, j: (l, 0, j)),
                  pl.BlockSpec((1, 1, tn), lambda l, j: (l, 0, j))],
        out_specs=pl.BlockSpec((1, rows, tn), lambda l, j: (l, 0, j)),
        out_shape=jax.ShapeDtypeStruct((depth, rows, n), F32),
        compiler_params=_cparams(2),
        name="modulation",
    )(cond, mod_w, mod_b.reshape(depth, 1, n))


def _tok_spec(tm, width):
    return pl.BlockSpec((1, tm, width), lambda b, i: (b, i, 0))


def _mod_spec(mod):
    nm, rows, d = mod.shape
    if nm == 1:
        return pl.BlockSpec((1, rows, d), lambda b, i: (0, 0, 0))
    return pl.BlockSpec((1, rows, d), lambda b, i: (b, 0, 0))


def _token_tile(seq_len):
    return 512 if seq_len % 512 == 0 else seq_len


def _ffn_body(x_ref, mod_ref, gpre_ref, gpost_ref, win_ref, wout_ref, o_ref, *, sub, dff):
    x = x_ref[0]
    m = mod_ref[0]
    h = _modulated(x, m, sub, gpre_ref[...]).astype(BF16)
    gu = _dot(h, win_ref[...])
    g = gu[:, :dff]
    u = gu[:, dff:]
    a = (_silu(g) * u).astype(BF16)
    y = _dot(a, wout_ref[...])
    gate = m[3 * sub + 2:3 * sub + 3]
    o_ref[0] = x + (0.5 * gate) * _rms(y, gpost_ref[...])


def _ffn(x, mod, sub, g_pre, g_post, w_in, w_out):
    b, seq_len, d = x.shape
    dff = w_out.shape[0]
    tm = _token_tile(seq_len)
    return pl.pallas_call(
        functools.partial(_ffn_body, sub=sub, dff=dff),
        grid=(b, seq_len // tm),
        in_specs=[_tok_spec(tm, d), _mod_spec(mod), _const_spec((1, d)), _const_spec((1, d)),
                  _const_spec(w_in.shape), _const_spec(w_out.shape)],
        out_specs=_tok_spec(tm, d),
        out_shape=jax.ShapeDtypeStruct(x.shape, F32),
        compiler_params=_cparams(2),
        name="ffn_half",
    )(x, mod, g_pre.reshape(1, d), g_post.reshape(1, d), w_in, w_out)


def _head_rms(a, g, bd):
    a2 = a * a
    hi = a2.astype(BF16)
    lo = (a2 - hi.astype(F32)).astype(BF16)
    ss = _dot(hi, bd) + _dot(lo, bd)
    return a * lax.rsqrt(ss * (1.0 / HEAD_DIM) + EPS) * g


def _rope(a, cos, sin):
    quarter = HEAD_DIM // 4
    outs = []
    for j in range(a.shape[1] // LANES):
        aj = a[:, j * LANES:(j + 1) * LANES]
        up = pltpu.roll(aj, LANES - quarter, axis=1)
        dn = pltpu.roll(aj, quarter, axis=1)
        lane = lax.broadcasted_iota(jnp.int32, aj.shape, 1)
        partner = jnp.where((lane % (2 * quarter)) < quarter, up, dn)
        outs.append(aj * cos + partner * sin)
    return outs[0] if len(outs) == 1 else jnp.concatenate(outs, axis=1)


def _ab_in_body(*refs, rope):
    if rope:
        (x_ref, mod_ref, gpre_ref, w_ref, qn_ref, kn_ref, bd_ref, cos_ref, sin_ref,
         rq_o, rk_o, rv_o, sg_o, gq_o, gk_o, gv_o) = refs
    else:
        (x_ref, mod_ref, gpre_ref, w_ref, qn_ref, kn_ref, bd_ref,
         rq_o, rk_o, rv_o, sg_o, gq_o, gk_o, gv_o) = refs
    h = _modulated(x_ref[0], mod_ref[0], 1, gpre_ref[...]).astype(BF16)
    p = _dot(h, w_ref[...])
    o = 0
    rq_o[0] = p[:, o:o + RET_W].astype(rq_o.dtype)
    o += RET_W
    rk_o[0] = (p[:, o:o + RET_W] * (RET_DK ** -0.5)).astype(rk_o.dtype)
    o += RET_W
    rv_o[0] = p[:, o:o + RET_VW].astype(rv_o.dtype)
    o += RET_VW
    sg_o[0] = _silu(p[:, o:o + RET_VW]).astype(sg_o.dtype)
    o += RET_VW
    gq = p[:, o:o + GQA_W]
    o += GQA_W
    gk = p[:, o:o + KV_W]
    o += KV_W
    gv = p[:, o:o + KV_W]
    bd = bd_ref[...]
    qh = _head_rms(gq, qn_ref[...], bd)
    kh = _head_rms(gk, kn_ref[...], bd[:KV_W, :KV_W])
    if rope:
        qh = _rope(qh, cos_ref[...], sin_ref[...])
        kh = _rope(kh, cos_ref[...], sin_ref[...])
    gq_o[0] = (qh * (HEAD_DIM ** -0.5)).astype(gq_o.dtype)
    gk_o[0] = kh.astype(gk_o.dtype)
    gv_o[0] = gv.astype(gv_o.dtype)


def _ab_in(x, mod, g_pre, w, qn, kn, rope_tables, kv_dtype):
    b, seq_len, d = x.shape
    tm = _token_tile(seq_len)
    rope = rope_tables is not None
    bd = jnp.asarray(np.kron(np.eye(N_Q), np.ones((HEAD_DIM, HEAD_DIM))), BF16)
    ins = [x, mod, g_pre.reshape(1, d), w, jnp.tile(qn, N_Q).reshape(1, GQA_W),
           jnp.tile(kn, N_KV).reshape(1, KV_W), bd]
    in_specs = [_tok_spec(tm, d), _mod_spec(mod), _const_spec((1, d)), _const_spec(w.shape),
                _const_spec((1, GQA_W)), _const_spec((1, KV_W)), _const_spec(bd.shape)]
    if rope:
        ins += list(rope_tables)
        in_specs += [pl.BlockSpec((tm, LANES), lambda b, i: (i, 0))] * 2
    widths = (RET_W, RET_W, RET_VW, RET_VW, GQA_W, KV_W, KV_W)
    dtypes = (BF16, BF16, BF16, BF16, BF16, kv_dtype, kv_dtype)
    return pl.pallas_call(
        functools.partial(_ab_in_body, rope=rope),
        grid=(b, seq_len // tm),
        in_specs=in_specs,
        out_specs=[_tok_spec(tm, wd) for wd in widths],
        out_shape=[jax.ShapeDtypeStruct((b, seq_len, wd), dt) for wd, dt in zip(widths, dtypes)],
        compiler_params=_cparams(2),
        name="ab_in_proj",
    )(*ins)


def _rope_tables(seq_len):
    t = jnp.arange(seq_len)
    quarter = HEAD_DIM // 4
    inv = ROPE_THETA ** (-jnp.arange(quarter, dtype=F32) / quarter)
    ang_r = (t // GRID_W).astype(F32)[:, None] * inv[None, :]
    ang_c = (t % GRID_W).astype(F32)[:, None] * inv[None, :]
    cos = jnp.concatenate([jnp.cos(ang_r)] * 2 + [jnp.cos(ang_c)] * 2, axis=1)
    sin = jnp.concatenate([-jnp.sin(ang_r), jnp.sin(ang_r), -jnp.sin(ang_c), jnp.sin(ang_c)], axis=1)
    return jnp.tile(cos, (1, LANES // HEAD_DIM)), jnp.tile(sin, (1, LANES // HEAD_DIM))


def _ret_body(lgf_ref, lgb_ref, rq_ref, rk_ref, rv_ref, sg_ref, gn_ref, sf_ref, sb_ref,
              y_ref, ff_ref, fb_ref, sbs_ref, *, chunk, n_chunks):
    pair = pl.program_id(1)
    c = chunk
    lo_row = _low_half((1, LANES))
    lgf = jnp.where(lo_row, lgf_ref[2 * pair], lgf_ref[2 * pair + 1])
    lgb = jnp.where(lo_row, lgb_ref[2 * pair], lgb_ref[2 * pair + 1])
    idx = lax.broadcasted_iota(jnp.int32, (c, 1), 0).astype(F32)
    qdec_f = jnp.exp(lgf * (idx + 1.0))
    kdec_f = jnp.exp(lgf * (c - 1.0 - idx))
    cdec_f = jnp.exp(lgf * float(c))
    qdec_b = jnp.exp(lgb * (c - idx))
    kdec_b = jnp.exp(lgb * idx)
    cdec_b = jnp.exp(lgb * float(c))
    diff = (lax.broadcasted_iota(jnp.int32, (c, c), 0)
            - lax.broadcasted_iota(jnp.int32, (c, c), 1)).astype(F32)
    dmat = [jnp.exp(jnp.where(diff >= 0, lgf_ref[2 * pair + h], -lgb_ref[2 * pair + h]) * diff)
            for h in range(2)]
    rr = lax.broadcasted_iota(jnp.int32, (LANES, LANES), 0) < HEAD_DIM
    cc = lax.broadcasted_iota(jnp.int32, (LANES, LANES), 1) < HEAD_DIM
    same_head = rr == cc
    lo = _low_half((c, LANES))

    def load(ref, n):
        return ref[0, pl.ds(pl.multiple_of(n * c, c), c), :]

    def bwd_step(t, state):
        n = n_chunks - 1 - t
        sbs_ref[n] = state
        k = load(rk_ref, n).astype(F32)
        v = load(rv_ref, n).astype(BF16)
        kv = _dot_tn((k * kdec_b).astype(BF16), v)
        return state * cdec_b + jnp.where(same_head, kv, 0.0)

    fb_ref[0, 0] = lax.fori_loop(0, n_chunks, bwd_step, sb_ref[0, 0])

    gn = gn_ref[...]

    def fwd_step(n, state):
        q = load(rq_ref, n).astype(BF16)
        k = load(rk_ref, n).astype(BF16)
        v = load(rv_ref, n).astype(BF16)
        qf = q.astype(F32)
        o = (_dot((qf * qdec_f).astype(BF16), state.astype(BF16))
             + _dot((qf * qdec_b).astype(BF16), sbs_ref[n].astype(BF16)))
        zero = jnp.zeros_like(q)
        intra = []
        for h in range(2):
            qh = jnp.where(lo, q, zero) if h == 0 else jnp.where(lo, zero, q)
            s = _dot_nt(qh, k) * dmat[h]
            intra.append(_dot(s.astype(BF16), v))
        o = o + jnp.where(lo, intra[0], intra[1])
        def head_mean(a):
            m0 = jnp.sum(jnp.where(lo, a, 0.0), axis=-1, keepdims=True)
            m1 = jnp.sum(jnp.where(lo, 0.0, a), axis=-1, keepdims=True)
            return jnp.where(lo, m0, m1) * (1.0 / RET_DV)
        dev = o - head_mean(o)
        var = head_mean(dev * dev)
        y = dev * lax.rsqrt(var + RET_GN_EPS) * gn * load(sg_ref, n).astype(F32)
        y_ref[0, pl.ds(pl.multiple_of(n * c, c), c), :] = y.astype(y_ref.dtype)
        kv = _dot_tn((k.astype(F32) * kdec_f).astype(BF16), v)
        return state * cdec_f + jnp.where(same_head, kv, 0.0)

    ff_ref[0, 0] = lax.fori_loop(0, n_chunks, fwd_step, sf_ref[0, 0])


def _pair_states(s):
    b, nh, dk, dv = s.shape
    s = s.reshape(b, nh // 2, 2, dk, dv)
    z = jnp.zeros_like(s[:, :, 0])
    top = jnp.concatenate([s[:, :, 0], z], axis=-1)
    bot = jnp.concatenate([z, s[:, :, 1]], axis=-1)
    return jnp.concatenate([top, bot], axis=-2)


def _unpair_states(s):
    b, npair = s.shape[:2]
    a = s[:, :, :RET_DK, :RET_DV]
    d = s[:, :, RET_DK:, RET_DV:]
    return jnp.stack([a, d], axis=2).reshape(b, 2 * npair, RET_DK, RET_DV)


def _retention(rq, rk, rv, sg, gn, lg_f, lg_b, s_f, s_b):
    b, seq_len, _ = rq.shape
    c = RET_CHUNK if seq_len % RET_CHUNK == 0 else seq_len
    nc = seq_len // c
    npair = N_RET // 2
    seq_spec = pl.BlockSpec((1, seq_len, LANES), lambda i, p: (i, 0, p))
    st_spec = pl.BlockSpec((1, 1, LANES, LANES), lambda i, p: (i, p, 0, 0))
    smem = pl.BlockSpec(memory_space=pltpu.SMEM)
    st_shape = jax.ShapeDtypeStruct((b, npair, LANES, LANES), F32)
    y, fin_f, fin_b = pl.pallas_call(
        functools.partial(_ret_body, chunk=c, n_chunks=nc),
        grid=(b, npair),
        in_specs=[smem, smem, seq_spec, seq_spec, seq_spec, seq_spec,
                  pl.BlockSpec((1, LANES), lambda i, p: (0, p)), st_spec, st_spec],
        out_specs=[seq_spec, st_spec, st_spec],
        out_shape=[jax.ShapeDtypeStruct((b, seq_len, RET_VW), BF16), st_shape, st_shape],
        scratch_shapes=[pltpu.VMEM((nc, LANES, LANES), F32)],
        compiler_params=_cparams(2),
        name="retention",
    )(lg_f, lg_b, rq, rk, rv, sg, gn.reshape(1, RET_VW), _pair_states(s_f), _pair_states(s_b))
    return y, _unpair_states(fin_f), _unpair_states(fin_b)


def _softmax_pv(s_list, v_list):
    m = s_list[0].max(axis=-1, keepdims=True)
    for s in s_list[1:]:
        m = jnp.maximum(m, s.max(axis=-1, keepdims=True))
    acc = None
    den = None
    for s, v in zip(s_list, v_list):
        p = jnp.exp(s - m)
        d = p.sum(axis=-1, keepdims=True)
        o = _dot(p.astype(BF16), v)
        acc = o if acc is None else acc + o
        den = d if den is None else den + d
    return acc * (1.0 / den)


def _attn_body(q_ref, k_ref, v_ref, o_ref, *, n_groups, kv_shared):
    tq = q_ref.shape[1]
    lo = _low_half((tq, LANES))
    for j in range(n_groups):
        sl = slice(j * LANES, (j + 1) * LANES)
        q = q_ref[0, :, sl].astype(BF16)
        k = (k_ref[0] if kv_shared else k_ref[0, :, sl]).astype(BF16)
        v = (v_ref[0] if kv_shared else v_ref[0, :, sl]).astype(BF16)
        zero = jnp.zeros_like(q)
        outs = []
        for h in range(2):
            qh = jnp.where(lo, q, zero) if h == 0 else jnp.where(lo, zero, q)
            outs.append(_softmax_pv([_dot_nt(qh, k)], [v]))
        o_ref[0, :, sl] = jnp.where(lo, outs[0], outs[1]).astype(o_ref.dtype)


def _attention(q, k, v, kv_shared, tq):
    b, lq, wq = q.shape
    lk, wk = k.shape[1:]
    return pl.pallas_call(
        functools.partial(_attn_body, n_groups=wq // LANES, kv_shared=kv_shared),
        grid=(b, lq // tq),
        in_specs=[pl.BlockSpec((1, tq, wq), lambda i, t: (i, t, 0)),
                  pl.BlockSpec((1, lk, wk), lambda i, t: (i, 0, 0)),
                  pl.BlockSpec((1, lk, wk), lambda i, t: (i, 0, 0))],
        out_specs=pl.BlockSpec((1, tq, wq), lambda i, t: (i, t, 0)),
        out_shape=jax.ShapeDtypeStruct((b, lq, wq), BF16),
        compiler_params=_cparams(2),
        name="dense_attention",
    )(q, k, v)


def _na_plan(rows):
    g = NA_ROWS_PER_STEP
    kr = min(NA_KR_MAX, rows)
    u = min(g + kr - 1, rows)
    n_steps = rows // g
    bases, vids, variants, keys = [], [], [], {}
    for s in range(n_steps):
        base = int(np.clip(s * g - kr // 2, 0, rows - u))
        r = s * g + np.arange(g)[:, None]
        krow = base + np.arange(u)[None, :]
        r0 = np.clip(r - kr // 2, 0, rows - kr)
        valid = (krow >= r0) & (krow < r0 + kr)
        dr = np.where(valid, krow - r + (NA_KR_MAX - 1), 0)
        key = (dr.tobytes(), valid.tobytes())
        if key not in keys:
            keys[key] = len(variants)
            variants.append((dr, valid))
        bases.append(base)
        vids.append(keys[key])
    return u, bases, vids, variants


def _na_bias_tables(rpb, variants, u):
    g = NA_ROWS_PER_STEP
    cols = np.arange(GRID_W)
    c0 = np.clip(cols - NA_KC // 2, 0, GRID_W - NA_KC)
    kc = np.arange(GRID_W)[None, :]
    cvalid = (kc >= c0[:, None]) & (kc < c0[:, None] + NA_KC)
    dc = np.where(cvalid, kc - cols[:, None] + (NA_KC - 1), 0)
    tables = []
    for dr, rvalid in variants:
        dr_full = np.broadcast_to(dr[:, None, :, None], (g, GRID_W, u, GRID_W))
        dc_full = np.broadcast_to(dc[None, :, None, :], (g, GRID_W, u, GRID_W))
        ok = rvalid[:, None, :, None] & cvalid[None, :, None, :]
        vals = rpb[:, dr_full.reshape(g * GRID_W, u * GRID_W), dc_full.reshape(g * GRID_W, u * GRID_W)]
        tables.append(jnp.where(ok.reshape(1, g * GRID_W, u * GRID_W), vals.astype(F32), MASK_VALUE))
    return jnp.stack(tables)


def _na_body(base_ref, vid_ref, q_ref, k_ref, v_ref, kc_ref, vc_ref, bias_ref, o_ref, *, n_steps, tq, tk):
    lo = _low_half((tq, LANES))
    kc = kc_ref[0].astype(BF16)
    vc = vc_ref[0].astype(BF16)

    def step(s, carry):
        q = q_ref[0, pl.ds(pl.multiple_of(s * tq, tq), tq), :]
        start = pl.multiple_of(base_ref[s] * GRID_W, GRID_W)
        k = k_ref[0, pl.ds(start, tk), :]
        v = v_ref[0, pl.ds(start, tk), :]
        vid = vid_ref[s]
        zero = jnp.zeros_like(q)
        outs = []
        for h in range(2):
            qh = jnp.where(lo, q, zero) if h == 0 else jnp.where(lo, zero, q)
            s_win = _dot_nt(qh, k) + bias_ref[vid, h]
            s_ctx = _dot_nt(qh, kc)
            outs.append(_softmax_pv([s_win, s_ctx], [v, vc]))
        o_ref[0, pl.ds(pl.multiple_of(s * tq, tq), tq), :] = jnp.where(lo, outs[0], outs[1]).astype(o_ref.dtype)
        return carry

    lax.fori_loop(0, n_steps, step, 0)


def _na_attention(q, k, v, k_ctx, v_ctx, rpb):
    b, seq_len, w = q.shape
    rows = seq_len // GRID_W
    lc = k_ctx.shape[1]
    u, bases, vids, variants = _na_plan(rows)
    bias = _na_bias_tables(rpb, variants, u)
    nv = bias.shape[0]
    tq = NA_ROWS_PER_STEP * GRID_W
    tk = u * GRID_W
    n_steps = rows // NA_ROWS_PER_STEP
    smem = pl.BlockSpec(memory_space=pltpu.SMEM)
    seq_spec = pl.BlockSpec((1, seq_len, LANES), lambda i, p: (i, 0, p))
    ctx_spec = pl.BlockSpec((1, lc, LANES), lambda i, p: (i, 0, p))
    return pl.pallas_call(
        functools.partial(_na_body, n_steps=n_steps, tq=tq, tk=tk),
        grid=(b, w // LANES),
        in_specs=[smem, smem, seq_spec, seq_spec, seq_spec, ctx_spec, ctx_spec,
                  pl.BlockSpec((nv, 2, tq, tk), lambda i, p: (0, p, 0, 0))],
        out_specs=seq_spec,
        out_shape=jax.ShapeDtypeStruct((b, seq_len, w), BF16),
        compiler_params=_cparams(2),
        name="neighbourhood_attention",
    )(jnp.asarray(bases, jnp.int32), jnp.asarray(vids, jnp.int32), q, k, v, k_ctx, v_ctx, bias)


def _na_in_body(x_ref, mod_ref, gpre_ref, w_ref, q_o, k_o, v_o):
    h = _modulated(x_ref[0], mod_ref[0], 1, gpre_ref[...]).astype(BF16)
    p = _dot(h, w_ref[...])
    q_o[0] = (p[:, :NA_W] * (HEAD_DIM ** -0.5)).astype(q_o.dtype)
    k_o[0] = p[:, NA_W:2 * NA_W].astype(k_o.dtype)
    v_o[0] = p[:, 2 * NA_W:].astype(v_o.dtype)


def _na_in(x, mod, g_pre, w, kv_dtype):
    b, seq_len, d = x.shape
    tm = _token_tile(seq_len)
    return pl.pallas_call(
        _na_in_body,
        grid=(b, seq_len // tm),
        in_specs=[_tok_spec(tm, d), _mod_spec(mod), _const_spec((1, d)), _const_spec(w.shape)],
        out_specs=[_tok_spec(tm, NA_W)] * 3,
        out_shape=[jax.ShapeDtypeStruct((b, seq_len, NA_W), dt) for dt in (BF16, kv_dtype, kv_dtype)],
        compiler_params=_cparams(2),
        name="na_in_proj",
    )(x, mod, g_pre.reshape(1, d), w)


def _out_body(*refs, n_seg):
    x_ref, mod_ref, gpost_ref = refs[:3]
    seg_refs = refs[3:3 + n_seg]
    w_refs = refs[3 + n_seg:3 + 2 * n_seg]
    o_ref = refs[3 + 2 * n_seg]
    y = None
    for s_ref, w_ref in zip(seg_refs, w_refs):
        part = _dot(s_ref[0].astype(BF16), w_ref[...])
        y = part if y is None else y + part
    gate = mod_ref[0][5:6]
    o_ref[0] = x_ref[0] + gate * _rms(y, gpost_ref[...])


def _mixer_out(x, mod, g_post, segs, ws):
    b, seq_len, d = x.shape
    tm = _token_tile(seq_len)
    n_seg = len(segs)
    return pl.pallas_call(
        functools.partial(_out_body, n_seg=n_seg),
        grid=(b, seq_len // tm),
        in_specs=([_tok_spec(tm, d), _mod_spec(mod), _const_spec((1, d))]
                  + [_tok_spec(tm, s.shape[-1]) for s in segs]
                  + [_const_spec(w.shape) for w in ws]),
        out_specs=_tok_spec(tm, d),
        out_shape=jax.ShapeDtypeStruct(x.shape, F32),
        compiler_params=_cparams(2),
        name="mixer_out_proj",
    )(x, mod, g_post.reshape(1, d), *segs, *ws)


_Q_HEAD_ORDER = [h for j in range(Q_GROUP) for h in range(j, N_Q, Q_GROUP)]


def _permute_q_cols(w_in):
    cols = np.arange(w_in.shape[1])
    q0 = 2 * RET_W + 2 * RET_VW
    perm = np.concatenate([np.arange(h * HEAD_DIM, (h + 1) * HEAD_DIM) for h in _Q_HEAD_ORDER])
    cols[q0:q0 + GQA_W] = q0 + perm
    return w_in[:, cols]


def _permute_att_rows(w_att):
    perm = np.concatenate([np.arange(h * HEAD_DIM, (h + 1) * HEAD_DIM) for h in _Q_HEAD_ORDER])
    return w_att[perm]


def _trunk(x, mods, ctx_layers, wts):
    b, seq_len, d = x.shape
    is_ctx = ctx_layers is None
    depth = len(mods)
    flat = (lambda a: a.reshape(1, b * seq_len, a.shape[-1])) if is_ctx else (lambda a: a)
    unflat = (lambda a: a.reshape(b, seq_len, a.shape[-1])) if is_ctx else (lambda a: a)
    kv_dtype = F32 if is_ctx else BF16
    aux = []
    x = flat(x)
    for l in range(depth):
        mod = mods[l]
        i = l // 2
        x = _ffn(x, mod, 0, wts["norm_pre"][l, 0], wts["norm_post"][l, 0],
                 wts["ffn_w_in"][l][0], wts["ffn_w_out"][l][0])
        if l % 2 == 0:
            rope_tables = None if is_ctx else _rope_tables(seq_len)
            rq, rk, rv, sg, gq, gk, gv = _ab_in(x, mod, wts["norm_pre"][l, 1], wts["ab_w_in"][i],
                                               wts["gqa_q_norm"][i], wts["gqa_k_norm"][i],
                                               rope_tables, kv_dtype)
            rq, rk, rv, sg, gq, gk, gv = [unflat(a) for a in (rq, rk, rv, sg, gq, gk, gv)]
            if is_ctx:
                s_f = jnp.zeros((b, N_RET, RET_DK, RET_DV), F32)
                s_b = s_f
                k_all, v_all = gk, gv
            else:
                ck, cv, s_f, s_b = ctx_layers[l]
                lc = ck.shape[1]
                k_all = jnp.concatenate([gk, ck.reshape(b, lc, KV_W).astype(BF16)], axis=1)
                v_all = jnp.concatenate([gv, cv.reshape(b, lc, KV_W).astype(BF16)], axis=1)
            y_ret, fin_f, fin_b = _retention(rq, rk, rv, sg, wts["ret_gn"][i], wts["lg_f"][i],
                                             wts["lg_b"][i], s_f.astype(F32), s_b.astype(F32))
            y_att = _attention(gq, k_all, v_all, True, min(seq_len, 256))
            x = _mixer_out(x, mod, wts["norm_post"][l, 1], [flat(y_ret), flat(y_att)],
                           [wts["ab_w_out_ret"][i], wts["ab_w_out_att"][i]])
            aux.append((gk, gv, fin_f, fin_b) if is_ctx else None)
        else:
            q, k, v = _na_in(x, mod, wts["norm_pre"][l, 1], wts["na_w_qkv"][i], kv_dtype)
            q, k, v = unflat(q), unflat(k), unflat(v)
            if is_ctx:
                y = _attention(q, k, v, False, seq_len)
                aux.append((k, v))
            else:
                ck, cv = ctx_layers[l]
                lc = ck.shape[1]
                y = _na_attention(q, k, v, ck.reshape(b, lc, NA_W).astype(BF16),
                                  cv.reshape(b, lc, NA_W).astype(BF16), wts["na_rpb"][i])
                aux.append(None)
            x = _mixer_out(x, mod, wts["norm_post"][l, 1], [flat(y)], [wts["na_w_out"][i]])
        x = _ffn(x, mod, 2, wts["norm_pre"][l, 2], wts["norm_post"][l, 2],
                 wts["ffn_w_in"][l][1], wts["ffn_w_out"][l][1])
    return unflat(x), aux


def kernel(x_prompt, x_sample, cache_gqa_k, cache_gqa_v, state_ret_fwd, state_ret_bwd, cache_na_k,
           cache_na_v, c, c_ctx, mod_w, mod_b, norm_pre, norm_post, ffn_w_in, ffn_w_out, ab_w_in,
           ab_w_out, ret_decay_fwd, ret_decay_bwd, ret_gn, gqa_q_norm, gqa_k_norm, na_w_qkv, na_w_out,
           na_rpb):
    depth = mod_w.shape[0]
    d = x_prompt.shape[-1]
    n_dec = c.shape[0]
    batch, seq = x_prompt.shape[:2]

    n_cond = n_dec + 1
    pad = (-n_cond) % 8
    cond = jnp.concatenate([c, c_ctx[None, :], jnp.zeros((pad, d), F32)], axis=0)
    mod_all = _modulation(cond, mod_w, mod_b).reshape(depth, n_cond + pad, 3 * N_SUB, d)
    mods_sample = [mod_all[l, :n_dec] for l in range(depth)]
    mods_prompt = [mod_all[l, n_dec:n_dec + 1] for l in range(depth)]

    n_even = ab_w_in.shape[0]
    wts = {
        "norm_pre": norm_pre, "norm_post": norm_post,
        "ffn_w_in": [[ffn_w_in[l, s].astype(BF16) for s in range(2)] for l in range(depth)],
        "ffn_w_out": [[ffn_w_out[l, s].astype(BF16) for s in range(2)] for l in range(depth)],
        "ab_w_in": [_permute_q_cols(ab_w_in[i]).astype(BF16) for i in range(n_even)],
        "ab_w_out_ret": [ab_w_out[i, :RET_VW].astype(BF16) for i in range(n_even)],
        "ab_w_out_att": [_permute_att_rows(ab_w_out[i, RET_VW:]).astype(BF16) for i in range(n_even)],
        "lg_f": jax.nn.log_sigmoid(ret_decay_fwd.astype(F32)),
        "lg_b": jax.nn.log_sigmoid(ret_decay_bwd.astype(F32)),
        "ret_gn": ret_gn, "gqa_q_norm": gqa_q_norm, "gqa_k_norm": gqa_k_norm,
        "na_w_qkv": [na_w_qkv[i].astype(BF16) for i in range(na_w_qkv.shape[0])],
        "na_w_out": [na_w_out[i].astype(BF16) for i in range(na_w_out.shape[0])],
        "na_rpb": na_rpb,
    }

    y_prompt, ctx_aux = _trunk(x_prompt, mods_prompt, None, wts)

    ctx_layers = []
    for l in range(depth):
        i = l // 2
        if l % 2 == 0:
            ctx_layers.append((cache_gqa_k[:, i], cache_gqa_v[:, i], state_ret_fwd[:, i], state_ret_bwd[:, i]))
        else:
            ctx_layers.append((cache_na_k[:, i], cache_na_v[:, i]))
    y_sample, _ = _trunk(x_sample, mods_sample, ctx_layers, wts)

    def stack(idx, layers, tail):
        return jnp.stack([ctx_aux[l][idx].reshape((batch,) + tail) for l in layers], axis=1)

    even = range(0, depth, 2)
    odd = range(1, depth, 2)
    new_gqa_k = stack(0, even, (seq, N_KV, HEAD_DIM))
    new_gqa_v = stack(1, even, (seq, N_KV, HEAD_DIM))
    new_ret_fwd = stack(2, even, (N_RET, RET_DK, RET_DV))
    new_ret_bwd = stack(3, even, (N_RET, RET_DK, RET_DV))
    new_na_k = stack(0, odd, (seq, N_NA, HEAD_DIM))
    new_na_v = stack(1, odd, (seq, N_NA, HEAD_DIM))
    return (y_prompt, y_sample, new_gqa_k, new_gqa_v, new_ret_fwd, new_ret_bwd, new_na_k, new_na_v)
```

```python
import functools

import numpy as np
import jax
import jax.numpy as jnp
from jax import lax
from jax.experimental import pallas as pl
from jax.experimental.pallas import tpu as pltpu

F32 = jnp.float32
BF16 = jnp.bfloat16

GRID_W = 64
N_SUB = 3
HEAD_DIM = 64
N_RET = 8
RET_DK = 64
RET_DV = 64
RET_GN_EPS = 1e-5
N_Q = 8
N_KV = 2
Q_GROUP = N_Q // N_KV
ROPE_THETA = 10000.0
N_NA = 16
NA_KR_MAX = 8
NA_KC = 16
EPS = 1e-6
RET_W = N_RET * RET_DK
RET_VW = N_RET * RET_DV
GQA_W = N_Q * HEAD_DIM
KV_W = N_KV * HEAD_DIM
NA_W = N_NA * HEAD_DIM

LANES = 128
VMEM_LIMIT = 56 * 1024 * 1024
RET_CHUNK = 128
RET_UNROLL = 4
NA_ROWS_PER_STEP = 4
MASK_VALUE = -1e30


def _cparams(n_grid):
    return pltpu.CompilerParams(dimension_semantics=("parallel",) * n_grid,
                                vmem_limit_bytes=VMEM_LIMIT)


def _const_spec(shape):
    nd = len(shape)
    return pl.BlockSpec(shape, lambda *_: (0,) * nd, pipeline_mode=pl.Buffered(1))


def _silu(x):
    return x * jax.nn.sigmoid(x)


def _rms(x, g):
    ms = jnp.mean(x * x, axis=-1, keepdims=True)
    return x * lax.rsqrt(ms + EPS) * g


def _modulated(x, m, sub, g_pre):
    shift = m[3 * sub:3 * sub + 1]
    scale = m[3 * sub + 1:3 * sub + 2]
    return _rms(x, g_pre) * (1.0 + scale) + shift


def _dot(a, b):
    return jnp.dot(a, b, preferred_element_type=F32)


def _dot_nt(a, b):
    return lax.dot_general(a, b, (((1,), (1,)), ((), ())), preferred_element_type=F32)


def _dot_tn(a, b):
    return lax.dot_general(a, b, (((0,), (0,)), ((), ())), preferred_element_type=F32)


def _low_half(shape):
    return lax.broadcasted_iota(jnp.int32, shape, len(shape) - 1) < HEAD_DIM


def _mod_body(c_ref, w_ref, b_ref, o_ref):
    a = _silu(c_ref[...]).astype(BF16)
    o_ref[0] = _dot(a, w_ref[0].astype(BF16)) + b_ref[0]


def _modulation(cond, mod_w, mod_b):
    depth, d, n = mod_w.shape
    rows = cond.shape[0]
    tn = 1152 if n % 1152 == 0 else n
    return pl.pallas_call(
        _mod_body,
        grid=(depth, n // tn),
        in_specs=[pl.BlockSpec((rows, d), lambda l, j: (0, 0)),
                  pl.BlockSpec((1, d, tn), lambda l, j: (l, 0, j)),
                  pl.BlockSpec((1, 1, tn), lambda l, j: (l, 0, j))],
        out_specs=pl.BlockSpec((1, rows, tn), lambda l, j: (l, 0, j)),
        out_shape=jax.ShapeDtypeStruct((depth, rows, n), F32),
        compiler_params=_cparams(2),
        name="modulation",
    )(cond, mod_w, mod_b.reshape(depth, 1, n))


def _tok_spec(tm, width):
    return pl.BlockSpec((1, tm, width), lambda b, i: (b, i, 0))


def _mod_spec(mod):
    nm, rows, d = mod.shape
    if nm == 1:
        return pl.BlockSpec((1, rows, d), lambda b, i: (0, 0, 0))
    return pl.BlockSpec((1, rows, d), lambda b, i: (b, 0, 0))


def _token_tile(seq_len):
    return 512 if seq_len % 512 == 0 else seq_len


def _ffn_body(x_ref, mod_ref, gpre_ref, gpost_ref, win_ref, wout_ref, o_ref, *, sub, dff):
    x = x_ref[0]
    m = mod_ref[0]
    h = _modulated(x, m, sub, gpre_ref[...]).astype(BF16)
    gu = _dot(h, win_ref[...])
    g = gu[:, :dff]
    u = gu[:, dff:]
    a = (_silu(g) * u).astype(BF16)
    y = _dot(a, wout_ref[...])
    gate = m[3 * sub + 2:3 * sub + 3]
    o_ref[0] = x + (0.5 * gate) * _rms(y, gpost_ref[...])


def _ffn(x, mod, sub, g_pre, g_post, w_in, w_out):
    b, seq_len, d = x.shape
    dff = w_out.shape[0]
    tm = _token_tile(seq_len)
    return pl.pallas_call(
        functools.partial(_ffn_body, sub=sub, dff=dff),
        grid=(b, seq_len // tm),
        in_specs=[_tok_spec(tm, d), _mod_spec(mod), _const_spec((1, d)), _const_spec((1, d)),
                  _const_spec(w_in.shape), _const_spec(w_out.shape)],
        out_specs=_tok_spec(tm, d),
        out_shape=jax.ShapeDtypeStruct(x.shape, F32),
        compiler_params=_cparams(2),
        name="ffn_half",
    )(x, mod, g_pre.reshape(1, d), g_post.reshape(1, d), w_in, w_out)


def _head_rms(a, g, bd):
    a2 = a * a
    hi = a2.astype(BF16)
    lo = (a2 - hi.astype(F32)).astype(BF16)
    ss = _dot(hi, bd) + _dot(lo, bd)
    return a * lax.rsqrt(ss * (1.0 / HEAD_DIM) + EPS) * g


def _rope(a, cos, sin):
    quarter = HEAD_DIM // 4
    outs = []
    for j in range(a.shape[1] // LANES):
        aj = a[:, j * LANES:(j + 1) * LANES]
        up = pltpu.roll(aj, LANES - quarter, axis=1)
        dn = pltpu.roll(aj, quarter, axis=1)
        lane = lax.broadcasted_iota(jnp.int32, aj.shape, 1)
        partner = jnp.where((lane % (2 * quarter)) < quarter, up, dn)
        outs.append(aj * cos + partner * sin)
    return outs[0] if len(outs) == 1 else jnp.concatenate(outs, axis=1)


def _ab_in_body(*refs, rope):
    if rope:
        (x_ref, mod_ref, gpre_ref, w_ref, qn_ref, kn_ref, bd_ref, cos_ref, sin_ref,
         rq_o, rk_o, rv_o, sg_o, gq_o, gk_o, gv_o) = refs
    else:
        (x_ref, mod_ref, gpre_ref, w_ref, qn_ref, kn_ref, bd_ref,
         rq_o, rk_o, rv_o, sg_o, gq_o, gk_o, gv_o) = refs
    h = _modulated(x_ref[0], mod_ref[0], 1, gpre_ref[...]).astype(BF16)
    p = _dot(h, w_ref[...])
    o = 0
    rq_o[0] = p[:, o:o + RET_W].astype(rq_o.dtype)
    o += RET_W
    rk_o[0] = (p[:, o:o + RET_W] * (RET_DK ** -0.5)).astype(rk_o.dtype)
    o += RET_W
    rv_o[0] = p[:, o:o + RET_VW].astype(rv_o.dtype)
    o += RET_VW
    sg_o[0] = _silu(p[:, o:o + RET_VW]).astype(sg_o.dtype)
    o += RET_VW
    gq = p[:, o:o + GQA_W]
    o += GQA_W
    gk = p[:, o:o + KV_W]
    o += KV_W
    gv = p[:, o:o + KV_W]
    bd = bd_ref[...]
    qh = _head_rms(gq, qn_ref[...], bd)
    kh = _head_rms(gk, kn_ref[...], bd[:KV_W, :KV_W])
    if rope:
        qh = _rope(qh, cos_ref[...], sin_ref[...])
        kh = _rope(kh, cos_ref[...], sin_ref[...])
    gq_o[0] = (qh * (HEAD_DIM ** -0.5)).astype(gq_o.dtype)
    gk_o[0] = kh.astype(gk_o.dtype)
    gv_o[0] = gv.astype(gv_o.dtype)


def _ab_in(x, mod, g_pre, w, qn, kn, rope_tables, kv_dtype):
    b, seq_len, d = x.shape
    tm = _token_tile(seq_len)
    rope = rope_tables is not None
    bd = jnp.asarray(np.kron(np.eye(N_Q), np.ones((HEAD_DIM, HEAD_DIM))), BF16)
    ins = [x, mod, g_pre.reshape(1, d), w, jnp.tile(qn, N_Q).reshape(1, GQA_W),
           jnp.tile(kn, N_KV).reshape(1, KV_W), bd]
    in_specs = [_tok_spec(tm, d), _mod_spec(mod), _const_spec((1, d)), _const_spec(w.shape),
                _const_spec((1, GQA_W)), _const_spec((1, KV_W)), _const_spec(bd.shape)]
    if rope:
        ins += list(rope_tables)
        in_specs += [pl.BlockSpec((tm, LANES), lambda b, i: (i, 0))] * 2
    widths = (RET_W, RET_W, RET_VW, RET_VW, GQA_W, KV_W, KV_W)
    dtypes = (BF16, BF16, BF16, BF16, BF16, kv_dtype, kv_dtype)
    return pl.pallas_call(
        functools.partial(_ab_in_body, rope=rope),
        grid=(b, seq_len // tm),
        in_specs=in_specs,
        out_specs=[_tok_spec(tm, wd) for wd in widths],
        out_shape=[jax.ShapeDtypeStruct((b, seq_len, wd), dt) for wd, dt in zip(widths, dtypes)],
        compiler_params=_cparams(2),
        name="ab_in_proj",
    )(*ins)


def _rope_tables(seq_len):
    t = jnp.arange(seq_len)
    quarter = HEAD_DIM // 4
    inv = ROPE_THETA ** (-jnp.arange(quarter, dtype=F32) / quarter)
    ang_r = (t // GRID_W).astype(F32)[:, None] * inv[None, :]
    ang_c = (t % GRID_W).astype(F32)[:, None] * inv[None, :]
    cos = jnp.concatenate([jnp.cos(ang_r)] * 2 + [jnp.cos(ang_c)] * 2, axis=1)
    sin = jnp.concatenate([-jnp.sin(ang_r), jnp.sin(ang_r), -jnp.sin(ang_c), jnp.sin(ang_c)], axis=1)
    return jnp.tile(cos, (1, LANES // HEAD_DIM)), jnp.tile(sin, (1, LANES // HEAD_DIM))


def _ret_body(lgf_ref, lgb_ref, rq_ref, rk_ref, rv_ref, sg_ref, gn_ref, sf_ref, sb_ref,
              y_ref, ff_ref, fb_ref, sbs_ref, *, chunk, n_chunks):
    pair = pl.program_id(1)
    c = chunk
    lo_row = _low_half((1, LANES))
    lgf = jnp.where(lo_row, lgf_ref[2 * pair], lgf_ref[2 * pair + 1])
    lgb = jnp.where(lo_row, lgb_ref[2 * pair], lgb_ref[2 * pair + 1])
    idx = lax.broadcasted_iota(jnp.int32, (c, 1), 0).astype(F32)
    qdec_f = jnp.exp(lgf * (idx + 1.0))
    kdec_f = jnp.exp(lgf * (c - 1.0 - idx))
    cdec_f = jnp.exp(lgf * float(c))
    qdec_b = jnp.exp(lgb * (c - idx))
    kdec_b = jnp.exp(lgb * idx)
    cdec_b = jnp.exp(lgb * float(c))
    diff = (lax.broadcasted_iota(jnp.int32, (c, c), 0)
            - lax.broadcasted_iota(jnp.int32, (c, c), 1)).astype(F32)
    dmat = [jnp.exp(jnp.where(diff >= 0, lgf_ref[2 * pair + h], -lgb_ref[2 * pair + h]) * diff)
            for h in range(2)]
    rr = lax.broadcasted_iota(jnp.int32, (LANES, LANES), 0) < HEAD_DIM
    cc = lax.broadcasted_iota(jnp.int32, (LANES, LANES), 1) < HEAD_DIM
    same_head = rr == cc
    lo = _low_half((c, LANES))

    def load(ref, n):
        return ref[0, pl.ds(pl.multiple_of(n * c, c), c), :]

    def bwd_step(t, state):
        n = n_chunks - 1 - t
        sbs_ref[n] = state
        k = load(rk_ref, n).astype(F32)
        v = load(rv_ref, n).astype(BF16)
        kv = _dot_tn((k * kdec_b).astype(BF16), v)
        return state * cdec_b + jnp.where(same_head, kv, 0.0)

    unroll = int(np.gcd(n_chunks, RET_UNROLL))
    fb_ref[0, 0] = lax.fori_loop(0, n_chunks, bwd_step, sb_ref[0, 0], unroll=unroll)

    gn = gn_ref[...]

    def fwd_step(n, state):
        q = load(rq_ref, n).astype(BF16)
        k = load(rk_ref, n).astype(BF16)
        v = load(rv_ref, n).astype(BF16)
        qf = q.astype(F32)
        o = (_dot((qf * qdec_f).astype(BF16), state.astype(BF16))
             + _dot((qf * qdec_b).astype(BF16), sbs_ref[n].astype(BF16)))
        zero = jnp.zeros_like(q)
        intra = []
        for h in range(2):
            qh = jnp.where(lo, q, zero) if h == 0 else jnp.where(lo, zero, q)
            s = _dot_nt(qh, k) * dmat[h]
            intra.append(_dot(s.astype(BF16), v))
        o = o + jnp.where(lo, intra[0], intra[1])
        def head_mean(a):
            m0 = jnp.sum(jnp.where(lo, a, 0.0), axis=-1, keepdims=True)
            m1 = jnp.sum(jnp.where(lo, 0.0, a), axis=-1, keepdims=True)
            return jnp.where(lo, m0, m1) * (1.0 / RET_DV)
        dev = o - head_mean(o)
        var = head_mean(dev * dev)
        y = dev * lax.rsqrt(var + RET_GN_EPS) * gn * load(sg_ref, n).astype(F32)
        y_ref[0, pl.ds(pl.multiple_of(n * c, c), c), :] = y.astype(y_ref.dtype)
        kv = _dot_tn((k.astype(F32) * kdec_f).astype(BF16), v)
        return state * cdec_f + jnp.where(same_head, kv, 0.0)

    ff_ref[0, 0] = lax.fori_loop(0, n_chunks, fwd_step, sf_ref[0, 0], unroll=unroll)


def _pair_states(s):
    b, nh, dk, dv = s.shape
    s = s.reshape(b, nh // 2, 2, dk, dv)
    z = jnp.zeros_like(s[:, :, 0])
    top = jnp.concatenate([s[:, :, 0], z], axis=-1)
    bot = jnp.concatenate([z, s[:, :, 1]], axis=-1)
    return jnp.concatenate([top, bot], axis=-2)


def _unpair_states(s):
    b, npair = s.shape[:2]
    a = s[:, :, :RET_DK, :RET_DV]
    d = s[:, :, RET_DK:, RET_DV:]
    return jnp.stack([a, d], axis=2).reshape(b, 2 * npair, RET_DK, RET_DV)


def _retention(rq, rk, rv, sg, gn, lg_f, lg_b, s_f, s_b):
    b, seq_len, _ = rq.shape
    c = RET_CHUNK if seq_len % RET_CHUNK == 0 else seq_len
    nc = seq_len // c
    npair = N_RET // 2
    seq_spec = pl.BlockSpec((1, seq_len, LANES), lambda i, p: (i, 0, p))
    st_spec = pl.BlockSpec((1, 1, LANES, LANES), lambda i, p: (i, p, 0, 0))
    smem = pl.BlockSpec(memory_space=pltpu.SMEM)
    st_shape = jax.ShapeDtypeStruct((b, npair, LANES, LANES), F32)
    y, fin_f, fin_b = pl.pallas_call(
        functools.partial(_ret_body, chunk=c, n_chunks=nc),
        grid=(b, npair),
        in_specs=[smem, smem, seq_spec, seq_spec, seq_spec, seq_spec,
                  pl.BlockSpec((1, LANES), lambda i, p: (0, p)), st_spec, st_spec],
        out_specs=[seq_spec, st_spec, st_spec],
        out_shape=[jax.ShapeDtypeStruct((b, seq_len, RET_VW), BF16), st_shape, st_shape],
        scratch_shapes=[pltpu.VMEM((nc, LANES, LANES), F32)],
        compiler_params=_cparams(2),
        name="retention",
    )(lg_f, lg_b, rq, rk, rv, sg, gn.reshape(1, RET_VW), _pair_states(s_f), _pair_states(s_b))
    return y, _unpair_states(fin_f), _unpair_states(fin_b)


def _softmax_pv(s_list, v_list):
    m = s_list[0].max(axis=-1, keepdims=True)
    for s in s_list[1:]:
        m = jnp.maximum(m, s.max(axis=-1, keepdims=True))
    acc = None
    den = None
    for s, v in zip(s_list, v_list):
        p = jnp.exp(s - m)
        d = p.sum(axis=-1, keepdims=True)
        o = _dot(p.astype(BF16), v)
        acc = o if acc is None else acc + o
        den = d if den is None else den + d
    return acc * (1.0 / den)


def _attn_body(q_ref, k_ref, v_ref, o_ref, *, n_groups, kv_shared):
    tq = q_ref.shape[1]
    lo = _low_half((tq, LANES))
    for j in range(n_groups):
        sl = slice(j * LANES, (j + 1) * LANES)
        q = q_ref[0, :, sl].astype(BF16)
        k = (k_ref[0] if kv_shared else k_ref[0, :, sl]).astype(BF16)
        v = (v_ref[0] if kv_shared else v_ref[0, :, sl]).astype(BF16)
        zero = jnp.zeros_like(q)
        outs = []
        for h in range(2):
            qh = jnp.where(lo, q, zero) if h == 0 else jnp.where(lo, zero, q)
            outs.append(_softmax_pv([_dot_nt(qh, k)], [v]))
        o_ref[0, :, sl] = jnp.where(lo, outs[0], outs[1]).astype(o_ref.dtype)


def _attention(q, k, v, kv_shared, tq):
    b, lq, wq = q.shape
    lk, wk = k.shape[1:]
    return pl.pallas_call(
        functools.partial(_attn_body, n_groups=wq // LANES, kv_shared=kv_shared),
        grid=(b, lq // tq),
        in_specs=[pl.BlockSpec((1, tq, wq), lambda i, t: (i, t, 0)),
                  pl.BlockSpec((1, lk, wk), lambda i, t: (i, 0, 0)),
                  pl.BlockSpec((1, lk, wk), lambda i, t: (i, 0, 0))],
        out_specs=pl.BlockSpec((1, tq, wq), lambda i, t: (i, t, 0)),
        out_shape=jax.ShapeDtypeStruct((b, lq, wq), BF16),
        compiler_params=_cparams(2),
        name="dense_attention",
    )(q, k, v)


def _na_plan(rows):
    g = NA_ROWS_PER_STEP
    kr = min(NA_KR_MAX, rows)
    u = min(g + kr - 1, rows)
    n_steps = rows // g
    bases, vids, variants, keys = [], [], [], {}
    for s in range(n_steps):
        base = int(np.clip(s * g - kr // 2, 0, rows - u))
        r = s * g + np.arange(g)[:, None]
        krow = base + np.arange(u)[None, :]
        r0 = np.clip(r - kr // 2, 0, rows - kr)
        valid = (krow >= r0) & (krow < r0 + kr)
        dr = np.where(valid, krow - r + (NA_KR_MAX - 1), 0)
        key = (dr.tobytes(), valid.tobytes())
        if key not in keys:
            keys[key] = len(variants)
            variants.append((dr, valid))
        bases.append(base)
        vids.append(keys[key])
    return u, bases, vids, variants


def _na_bias_tables(rpb, variants, u):
    g = NA_ROWS_PER_STEP
    nh, n_dr, n_dc = rpb.shape
    cols = np.arange(GRID_W)
    c0 = np.clip(cols - NA_KC // 2, 0, GRID_W - NA_KC)
    kc = np.arange(GRID_W)[None, :]
    cvalid = (kc >= c0[:, None]) & (kc < c0[:, None] + NA_KC)
    dc = kc - cols[:, None] + (NA_KC - 1)
    sel_c = cvalid[None] & (dc[None] == np.arange(n_dc)[:, None, None])
    tables = []
    for dr, rvalid in variants:
        sel_r = rvalid[..., None] & (dr[..., None] == np.arange(n_dr))
        t = jnp.einsum("gud,hde,eck->hgcuk", jnp.asarray(sel_r, F32), rpb.astype(F32),
                       jnp.asarray(sel_c, F32), precision=lax.Precision.HIGHEST)
        ok = rvalid[:, None, :, None] & cvalid[None, :, None, :]
        tables.append(jnp.where(ok[None], t, MASK_VALUE).reshape(nh, g * GRID_W, u * GRID_W))
    return jnp.stack(tables)


def _na_body(base_ref, vid_ref, q_ref, k_ref, v_ref, kc_ref, vc_ref, bias_ref, o_ref, *, n_steps, tq, tk):
    lo = _low_half((tq, LANES))
    kc = kc_ref[0].astype(BF16)
    vc = vc_ref[0].astype(BF16)

    def step(s, carry):
        q = q_ref[0, pl.ds(pl.multiple_of(s * tq, tq), tq), :]
        start = pl.multiple_of(base_ref[s] * GRID_W, GRID_W)
        k = k_ref[0, pl.ds(start, tk), :]
        v = v_ref[0, pl.ds(start, tk), :]
        vid = vid_ref[s]
        zero = jnp.zeros_like(q)
        outs = []
        for h in range(2):
            qh = jnp.where(lo, q, zero) if h == 0 else jnp.where(lo, zero, q)
            s_win = _dot_nt(qh, k) + bias_ref[vid, h]
            s_ctx = _dot_nt(qh, kc)
            outs.append(_softmax_pv([s_win, s_ctx], [v, vc]))
        o_ref[0, pl.ds(pl.multiple_of(s * tq, tq), tq), :] = jnp.where(lo, outs[0], outs[1]).astype(o_ref.dtype)
        return carry

    lax.fori_loop(0, n_steps, step, 0, unroll=2 if n_steps % 2 == 0 else 1)


def _na_attention(q, k, v, k_ctx, v_ctx, rpb):
    b, seq_len, w = q.shape
    rows = seq_len // GRID_W
    lc = k_ctx.shape[1]
    u, bases, vids, variants = _na_plan(rows)
    bias = _na_bias_tables(rpb, variants, u)
    nv = bias.shape[0]
    tq = NA_ROWS_PER_STEP * GRID_W
    tk = u * GRID_W
    n_steps = rows // NA_ROWS_PER_STEP
    smem = pl.BlockSpec(memory_space=pltpu.SMEM)
    seq_spec = pl.BlockSpec((1, seq_len, LANES), lambda p, i: (i, 0, p))
    ctx_spec = pl.BlockSpec((1, lc, LANES), lambda p, i: (i, 0, p))
    return pl.pallas_call(
        functools.partial(_na_body, n_steps=n_steps, tq=tq, tk=tk),
        grid=(w // LANES, b),
        in_specs=[smem, smem, seq_spec, seq_spec, seq_spec, ctx_spec, ctx_spec,
                  pl.BlockSpec((nv, 2, tq, tk), lambda p, i: (0, p, 0, 0))],
        out_specs=seq_spec,
        out_shape=jax.ShapeDtypeStruct((b, seq_len, w), BF16),
        compiler_params=_cparams(2),
        name="neighbourhood_attention",
    )(jnp.asarray(bases, jnp.int32), jnp.asarray(vids, jnp.int32), q, k, v, k_ctx, v_ctx, bias)


def _na_in_body(x_ref, mod_ref, gpre_ref, w_ref, q_o, k_o, v_o):
    h = _modulated(x_ref[0], mod_ref[0], 1, gpre_ref[...]).astype(BF16)
    p = _dot(h, w_ref[...])
    q_o[0] = (p[:, :NA_W] * (HEAD_DIM ** -0.5)).astype(q_o.dtype)
    k_o[0] = p[:, NA_W:2 * NA_W].astype(k_o.dtype)
    v_o[0] = p[:, 2 * NA_W:].astype(v_o.dtype)


def _na_in(x, mod, g_pre, w, kv_dtype):
    b, seq_len, d = x.shape
    tm = _token_tile(seq_len)
    return pl.pallas_call(
        _na_in_body,
        grid=(b, seq_len // tm),
        in_specs=[_tok_spec(tm, d), _mod_spec(mod), _const_spec((1, d)), _const_spec(w.shape)],
        out_specs=[_tok_spec(tm, NA_W)] * 3,
        out_shape=[jax.ShapeDtypeStruct((b, seq_len, NA_W), dt) for dt in (BF16, kv_dtype, kv_dtype)],
        compiler_params=_cparams(2),
        name="na_in_proj",
    )(x, mod, g_pre.reshape(1, d), w)


def _out_body(*refs, n_seg):
    x_ref, mod_ref, gpost_ref = refs[:3]
    seg_refs = refs[3:3 + n_seg]
    w_refs = refs[3 + n_seg:3 + 2 * n_seg]
    o_ref = refs[3 + 2 * n_seg]
    y = None
    for s_ref, w_ref in zip(seg_refs, w_refs):
        part = _dot(s_ref[0].astype(BF16), w_ref[...])
        y = part if y is None else y + part
    gate = mod_ref[0][5:6]
    o_ref[0] = x_ref[0] + gate * _rms(y, gpost_ref[...])


def _mixer_out(x, mod, g_post, segs, ws):
    b, seq_len, d = x.shape
    tm = _token_tile(seq_len)
    n_seg = len(segs)
    return pl.pallas_call(
        functools.partial(_out_body, n_seg=n_seg),
        grid=(b, seq_len // tm),
        in_specs=([_tok_spec(tm, d), _mod_spec(mod), _const_spec((1, d))]
                  + [_tok_spec(tm, s.shape[-1]) for s in segs]
                  + [_const_spec(w.shape) for w in ws]),
        out_specs=_tok_spec(tm, d),
        out_shape=jax.ShapeDtypeStruct(x.shape, F32),
        compiler_params=_cparams(2),
        name="mixer_out_proj",
    )(x, mod, g_post.reshape(1, d), *segs, *ws)


_Q_HEAD_ORDER = [h for j in range(Q_GROUP) for h in range(j, N_Q, Q_GROUP)]


def _permute_q_cols(w_in):
    cols = np.arange(w_in.shape[1])
    q0 = 2 * RET_W + 2 * RET_VW
    perm = np.concatenate([np.arange(h * HEAD_DIM, (h + 1) * HEAD_DIM) for h in _Q_HEAD_ORDER])
    cols[q0:q0 + GQA_W] = q0 + perm
    return w_in[:, cols]


def _permute_att_rows(w_att):
    perm = np.concatenate([np.arange(h * HEAD_DIM, (h + 1) * HEAD_DIM) for h in _Q_HEAD_ORDER])
    return w_att[perm]


def _trunk(x, mods, ctx_layers, wts):
    b, seq_len, d = x.shape
    is_ctx = ctx_layers is None
    depth = len(mods)
    flat = (lambda a: a.reshape(1, b * seq_len, a.shape[-1])) if is_ctx else (lambda a: a)
    unflat = (lambda a: a.reshape(b, seq_len, a.shape[-1])) if is_ctx else (lambda a: a)
    kv_dtype = F32 if is_ctx else BF16
    aux = []
    x = flat(x)
    for l in range(depth):
        mod = mods[l]
        i = l // 2
        x = _ffn(x, mod, 0, wts["norm_pre"][l, 0], wts["norm_post"][l, 0],
                 wts["ffn_w_in"][l][0], wts["ffn_w_out"][l][0])
        if l % 2 == 0:
            rope_tables = None if is_ctx else _rope_tables(seq_len)
            rq, rk, rv, sg, gq, gk, gv = _ab_in(x, mod, wts["norm_pre"][l, 1], wts["ab_w_in"][i],
                                               wts["gqa_q_norm"][i], wts["gqa_k_norm"][i],
                                               rope_tables, kv_dtype)
            rq, rk, rv, sg, gq, gk, gv = [unflat(a) for a in (rq, rk, rv, sg, gq, gk, gv)]
            if is_ctx:
                s_f = jnp.zeros((b, N_RET, RET_DK, RET_DV), F32)
                s_b = s_f
                k_all, v_all = gk, gv
            else:
                ck, cv, s_f, s_b = ctx_layers[l]
                lc = ck.shape[1]
                k_all = jnp.concatenate([gk, ck.reshape(b, lc, KV_W).astype(BF16)], axis=1)
                v_all = jnp.concatenate([gv, cv.reshape(b, lc, KV_W).astype(BF16)], axis=1)
            y_ret, fin_f, fin_b = _retention(rq, rk, rv, sg, wts["ret_gn"][i], wts["lg_f"][i],
                                             wts["lg_b"][i], s_f.astype(F32), s_b.astype(F32))
            y_att = _attention(gq, k_all, v_all, True, min(seq_len, 256))
            x = _mixer_out(x, mod, wts["norm_post"][l, 1], [flat(y_ret), flat(y_att)],
                           [wts["ab_w_out_ret"][i], wts["ab_w_out_att"][i]])
            aux.append((gk, gv, fin_f, fin_b) if is_ctx else None)
        else:
            q, k, v = _na_in(x, mod, wts["norm_pre"][l, 1], wts["na_w_qkv"][i], kv_dtype)
            q, k, v = unflat(q), unflat(k), unflat(v)
            if is_ctx:
                y = _attention(q, k, v, False, seq_len)
                aux.append((k, v))
            else:
                ck, cv = ctx_layers[l]
                lc = ck.shape[1]
                y = _na_attention(q, k, v, ck.reshape(b, lc, NA_W).astype(BF16),
                                  cv.reshape(b, lc, NA_W).astype(BF16), wts["na_rpb"][i])
                aux.append(None)
            x = _mixer_out(x, mod, wts["norm_post"][l, 1], [flat(y)], [wts["na_w_out"][i]])
        x = _ffn(x, mod, 2, wts["norm_pre"][l, 2], wts["norm_post"][l, 2],
                 wts["ffn_w_in"][l][1], wts["ffn_w_out"][l][1])
    return unflat(x), aux


def kernel(x_prompt, x_sample, cache_gqa_k, cache_gqa_v, state_ret_fwd, state_ret_bwd, cache_na_k,
           cache_na_v, c, c_ctx, mod_w, mod_b, norm_pre, norm_post, ffn_w_in, ffn_w_out, ab_w_in,
           ab_w_out, ret_decay_fwd, ret_decay_bwd, ret_gn, gqa_q_norm, gqa_k_norm, na_w_qkv, na_w_out,
           na_rpb):
    depth = mod_w.shape[0]
    d = x_prompt.shape[-1]
    n_dec = c.shape[0]
    batch, seq = x_prompt.shape[:2]

    n_cond = n_dec + 1
    pad = (-n_cond) % 8
    cond = jnp.concatenate([c, c_ctx[None, :], jnp.zeros((pad, d), F32)], axis=0)
    mod_all = _modulation(cond, mod_w, mod_b).reshape(depth, n_cond + pad, 3 * N_SUB, d)
    mods_sample = [mod_all[l, :n_dec] for l in range(depth)]
    mods_prompt = [mod_all[l, n_dec:n_dec + 1] for l in range(depth)]

    n_even = ab_w_in.shape[0]
    wts = {
        "norm_pre": norm_pre, "norm_post": norm_post,
        "ffn_w_in": [[ffn_w_in[l, s].astype(BF16) for s in range(2)] for l in range(depth)],
        "ffn_w_out": [[ffn_w_out[l, s].astype(BF16) for s in range(2)] for l in range(depth)],
        "ab_w_in": [_permute_q_cols(ab_w_in[i]).astype(BF16) for i in range(n_even)],
        "ab_w_out_ret": [ab_w_out[i, :RET_VW].astype(BF16) for i in range(n_even)],
        "ab_w_out_att": [_permute_att_rows(ab_w_out[i, RET_VW:]).astype(BF16) for i in range(n_even)],
        "lg_f": jax.nn.log_sigmoid(ret_decay_fwd.astype(F32)),
        "lg_b": jax.nn.log_sigmoid(ret_decay_bwd.astype(F32)),
        "ret_gn": ret_gn, "gqa_q_norm": gqa_q_norm, "gqa_k_norm": gqa_k_norm,
        "na_w_qkv": [na_w_qkv[i].astype(BF16) for i in range(na_w_qkv.shape[0])],
        "na_w_out": [na_w_out[i].astype(BF16) for i in range(na_w_out.shape[0])],
        "na_rpb": na_rpb,
    }

    y_prompt, ctx_aux = _trunk(x_prompt, mods_prompt, None, wts)

    ctx_layers = []
    for l in range(depth):
        i = l // 2
        if l % 2 == 0:
            ctx_layers.append((cache_gqa_k[:, i], cache_gqa_v[:, i], state_ret_fwd[:, i], state_ret_bwd[:, i]))
        else:
            ctx_layers.append((cache_na_k[:, i], cache_na_v[:, i]))
    y_sample, _ = _trunk(x_sample, mods_sample, ctx_layers, wts)

    def stack(idx, layers, tail):
        return jnp.stack([ctx_aux[l][idx].reshape((batch,) + tail) for l in layers], axis=1)

    even = range(0, depth, 2)
    odd = range(1, depth, 2)
    new_gqa_k = stack(0, even, (seq, N_KV, HEAD_DIM))
    new_gqa_v = stack(1, even, (seq, N_KV, HEAD_DIM))
    new_ret_fwd = stack(2, even, (N_RET, RET_DK, RET_DV))
    new_ret_bwd = stack(3, even, (N_RET, RET_DK, RET_DV))
    new_na_k = stack(0, odd, (seq, N_NA, HEAD_DIM))
    new_na_v = stack(1, odd, (seq, N_NA, HEAD_DIM))
    return (y_prompt, y_sample, new_gqa_k, new_gqa_v, new_ret_fwd, new_ret_bwd, new_na_k, new_na_v)
```

```python
import functools

import numpy as np
import jax
import jax.numpy as jnp
from jax import lax
from jax.experimental import pallas as pl
from jax.experimental.pallas import tpu as pltpu

F32 = jnp.float32
BF16 = jnp.bfloat16

GRID_W = 64
N_SUB = 3
HEAD_DIM = 64
N_RET = 8
RET_DK = 64
RET_DV = 64
RET_GN_EPS = 1e-5
N_Q = 8
N_KV = 2
Q_GROUP = N_Q // N_KV
ROPE_THETA = 10000.0
N_NA = 16
NA_KR_MAX = 8
NA_KC = 16
EPS = 1e-6
RET_W = N_RET * RET_DK
RET_VW = N_RET * RET_DV
GQA_W = N_Q * HEAD_DIM
KV_W = N_KV * HEAD_DIM
NA_W = N_NA * HEAD_DIM

LANES = 128
VMEM_LIMIT = 56 * 1024 * 1024
RET_CHUNK = 256
RET_UNROLL = 4
NA_ROWS_PER_STEP = 4
NA_UNROLL = 4
GQA_TQ = 256
GQA_KEY_CHUNK = 1024
MASK_VALUE = -1e30
LOG2E = 1.4426950408889634
SCORE_SCALE = HEAD_DIM ** -0.5 * LOG2E


def _cparams(n_grid):
    return pltpu.CompilerParams(dimension_semantics=("parallel",) * n_grid,
                                vmem_limit_bytes=VMEM_LIMIT)


def _const_spec(shape):
    nd = len(shape)
    return pl.BlockSpec(shape, lambda *_: (0,) * nd, pipeline_mode=pl.Buffered(1))


def _silu(x):
    return x * jax.nn.sigmoid(x)


def _rms(x, g):
    ms = jnp.mean(x * x, axis=-1, keepdims=True)
    return x * lax.rsqrt(ms + EPS) * g


def _modulated(x, m, sub, g_pre):
    shift = m[3 * sub:3 * sub + 1]
    scale = m[3 * sub + 1:3 * sub + 2]
    return _rms(x, g_pre) * (1.0 + scale) + shift


def _dot(a, b):
    return jnp.dot(a, b, preferred_element_type=F32)


def _dot_nt(a, b):
    return lax.dot_general(a, b, (((1,), (1,)), ((), ())), preferred_element_type=F32)


def _dot_tn(a, b):
    return lax.dot_general(a, b, (((0,), (0,)), ((), ())), preferred_element_type=F32)


def _low_half(shape):
    return lax.broadcasted_iota(jnp.int32, shape, len(shape) - 1) < HEAD_DIM


def _mod_body(c_ref, w_ref, b_ref, o_ref):
    a = _silu(c_ref[...]).astype(BF16)
    o_ref[0] = _dot(a, w_ref[0].astype(BF16)) + b_ref[0]


def _modulation(cond, mod_w, mod_b):
    depth, d, n = mod_w.shape
    rows = cond.shape[0]
    tn = 1152 if n % 1152 == 0 else n
    return pl.pallas_call(
        _mod_body,
        grid=(depth, n // tn),
        in_specs=[pl.BlockSpec((rows, d), lambda l, j: (0, 0)),
                  pl.BlockSpec((1, d, tn), lambda l, j: (l, 0, j)),
                  pl.BlockSpec((1, 1, tn), lambda l, j: (l, 0, j))],
        out_specs=pl.BlockSpec((1, rows, tn), lambda l, j: (l, 0, j)),
        out_shape=jax.ShapeDtypeStruct((depth, rows, n), F32),
        compiler_params=_cparams(2),
        name="modulation",
    )(cond, mod_w, mod_b.reshape(depth, 1, n))


def _tok_spec(tm, width):
    return pl.BlockSpec((1, tm, width), lambda b, i: (b, i, 0))


def _mod_spec(mod):
    nm, rows, d = mod.shape
    if nm == 1:
        return pl.BlockSpec((1, rows, d), lambda b, i: (0, 0, 0))
    return pl.BlockSpec((1, rows, d), lambda b, i: (b, 0, 0))


def _token_tile(seq_len):
    return 512 if seq_len % 512 == 0 else seq_len


def _ffn_body(x_ref, mod_ref, gpre_ref, gpost_ref, win_ref, wout_ref, o_ref, *, sub, dff):
    x = x_ref[0]
    m = mod_ref[0]
    h = _modulated(x, m, sub, gpre_ref[...]).astype(BF16)
    gu = _dot(h, win_ref[...])
    g = gu[:, :dff]
    u = gu[:, dff:]
    a = (_silu(g) * u).astype(BF16)
    y = _dot(a, wout_ref[...])
    gate = m[3 * sub + 2:3 * sub + 3]
    o_ref[0] = x + (0.5 * gate) * _rms(y, gpost_ref[...])


def _ffn(x, mod, sub, g_pre, g_post, w_in, w_out):
    b, seq_len, d = x.shape
    dff = w_out.shape[0]
    tm = _token_tile(seq_len)
    return pl.pallas_call(
        functools.partial(_ffn_body, sub=sub, dff=dff),
        grid=(b, seq_len // tm),
        in_specs=[_tok_spec(tm, d), _mod_spec(mod), _const_spec((1, d)), _const_spec((1, d)),
                  _const_spec(w_in.shape), _const_spec(w_out.shape)],
        out_specs=_tok_spec(tm, d),
        out_shape=jax.ShapeDtypeStruct(x.shape, F32),
        compiler_params=_cparams(2),
        name="ffn_half",
    )(x, mod, g_pre.reshape(1, d), g_post.reshape(1, d), w_in, w_out)


def _head_rms(a, g, bd):
    a2 = a * a
    hi = a2.astype(BF16)
    lo = (a2 - hi.astype(F32)).astype(BF16)
    ss = _dot(hi, bd) + _dot(lo, bd)
    return a * lax.rsqrt(ss * (1.0 / HEAD_DIM) + EPS) * g


def _rope(a, cos, sin):
    quarter = HEAD_DIM // 4
    outs = []
    for j in range(a.shape[1] // LANES):
        aj = a[:, j * LANES:(j + 1) * LANES]
        up = pltpu.roll(aj, LANES - quarter, axis=1)
        dn = pltpu.roll(aj, quarter, axis=1)
        lane = lax.broadcasted_iota(jnp.int32, aj.shape, 1)
        partner = jnp.where((lane % (2 * quarter)) < quarter, up, dn)
        outs.append(aj * cos + partner * sin)
    return outs[0] if len(outs) == 1 else jnp.concatenate(outs, axis=1)


def _ab_in_body(*refs, rope):
    if rope:
        (x_ref, mod_ref, gpre_ref, w_ref, qn_ref, kn_ref, bd_ref, cos_ref, sin_ref,
         rq_o, rk_o, rv_o, sg_o, gq_o, gk_o, gv_o) = refs
    else:
        (x_ref, mod_ref, gpre_ref, w_ref, qn_ref, kn_ref, bd_ref,
         rq_o, rk_o, rv_o, sg_o, gq_o, gk_o, gv_o) = refs
    h = _modulated(x_ref[0], mod_ref[0], 1, gpre_ref[...]).astype(BF16)
    p = _dot(h, w_ref[...])
    o = 0
    rq_o[0] = p[:, o:o + RET_W].astype(rq_o.dtype)
    o += RET_W
    rk_o[0] = (p[:, o:o + RET_W] * (RET_DK ** -0.5)).astype(rk_o.dtype)
    o += RET_W
    rv_o[0] = p[:, o:o + RET_VW].astype(rv_o.dtype)
    o += RET_VW
    sg_o[0] = _silu(p[:, o:o + RET_VW]).astype(sg_o.dtype)
    o += RET_VW
    gq = p[:, o:o + GQA_W]
    o += GQA_W
    gk = p[:, o:o + KV_W]
    o += KV_W
    gv = p[:, o:o + KV_W]
    bd = bd_ref[...]
    qh = _head_rms(gq, qn_ref[...], bd)
    kh = _head_rms(gk, kn_ref[...], bd[:KV_W, :KV_W])
    if rope:
        qh = _rope(qh, cos_ref[...], sin_ref[...])
        kh = _rope(kh, cos_ref[...], sin_ref[...])
    gq_o[0] = (qh * SCORE_SCALE).astype(gq_o.dtype)
    gk_o[0] = kh.astype(gk_o.dtype)
    gv_o[0] = gv.astype(gv_o.dtype)


def _ab_in(x, mod, g_pre, w, qn, kn, rope_tables, kv_dtype):
    b, seq_len, d = x.shape
    tm = _token_tile(seq_len)
    rope = rope_tables is not None
    bd = jnp.asarray(np.kron(np.eye(N_Q), np.ones((HEAD_DIM, HEAD_DIM))), BF16)
    ins = [x, mod, g_pre.reshape(1, d), w, jnp.tile(qn, N_Q).reshape(1, GQA_W),
           jnp.tile(kn, N_KV).reshape(1, KV_W), bd]
    in_specs = [_tok_spec(tm, d), _mod_spec(mod), _const_spec((1, d)), _const_spec(w.shape),
                _const_spec((1, GQA_W)), _const_spec((1, KV_W)), _const_spec(bd.shape)]
    if rope:
        ins += list(rope_tables)
        in_specs += [pl.BlockSpec((tm, LANES), lambda b, i: (i, 0))] * 2
    widths = (RET_W, RET_W, RET_VW, RET_VW, GQA_W, KV_W, KV_W)
    dtypes = (BF16, BF16, BF16, BF16, BF16, kv_dtype, kv_dtype)
    return pl.pallas_call(
        functools.partial(_ab_in_body, rope=rope),
        grid=(b, seq_len // tm),
        in_specs=in_specs,
        out_specs=[_tok_spec(tm, wd) for wd in widths],
        out_shape=[jax.ShapeDtypeStruct((b, seq_len, wd), dt) for wd, dt in zip(widths, dtypes)],
        compiler_params=_cparams(2),
        name="ab_in_proj",
    )(*ins)


def _rope_tables(seq_len):
    t = jnp.arange(seq_len)
    quarter = HEAD_DIM // 4
    inv = ROPE_THETA ** (-jnp.arange(quarter, dtype=F32) / quarter)
    ang_r = (t // GRID_W).astype(F32)[:, None] * inv[None, :]
    ang_c = (t % GRID_W).astype(F32)[:, None] * inv[None, :]
    cos = jnp.concatenate([jnp.cos(ang_r)] * 2 + [jnp.cos(ang_c)] * 2, axis=1)
    sin = jnp.concatenate([-jnp.sin(ang_r), jnp.sin(ang_r), -jnp.sin(ang_c), jnp.sin(ang_c)], axis=1)
    return jnp.tile(cos, (1, LANES // HEAD_DIM)), jnp.tile(sin, (1, LANES // HEAD_DIM))


def _ret_body(lgf_ref, lgb_ref, rq_ref, rk_ref, rv_ref, sg_ref, gn_ref, sf_ref, sb_ref,
              y_ref, ff_ref, fb_ref, sbs_ref, *, chunk, n_chunks):
    pair = pl.program_id(1)
    c = chunk
    lo_row = _low_half((1, LANES))
    lgf = jnp.where(lo_row, lgf_ref[2 * pair], lgf_ref[2 * pair + 1])
    lgb = jnp.where(lo_row, lgb_ref[2 * pair], lgb_ref[2 * pair + 1])
    idx = lax.broadcasted_iota(jnp.int32, (c, 1), 0).astype(F32)
    qdec_f = jnp.exp(lgf * (idx + 1.0))
    kdec_f = jnp.exp(lgf * (c - 1.0 - idx))
    cdec_f = jnp.exp(lgf * float(c))
    qdec_b = jnp.exp(lgb * (c - idx))
    kdec_b = jnp.exp(lgb * idx)
    cdec_b = jnp.exp(lgb * float(c))
    diff = (lax.broadcasted_iota(jnp.int32, (c, c), 0)
            - lax.broadcasted_iota(jnp.int32, (c, c), 1)).astype(F32)
    dmat = [jnp.exp(jnp.where(diff >= 0, lgf_ref[2 * pair + h], -lgb_ref[2 * pair + h]) * diff)
            for h in range(2)]
    rr = lax.broadcasted_iota(jnp.int32, (LANES, LANES), 0) < HEAD_DIM
    cc = lax.broadcasted_iota(jnp.int32, (LANES, LANES), 1) < HEAD_DIM
    same_head = rr == cc
    lo = _low_half((c, LANES))

    def load(ref, n):
        return ref[0, pl.ds(pl.multiple_of(n * c, c), c), :]

    def bwd_step(t, state):
        n = n_chunks - 1 - t
        sbs_ref[n] = state
        k = load(rk_ref, n).astype(F32)
        v = load(rv_ref, n).astype(BF16)
        kv = _dot_tn((k * kdec_b).astype(BF16), v)
        return state * cdec_b + jnp.where(same_head, kv, 0.0)

    unroll = int(np.gcd(n_chunks, RET_UNROLL))
    fb_ref[0, 0] = lax.fori_loop(0, n_chunks, bwd_step, sb_ref[0, 0], unroll=unroll)

    gn = gn_ref[...]

    def fwd_step(n, state):
        q = load(rq_ref, n).astype(BF16)
        k = load(rk_ref, n).astype(BF16)
        v = load(rv_ref, n).astype(BF16)
        qf = q.astype(F32)
        o = (_dot((qf * qdec_f).astype(BF16), state.astype(BF16))
             + _dot((qf * qdec_b).astype(BF16), sbs_ref[n].astype(BF16)))
        zero = jnp.zeros_like(q)
        intra = []
        for h in range(2):
            qh = jnp.where(lo, q, zero) if h == 0 else jnp.where(lo, zero, q)
            s = _dot_nt(qh, k) * dmat[h]
            intra.append(_dot(s.astype(BF16), v))
        o = o + jnp.where(lo, intra[0], intra[1])
        def head_mean(a):
            m0 = jnp.sum(jnp.where(lo, a, 0.0), axis=-1, keepdims=True)
            m1 = jnp.sum(jnp.where(lo, 0.0, a), axis=-1, keepdims=True)
            return jnp.where(lo, m0, m1) * (1.0 / RET_DV)
        dev = o - head_mean(o)
        var = head_mean(dev * dev)
        y = dev * lax.rsqrt(var + RET_GN_EPS) * gn * load(sg_ref, n).astype(F32)
        y_ref[0, pl.ds(pl.multiple_of(n * c, c), c), :] = y.astype(y_ref.dtype)
        kv = _dot_tn((k.astype(F32) * kdec_f).astype(BF16), v)
        return state * cdec_f + jnp.where(same_head, kv, 0.0)

    ff_ref[0, 0] = lax.fori_loop(0, n_chunks, fwd_step, sf_ref[0, 0], unroll=unroll)


def _pair_states(s):
    b, nh, dk, dv = s.shape
    s = s.reshape(b, nh // 2, 2, dk, dv)
    z = jnp.zeros_like(s[:, :, 0])
    top = jnp.concatenate([s[:, :, 0], z], axis=-1)
    bot = jnp.concatenate([z, s[:, :, 1]], axis=-1)
    return jnp.concatenate([top, bot], axis=-2)


def _unpair_states(s):
    b, npair = s.shape[:2]
    a = s[:, :, :RET_DK, :RET_DV]
    d = s[:, :, RET_DK:, RET_DV:]
    return jnp.stack([a, d], axis=2).reshape(b, 2 * npair, RET_DK, RET_DV)


def _retention(rq, rk, rv, sg, gn, lg_f, lg_b, s_f, s_b):
    b, seq_len, _ = rq.shape
    c = RET_CHUNK if seq_len % RET_CHUNK == 0 else seq_len
    nc = seq_len // c
    npair = N_RET // 2
    seq_spec = pl.BlockSpec((1, seq_len, LANES), lambda i, p: (i, 0, p))
    st_spec = pl.BlockSpec((1, 1, LANES, LANES), lambda i, p: (i, p, 0, 0))
    smem = pl.BlockSpec(memory_space=pltpu.SMEM)
    st_shape = jax.ShapeDtypeStruct((b, npair, LANES, LANES), F32)
    y, fin_f, fin_b = pl.pallas_call(
        functools.partial(_ret_body, chunk=c, n_chunks=nc),
        grid=(b, npair),
        in_specs=[smem, smem, seq_spec, seq_spec, seq_spec, seq_spec,
                  pl.BlockSpec((1, LANES), lambda i, p: (0, p)), st_spec, st_spec],
        out_specs=[seq_spec, st_spec, st_spec],
        out_shape=[jax.ShapeDtypeStruct((b, seq_len, RET_VW), BF16), st_shape, st_shape],
        scratch_shapes=[pltpu.VMEM((nc, LANES, LANES), F32)],
        compiler_params=_cparams(2),
        name="retention",
    )(lg_f, lg_b, rq, rk, rv, sg, gn.reshape(1, RET_VW), _pair_states(s_f), _pair_states(s_b))
    return y, _unpair_states(fin_f), _unpair_states(fin_b)


def _softmax_pv(s_list, v_list):
    m = s_list[0].max(axis=-1, keepdims=True)
    for s in s_list[1:]:
        m = jnp.maximum(m, s.max(axis=-1, keepdims=True))
    acc = None
    for s, v in zip(s_list, v_list):
        o = _dot(jnp.exp2(s - m).astype(BF16), v)
        acc = o if acc is None else acc + o
    return acc * (1.0 / pltpu.roll(acc, HEAD_DIM, axis=1))


def _ones_other_head(v, h):
    lo = _low_half(v.shape)
    one = jnp.ones_like(v)
    return jnp.where(lo, v, one) if h == 0 else jnp.where(lo, one, v)


def _only_head(q, h):
    lo = _low_half(q.shape)
    zero = jnp.zeros_like(q)
    return jnp.where(lo, q, zero) if h == 0 else jnp.where(lo, zero, q)


def _attn_body(q_ref, k_ref, v_ref, o_ref, *, n_groups, kv_shared):
    tq = q_ref.shape[1]
    lo = _low_half((tq, LANES))
    for j in range(n_groups):
        sl = slice(j * LANES, (j + 1) * LANES)
        q = q_ref[0, :, sl].astype(BF16)
        k = (k_ref[0] if kv_shared else k_ref[0, :, sl]).astype(BF16)
        v = (v_ref[0] if kv_shared else v_ref[0, :, sl]).astype(BF16)
        outs = [_softmax_pv([_dot_nt(_only_head(q, h), k)], [_ones_other_head(v, h)]) for h in range(2)]
        o_ref[0, :, sl] = jnp.where(lo, outs[0], outs[1]).astype(o_ref.dtype)


def _col_max(x, slab=64):
    n, t = x.shape
    if n > slab and n % slab == 0:
        x = x.reshape(n // slab, slab, t).max(axis=0)
    return x.max(axis=0, keepdims=True)


def _gqa_t_body(q_ref, kn_ref, kc_ref, vtn_ref, vtc_ref, o_ref, *, n_groups, chunk):
    tq = q_ref.shape[1]
    top = lax.broadcasted_iota(jnp.int32, (LANES, tq), 0) < HEAD_DIM
    ln, lc = kn_ref.shape[1], kc_ref.shape[1]
    chunks = [(kn_ref, vtn_ref, c * chunk, chunk) for c in range(ln // chunk)] + [(kc_ref, vtc_ref, 0, lc)]
    items = [(j, h, c) for j in range(n_groups) for h in range(2) for c in range(len(chunks))]

    def scores(item):
        j, h, c = item
        k_ref, _, off, size = chunks[c]
        qh = _only_head(q_ref[0, :, j * LANES:(j + 1) * LANES], h)
        return _dot_nt(k_ref[0, off:off + size, :], qh)

    outs = {}
    state = {"m": None, "acc": None}

    def probs(item, st):
        c = item[2]
        mc = _col_max(st)
        m_new = mc if c == 0 else jnp.maximum(state["m"], mc)
        alpha = None if c == 0 else jnp.exp2(state["m"] - m_new)
        state["m"] = m_new
        return jnp.exp2(st - m_new).astype(BF16), alpha

    def accumulate(item, p, alpha):
        j, h, c = item
        _, vt_ref, off, size = chunks[c]
        o = _dot(vt_ref[0, h, :, off:off + size], p)
        acc = o if c == 0 else state["acc"] * alpha + o
        state["acc"] = acc
        if c == len(chunks) - 1:
            den = acc[HEAD_DIM:HEAD_DIM + 1] if h == 0 else acc[0:1]
            outs[h] = acc * (1.0 / den)
            if h == 1:
                sl = slice(j * LANES, (j + 1) * LANES)
                o_ref[0, :, sl] = jnp.where(top, outs[0], outs[1]).T.astype(o_ref.dtype)

    n = len(items)
    ahead = 2
    sts = [scores(items[i]) for i in range(min(ahead, n))]
    pending = None
    for i in range(n):
        st = sts.pop(0)
        if i + ahead < n:
            sts.append(scores(items[i + ahead]))
        p, alpha = probs(items[i], st)
        if pending is not None:
            accumulate(*pending)
        pending = (items[i], p, alpha)
    accumulate(*pending)


def _vt_with_ones(v):
    vt = jnp.swapaxes(v, 1, 2)
    top = (np.arange(KV_W) < HEAD_DIM)[None, :, None]
    one = jnp.ones_like(vt)
    return jnp.stack([jnp.where(top, vt, one), jnp.where(top, one, vt)], axis=1)


def _gqa_attention_t(q, k_new, v_new, k_ctx, v_ctx, tq):
    b, lq, wq = q.shape
    ln, lc = k_new.shape[1], k_ctx.shape[1]
    chunk = GQA_KEY_CHUNK if ln % GQA_KEY_CHUNK == 0 else ln
    whole3 = lambda i, t: (i, 0, 0)
    whole4 = lambda i, t: (i, 0, 0, 0)
    return pl.pallas_call(
        functools.partial(_gqa_t_body, n_groups=wq // LANES, chunk=chunk),
        grid=(b, lq // tq),
        in_specs=[pl.BlockSpec((1, tq, wq), lambda i, t: (i, t, 0)),
                  pl.BlockSpec((1, ln, KV_W), whole3), pl.BlockSpec((1, lc, KV_W), whole3),
                  pl.BlockSpec((1, 2, KV_W, ln), whole4), pl.BlockSpec((1, 2, KV_W, lc), whole4)],
        out_specs=pl.BlockSpec((1, tq, wq), lambda i, t: (i, t, 0)),
        out_shape=jax.ShapeDtypeStruct((b, lq, wq), BF16),
        compiler_params=_cparams(2),
        name="gqa_attention_t",
    )(q, k_new, k_ctx, _vt_with_ones(v_new), _vt_with_ones(v_ctx))


def _attention(q, k, v, kv_shared, tq):
    b, lq, wq = q.shape
    lk, wk = k.shape[1:]
    return pl.pallas_call(
        functools.partial(_attn_body, n_groups=wq // LANES, kv_shared=kv_shared),
        grid=(b, lq // tq),
        in_specs=[pl.BlockSpec((1, tq, wq), lambda i, t: (i, t, 0)),
                  pl.BlockSpec((1, lk, wk), lambda i, t: (i, 0, 0)),
                  pl.BlockSpec((1, lk, wk), lambda i, t: (i, 0, 0))],
        out_specs=pl.BlockSpec((1, tq, wq), lambda i, t: (i, t, 0)),
        out_shape=jax.ShapeDtypeStruct((b, lq, wq), BF16),
        compiler_params=_cparams(2),
        name="dense_attention",
    )(q, k, v)


def _na_plan(rows):
    g = NA_ROWS_PER_STEP
    kr = min(NA_KR_MAX, rows)
    u = min(g + kr - 1, rows)
    n_steps = rows // g
    bases, vids, variants, keys = [], [], [], {}
    for s in range(n_steps):
        base = int(np.clip(s * g - kr // 2, 0, rows - u))
        r = s * g + np.arange(g)[:, None]
        krow = base + np.arange(u)[None, :]
        r0 = np.clip(r - kr // 2, 0, rows - kr)
        valid = (krow >= r0) & (krow < r0 + kr)
        dr = np.where(valid, krow - r + (NA_KR_MAX - 1), 0)
        key = (dr.tobytes(), valid.tobytes())
        if key not in keys:
            keys[key] = len(variants)
            variants.append((dr, valid))
        bases.append(base)
        vids.append(keys[key])
    return u, bases, vids, variants


def _na_bias_tables(rpb, variants, u):
    g = NA_ROWS_PER_STEP
    nh, n_dr, n_dc = rpb.shape
    cols = np.arange(GRID_W)
    c0 = np.clip(cols - NA_KC // 2, 0, GRID_W - NA_KC)
    kc = np.arange(GRID_W)[None, :]
    cvalid = (kc >= c0[:, None]) & (kc < c0[:, None] + NA_KC)
    dc = kc - cols[:, None] + (NA_KC - 1)
    sel_c = cvalid[None] & (dc[None] == np.arange(n_dc)[:, None, None])
    tables = []
    for dr, rvalid in variants:
        sel_r = rvalid[..., None] & (dr[..., None] == np.arange(n_dr))
        t = jnp.einsum("gud,hde,eck->hgcuk", jnp.asarray(sel_r, F32), rpb.astype(F32),
                       jnp.asarray(sel_c, F32), precision=lax.Precision.HIGHEST)
        ok = rvalid[:, None, :, None] & cvalid[None, :, None, :]
        tables.append(jnp.where(ok[None], t * LOG2E, MASK_VALUE).reshape(nh, g * GRID_W, u * GRID_W))
    return jnp.stack(tables)


def _na_body(base_ref, vid_ref, q_ref, k_ref, v_ref, kc_ref, vc_ref, bias_ref, o_ref, *, n_steps, tq, tk):
    lo = _low_half((tq, LANES))
    kc = kc_ref[0].astype(BF16)
    vc = [_ones_other_head(vc_ref[0].astype(BF16), h) for h in range(2)]

    unroll = int(np.gcd(n_steps, NA_UNROLL))

    def trip(t, carry):
        items = [(u, h) for u in range(unroll) for h in range(2)]
        steps = [t * unroll + u for u in range(unroll)]
        outs = {}

        def q_rows(u):
            return pl.ds(pl.multiple_of(steps[u] * tq, tq), tq)

        def k_rows(u):
            return pl.ds(pl.multiple_of(base_ref[steps[u]] * GRID_W, GRID_W), tk)

        def scores(item):
            u, h = item
            qh = _only_head(q_ref[0, q_rows(u), :], h)
            s_win = _dot_nt(qh, k_ref[0, k_rows(u), :]) + bias_ref[vid_ref[steps[u]], h]
            return s_win, _dot_nt(qh, kc)

        def probs(sc):
            s_win, s_ctx = sc
            m = jnp.maximum(s_win.max(axis=-1, keepdims=True), s_ctx.max(axis=-1, keepdims=True))
            return jnp.exp2(s_win - m).astype(BF16), jnp.exp2(s_ctx - m).astype(BF16)

        def accumulate(item, ps):
            u, h = item
            acc = _dot(ps[0], _ones_other_head(v_ref[0, k_rows(u), :], h)) + _dot(ps[1], vc[h])
            outs[h] = acc * (1.0 / pltpu.roll(acc, HEAD_DIM, axis=1))
            if h == 1:
                o_ref[0, q_rows(u), :] = jnp.where(lo, outs[0], outs[1]).astype(o_ref.dtype)

        n = len(items)
        ahead = 2
        scs = [scores(items[i]) for i in range(min(ahead, n))]
        pending = None
        for i in range(n):
            sc = scs.pop(0)
            if i + ahead < n:
                scs.append(scores(items[i + ahead]))
            ps = probs(sc)
            if pending is not None:
                accumulate(*pending)
            pending = (items[i], ps)
        accumulate(*pending)
        return carry

    lax.fori_loop(0, n_steps // unroll, trip, 0)


def _na_attention(q, k, v, k_ctx, v_ctx, rpb):
    b, seq_len, w = q.shape
    rows = seq_len // GRID_W
    lc = k_ctx.shape[1]
    u, bases, vids, variants = _na_plan(rows)
    bias = _na_bias_tables(rpb, variants, u)
    nv = bias.shape[0]
    tq = NA_ROWS_PER_STEP * GRID_W
    tk = u * GRID_W
    n_steps = rows // NA_ROWS_PER_STEP
    smem = pl.BlockSpec(memory_space=pltpu.SMEM)
    seq_spec = pl.BlockSpec((1, seq_len, LANES), lambda p, i: (i, 0, p))
    ctx_spec = pl.BlockSpec((1, lc, LANES), lambda p, i: (i, 0, p))
    return pl.pallas_call(
        functools.partial(_na_body, n_steps=n_steps, tq=tq, tk=tk),
        grid=(w // LANES, b),
        in_specs=[smem, smem, seq_spec, seq_spec, seq_spec, ctx_spec, ctx_spec,
                  pl.BlockSpec((nv, 2, tq, tk), lambda p, i: (0, p, 0, 0))],
        out_specs=seq_spec,
        out_shape=jax.ShapeDtypeStruct((b, seq_len, w), BF16),
        compiler_params=_cparams(2),
        name="neighbourhood_attention",
    )(jnp.asarray(bases, jnp.int32), jnp.asarray(vids, jnp.int32), q, k, v, k_ctx, v_ctx, bias)


def _na_in_body(x_ref, mod_ref, gpre_ref, w_ref, q_o, k_o, v_o):
    h = _modulated(x_ref[0], mod_ref[0], 1, gpre_ref[...]).astype(BF16)
    p = _dot(h, w_ref[...])
    q_o[0] = (p[:, :NA_W] * SCORE_SCALE).astype(q_o.dtype)
    k_o[0] = p[:, NA_W:2 * NA_W].astype(k_o.dtype)
    v_o[0] = p[:, 2 * NA_W:].astype(v_o.dtype)


def _na_in(x, mod, g_pre, w, kv_dtype):
    b, seq_len, d = x.shape
    tm = _token_tile(seq_len)
    return pl.pallas_call(
        _na_in_body,
        grid=(b, seq_len // tm),
        in_specs=[_tok_spec(tm, d), _mod_spec(mod), _const_spec((1, d)), _const_spec(w.shape)],
        out_specs=[_tok_spec(tm, NA_W)] * 3,
        out_shape=[jax.ShapeDtypeStruct((b, seq_len, NA_W), dt) for dt in (BF16, kv_dtype, kv_dtype)],
        compiler_params=_cparams(2),
        name="na_in_proj",
    )(x, mod, g_pre.reshape(1, d), w)


def _out_body(*refs, n_seg):
    x_ref, mod_ref, gpost_ref = refs[:3]
    seg_refs = refs[3:3 + n_seg]
    w_refs = refs[3 + n_seg:3 + 2 * n_seg]
    o_ref = refs[3 + 2 * n_seg]
    y = None
    for s_ref, w_ref in zip(seg_refs, w_refs):
        part = _dot(s_ref[0].astype(BF16), w_ref[...])
        y = part if y is None else y + part
    gate = mod_ref[0][5:6]
    o_ref[0] = x_ref[0] + gate * _rms(y, gpost_ref[...])


def _mixer_out(x, mod, g_post, segs, ws):
    b, seq_len, d = x.shape
    tm = _token_tile(seq_len)
    n_seg = len(segs)
    return pl.pallas_call(
        functools.partial(_out_body, n_seg=n_seg),
        grid=(b, seq_len // tm),
        in_specs=([_tok_spec(tm, d), _mod_spec(mod), _const_spec((1, d))]
                  + [_tok_spec(tm, s.shape[-1]) for s in segs]
                  + [_const_spec(w.shape) for w in ws]),
        out_specs=_tok_spec(tm, d),
        out_shape=jax.ShapeDtypeStruct(x.shape, F32),
        compiler_params=_cparams(2),
        name="mixer_out_proj",
    )(x, mod, g_post.reshape(1, d), *segs, *ws)


_Q_HEAD_ORDER = [h for j in range(Q_GROUP) for h in range(j, N_Q, Q_GROUP)]


def _permute_q_cols(w_in):
    cols = np.arange(w_in.shape[1])
    q0 = 2 * RET_W + 2 * RET_VW
    perm = np.concatenate([np.arange(h * HEAD_DIM, (h + 1) * HEAD_DIM) for h in _Q_HEAD_ORDER])
    cols[q0:q0 + GQA_W] = q0 + perm
    return w_in[:, cols]


def _permute_att_rows(w_att):
    perm = np.concatenate([np.arange(h * HEAD_DIM, (h + 1) * HEAD_DIM) for h in _Q_HEAD_ORDER])
    return w_att[perm]


def _trunk(x, mods, ctx_layers, wts):
    b, seq_len, d = x.shape
    is_ctx = ctx_layers is None
    depth = len(mods)
    flat = (lambda a: a.reshape(1, b * seq_len, a.shape[-1])) if is_ctx else (lambda a: a)
    unflat = (lambda a: a.reshape(b, seq_len, a.shape[-1])) if is_ctx else (lambda a: a)
    kv_dtype = F32 if is_ctx else BF16
    aux = []
    x = flat(x)
    for l in range(depth):
        mod = mods[l]
        i = l // 2
        x = _ffn(x, mod, 0, wts["norm_pre"][l, 0], wts["norm_post"][l, 0],
                 wts["ffn_w_in"][l][0], wts["ffn_w_out"][l][0])
        if l % 2 == 0:
            rope_tables = None if is_ctx else _rope_tables(seq_len)
            rq, rk, rv, sg, gq, gk, gv = _ab_in(x, mod, wts["norm_pre"][l, 1], wts["ab_w_in"][i],
                                               wts["gqa_q_norm"][i], wts["gqa_k_norm"][i],
                                               rope_tables, kv_dtype)
            rq, rk, rv, sg, gq, gk, gv = [unflat(a) for a in (rq, rk, rv, sg, gq, gk, gv)]
            if is_ctx:
                s_f = jnp.zeros((b, N_RET, RET_DK, RET_DV), F32)
                s_b = s_f
            else:
                ck, cv, s_f, s_b = ctx_layers[l]
                lc = ck.shape[1]
            y_ret, fin_f, fin_b = _retention(rq, rk, rv, sg, wts["ret_gn"][i], wts["lg_f"][i],
                                             wts["lg_b"][i], s_f.astype(F32), s_b.astype(F32))
            if is_ctx:
                y_att = _attention(gq, gk, gv, True, seq_len)
            else:
                y_att = _gqa_attention_t(gq, gk, gv, ck.reshape(b, lc, KV_W).astype(BF16),
                                         cv.reshape(b, lc, KV_W).astype(BF16), GQA_TQ)
            x = _mixer_out(x, mod, wts["norm_post"][l, 1], [flat(y_ret), flat(y_att)],
                           [wts["ab_w_out_ret"][i], wts["ab_w_out_att"][i]])
            aux.append((gk, gv, fin_f, fin_b) if is_ctx else None)
        else:
            q, k, v = _na_in(x, mod, wts["norm_pre"][l, 1], wts["na_w_qkv"][i], kv_dtype)
            q, k, v = unflat(q), unflat(k), unflat(v)
            if is_ctx:
                y = _attention(q, k, v, False, seq_len)
                aux.append((k, v))
            else:
                ck, cv = ctx_layers[l]
                lc = ck.shape[1]
                y = _na_attention(q, k, v, ck.reshape(b, lc, NA_W).astype(BF16),
                                  cv.reshape(b, lc, NA_W).astype(BF16), wts["na_rpb"][i])
                aux.append(None)
            x = _mixer_out(x, mod, wts["norm_post"][l, 1], [flat(y)], [wts["na_w_out"][i]])
        x = _ffn(x, mod, 2, wts["norm_pre"][l, 2], wts["norm_post"][l, 2],
                 wts["ffn_w_in"][l][1], wts["ffn_w_out"][l][1])
    return unflat(x), aux


def kernel(x_prompt, x_sample, cache_gqa_k, cache_gqa_v, state_ret_fwd, state_ret_bwd, cache_na_k,
           cache_na_v, c, c_ctx, mod_w, mod_b, norm_pre, norm_post, ffn_w_in, ffn_w_out, ab_w_in,
           ab_w_out, ret_decay_fwd, ret_decay_bwd, ret_gn, gqa_q_norm, gqa_k_norm, na_w_qkv, na_w_out,
           na_rpb):
    depth = mod_w.shape[0]
    d = x_prompt.shape[-1]
    n_dec = c.shape[0]
    batch, seq = x_prompt.shape[:2]

    n_cond = n_dec + 1
    pad = (-n_cond) % 8
    cond = jnp.concatenate([c, c_ctx[None, :], jnp.zeros((pad, d), F32)], axis=0)
    mod_all = _modulation(cond, mod_w, mod_b).reshape(depth, n_cond + pad, 3 * N_SUB, d)
    mods_sample = [mod_all[l, :n_dec] for l in range(depth)]
    mods_prompt = [mod_all[l, n_dec:n_dec + 1] for l in range(depth)]

    n_even = ab_w_in.shape[0]
    wts = {
        "norm_pre": norm_pre, "norm_post": norm_post,
        "ffn_w_in": [[ffn_w_in[l, s].astype(BF16) for s in range(2)] for l in range(depth)],
        "ffn_w_out": [[ffn_w_out[l, s].astype(BF16) for s in range(2)] for l in range(depth)],
        "ab_w_in": [_permute_q_cols(ab_w_in[i]).astype(BF16) for i in range(n_even)],
        "ab_w_out_ret": [ab_w_out[i, :RET_VW].astype(BF16) for i in range(n_even)],
        "ab_w_out_att": [_permute_att_rows(ab_w_out[i, RET_VW:]).astype(BF16) for i in range(n_even)],
        "lg_f": jax.nn.log_sigmoid(ret_decay_fwd.astype(F32)),
        "lg_b": jax.nn.log_sigmoid(ret_decay_bwd.astype(F32)),
        "ret_gn": ret_gn, "gqa_q_norm": gqa_q_norm, "gqa_k_norm": gqa_k_norm,
        "na_w_qkv": [na_w_qkv[i].astype(BF16) for i in range(na_w_qkv.shape[0])],
        "na_w_out": [na_w_out[i].astype(BF16) for i in range(na_w_out.shape[0])],
        "na_rpb": na_rpb,
    }

    y_prompt, ctx_aux = _trunk(x_prompt, mods_prompt, None, wts)

    ctx_layers = []
    for l in range(depth):
        i = l // 2
        if l % 2 == 0:
            ctx_layers.append((cache_gqa_k[:, i], cache_gqa_v[:, i], state_ret_fwd[:, i], state_ret_bwd[:, i]))
        else:
            ctx_layers.append((cache_na_k[:, i], cache_na_v[:, i]))
    y_sample, _ = _trunk(x_sample, mods_sample, ctx_layers, wts)

    def stack(idx, layers, tail):
        return jnp.stack([ctx_aux[l][idx].reshape((batch,) + tail) for l in layers], axis=1)

    even = range(0, depth, 2)
    odd = range(1, depth, 2)
    new_gqa_k = stack(0, even, (seq, N_KV, HEAD_DIM))
    new_gqa_v = stack(1, even, (seq, N_KV, HEAD_DIM))
    new_ret_fwd = stack(2, even, (N_RET, RET_DK, RET_DV))
    new_ret_bwd = stack(3, even, (N_RET, RET_DK, RET_DV))
    new_na_k = stack(0, odd, (seq, N_NA, HEAD_DIM))
    new_na_v = stack(1, odd, (seq, N_NA, HEAD_DIM))
    return (y_prompt, y_sample, new_gqa_k, new_gqa_v, new_ret_fwd, new_ret_bwd, new_na_k, new_na_v)
```

```python
import functools

import numpy as np
import jax
import jax.numpy as jnp
from jax import lax
from jax.experimental import pallas as pl
from jax.experimental.pallas import tpu as pltpu

F32 = jnp.float32
BF16 = jnp.bfloat16

GRID_W = 64
N_SUB = 3
HEAD_DIM = 64
N_RET = 8
RET_DK = 64
RET_DV = 64
RET_GN_EPS = 1e-5
N_Q = 8
N_KV = 2
Q_GROUP = N_Q // N_KV
ROPE_THETA = 10000.0
N_NA = 16
NA_KR_MAX = 8
NA_KC = 16
EPS = 1e-6
RET_W = N_RET * RET_DK
RET_VW = N_RET * RET_DV
GQA_W = N_Q * HEAD_DIM
KV_W = N_KV * HEAD_DIM
NA_W = N_NA * HEAD_DIM

LANES = 128
VMEM_LIMIT = 56 * 1024 * 1024
RET_CHUNK = 256
RET_UNROLL = 4
NA_ROWS_PER_STEP = 4
FFN_SUBTILES = 2
NA_UNROLL = 4
GQA_TQ = 256
GQA_KEY_CHUNK = 1024
MASK_VALUE = -1e30
LOG2E = 1.4426950408889634
SCORE_SCALE = HEAD_DIM ** -0.5 * LOG2E


def _cparams(n_grid):
    return pltpu.CompilerParams(dimension_semantics=("parallel",) * n_grid,
                                vmem_limit_bytes=VMEM_LIMIT)


def _const_spec(shape):
    nd = len(shape)
    return pl.BlockSpec(shape, lambda *_: (0,) * nd, pipeline_mode=pl.Buffered(1))


def _silu(x):
    return x * jax.nn.sigmoid(x)


def _rms(x, g):
    ms = jnp.mean(x * x, axis=-1, keepdims=True)
    return x * lax.rsqrt(ms + EPS) * g


def _modulated(x, m, sub, g_pre):
    shift = m[3 * sub:3 * sub + 1]
    scale = m[3 * sub + 1:3 * sub + 2]
    return _rms(x, g_pre) * (1.0 + scale) + shift


def _dot(a, b):
    return jnp.dot(a, b, preferred_element_type=F32)


def _dot_nt(a, b):
    return lax.dot_general(a, b, (((1,), (1,)), ((), ())), preferred_element_type=F32)


def _dot_tn(a, b):
    return lax.dot_general(a, b, (((0,), (0,)), ((), ())), preferred_element_type=F32)


def _low_half(shape):
    return lax.broadcasted_iota(jnp.int32, shape, len(shape) - 1) < HEAD_DIM


def _mod_body(c_ref, w_ref, b_ref, o_ref):
    a = _silu(c_ref[...]).astype(BF16)
    o_ref[0] = _dot(a, w_ref[0].astype(BF16)) + b_ref[0]


def _modulation(cond, mod_w, mod_b):
    depth, d, n = mod_w.shape
    rows = cond.shape[0]
    tn = 1152 if n % 1152 == 0 else n
    return pl.pallas_call(
        _mod_body,
        grid=(depth, n // tn),
        in_specs=[pl.BlockSpec((rows, d), lambda l, j: (0, 0)),
                  pl.BlockSpec((1, d, tn), lambda l, j: (l, 0, j)),
                  pl.BlockSpec((1, 1, tn), lambda l, j: (l, 0, j))],
        out_specs=pl.BlockSpec((1, rows, tn), lambda l, j: (l, 0, j)),
        out_shape=jax.ShapeDtypeStruct((depth, rows, n), F32),
        compiler_params=_cparams(2),
        name="modulation",
    )(cond, mod_w, mod_b.reshape(depth, 1, n))


def _tok_spec(tm, width):
    return pl.BlockSpec((1, tm, width), lambda b, i: (b, i, 0))


def _mod_spec(mod):
    nm, rows, d = mod.shape
    if nm == 1:
        return pl.BlockSpec((1, rows, d), lambda b, i: (0, 0, 0))
    return pl.BlockSpec((1, rows, d), lambda b, i: (b, 0, 0))


def _token_tile(seq_len):
    return 512 if seq_len % 512 == 0 else seq_len


def _ffn_body(*refs, sub, dff, n_seg, n_sub):
    x_ref, mod_ref, gpre_ref, gpost_ref, win_ref, wout_ref = refs[:6]
    rest = refs[6:]
    if n_seg:
        gmix_ref = rest[0]
        seg_refs = rest[1:1 + n_seg]
        wmix_refs = rest[1 + n_seg:1 + 2 * n_seg]
    o_ref = refs[-1]
    m = mod_ref[0]
    tm = x_ref.shape[1]
    ts = tm // n_sub
    rows = [slice(t * ts, (t + 1) * ts) for t in range(n_sub)]

    def residual(t):
        x = x_ref[0, rows[t], :]
        if n_seg:
            y = None
            for s_ref, w_ref in zip(seg_refs, wmix_refs):
                part = _dot(s_ref[0, rows[t], :].astype(BF16), w_ref[...])
                y = part if y is None else y + part
            x = x + m[5:6] * _rms(y, gmix_ref[...])
        return x

    def hidden(t, x):
        h = _modulated(x, m, sub, gpre_ref[...]).astype(BF16)
        gu = _dot(h, win_ref[...])
        return (_silu(gu[:, :dff]) * gu[:, dff:]).astype(BF16)

    def finish(t, x, a):
        y = _dot(a, wout_ref[...])
        gate = m[3 * sub + 2:3 * sub + 3]
        o_ref[0, rows[t], :] = x + (0.5 * gate) * _rms(y, gpost_ref[...])

    xs = [residual(t) for t in range(n_sub)]
    acts = [hidden(t, xs[t]) for t in range(n_sub)]
    for t in range(n_sub):
        finish(t, xs[t], acts[t])


def _ffn(x, mod, sub, g_pre, g_post, w_in, w_out, mixer=None):
    b, seq_len, d = x.shape
    dff = w_out.shape[0]
    tm = _token_tile(seq_len)
    ins = [x, mod, g_pre.reshape(1, d), g_post.reshape(1, d), w_in, w_out]
    in_specs = [_tok_spec(tm, d), _mod_spec(mod), _const_spec((1, d)), _const_spec((1, d)),
                _const_spec(w_in.shape), _const_spec(w_out.shape)]
    n_seg = 0
    if mixer is not None:
        g_mix, segs, ws = mixer
        n_seg = len(segs)
        ins += [g_mix.reshape(1, d)] + list(segs) + list(ws)
        in_specs += ([_const_spec((1, d))] + [_tok_spec(tm, s.shape[-1]) for s in segs]
                     + [_const_spec(w.shape) for w in ws])
    n_sub = FFN_SUBTILES if tm % (8 * FFN_SUBTILES) == 0 else 1
    return pl.pallas_call(
        functools.partial(_ffn_body, sub=sub, dff=dff, n_seg=n_seg, n_sub=n_sub),
        grid=(b, seq_len // tm),
        in_specs=in_specs,
        out_specs=_tok_spec(tm, d),
        out_shape=jax.ShapeDtypeStruct(x.shape, F32),
        compiler_params=_cparams(2),
        name="ffn_half" if mixer is None else "mixer_out_ffn_half",
    )(*ins)


def _head_rms(a, g, bd):
    a2 = a * a
    hi = a2.astype(BF16)
    lo = (a2 - hi.astype(F32)).astype(BF16)
    ss = _dot(hi, bd) + _dot(lo, bd)
    return a * lax.rsqrt(ss * (1.0 / HEAD_DIM) + EPS) * g


def _rope(a, cos, sin):
    quarter = HEAD_DIM // 4
    outs = []
    for j in range(a.shape[1] // LANES):
        aj = a[:, j * LANES:(j + 1) * LANES]
        up = pltpu.roll(aj, LANES - quarter, axis=1)
        dn = pltpu.roll(aj, quarter, axis=1)
        lane = lax.broadcasted_iota(jnp.int32, aj.shape, 1)
        partner = jnp.where((lane % (2 * quarter)) < quarter, up, dn)
        outs.append(aj * cos + partner * sin)
    return outs[0] if len(outs) == 1 else jnp.concatenate(outs, axis=1)


def _ab_in_body(*refs, rope):
    if rope:
        (x_ref, mod_ref, gpre_ref, w_ref, qn_ref, kn_ref, bd_ref, cos_ref, sin_ref,
         rq_o, rk_o, rv_o, sg_o, gq_o, gk_o, gv_o) = refs
    else:
        (x_ref, mod_ref, gpre_ref, w_ref, qn_ref, kn_ref, bd_ref,
         rq_o, rk_o, rv_o, sg_o, gq_o, gk_o, gv_o) = refs
    h = _modulated(x_ref[0], mod_ref[0], 1, gpre_ref[...]).astype(BF16)
    p = _dot(h, w_ref[...])
    o = 0
    rq_o[0] = p[:, o:o + RET_W].astype(rq_o.dtype)
    o += RET_W
    rk_o[0] = (p[:, o:o + RET_W] * (RET_DK ** -0.5)).astype(rk_o.dtype)
    o += RET_W
    rv_o[0] = p[:, o:o + RET_VW].astype(rv_o.dtype)
    o += RET_VW
    sg_o[0] = _silu(p[:, o:o + RET_VW]).astype(sg_o.dtype)
    o += RET_VW
    gq = p[:, o:o + GQA_W]
    o += GQA_W
    gk = p[:, o:o + KV_W]
    o += KV_W
    gv = p[:, o:o + KV_W]
    bd = bd_ref[...]
    qh = _head_rms(gq, qn_ref[...], bd)
    kh = _head_rms(gk, kn_ref[...], bd[:KV_W, :KV_W])
    if rope:
        qh = _rope(qh, cos_ref[...], sin_ref[...])
        kh = _rope(kh, cos_ref[...], sin_ref[...])
    gq_o[0] = (qh * SCORE_SCALE).astype(gq_o.dtype)
    gk_o[0] = kh.astype(gk_o.dtype)
    gv_o[0] = gv.astype(gv_o.dtype)


def _ab_in(x, mod, g_pre, w, qn, kn, rope_tables, kv_dtype):
    b, seq_len, d = x.shape
    tm = _token_tile(seq_len)
    rope = rope_tables is not None
    bd = jnp.asarray(np.kron(np.eye(N_Q), np.ones((HEAD_DIM, HEAD_DIM))), BF16)
    ins = [x, mod, g_pre.reshape(1, d), w, jnp.tile(qn, N_Q).reshape(1, GQA_W),
           jnp.tile(kn, N_KV).reshape(1, KV_W), bd]
    in_specs = [_tok_spec(tm, d), _mod_spec(mod), _const_spec((1, d)), _const_spec(w.shape),
                _const_spec((1, GQA_W)), _const_spec((1, KV_W)), _const_spec(bd.shape)]
    if rope:
        ins += list(rope_tables)
        in_specs += [pl.BlockSpec((tm, LANES), lambda b, i: (i, 0))] * 2
    widths = (RET_W, RET_W, RET_VW, RET_VW, GQA_W, KV_W, KV_W)
    dtypes = (BF16, BF16, BF16, BF16, BF16, kv_dtype, kv_dtype)
    return pl.pallas_call(
        functools.partial(_ab_in_body, rope=rope),
        grid=(b, seq_len // tm),
        in_specs=in_specs,
        out_specs=[_tok_spec(tm, wd) for wd in widths],
        out_shape=[jax.ShapeDtypeStruct((b, seq_len, wd), dt) for wd, dt in zip(widths, dtypes)],
        compiler_params=_cparams(2),
        name="ab_in_proj",
    )(*ins)


def _rope_tables(seq_len):
    t = jnp.arange(seq_len)
    quarter = HEAD_DIM // 4
    inv = ROPE_THETA ** (-jnp.arange(quarter, dtype=F32) / quarter)
    ang_r = (t // GRID_W).astype(F32)[:, None] * inv[None, :]
    ang_c = (t % GRID_W).astype(F32)[:, None] * inv[None, :]
    cos = jnp.concatenate([jnp.cos(ang_r)] * 2 + [jnp.cos(ang_c)] * 2, axis=1)
    sin = jnp.concatenate([-jnp.sin(ang_r), jnp.sin(ang_r), -jnp.sin(ang_c), jnp.sin(ang_c)], axis=1)
    return jnp.tile(cos, (1, LANES // HEAD_DIM)), jnp.tile(sin, (1, LANES // HEAD_DIM))


def _ret_body(lgf_ref, lgb_ref, rq_ref, rk_ref, rv_ref, sg_ref, gn_ref, sf_ref, sb_ref,
              y_ref, ff_ref, fb_ref, sbs_ref, *, chunk, n_chunks):
    pair = pl.program_id(1)
    c = chunk
    lo_row = _low_half((1, LANES))
    lgf = jnp.where(lo_row, lgf_ref[2 * pair], lgf_ref[2 * pair + 1])
    lgb = jnp.where(lo_row, lgb_ref[2 * pair], lgb_ref[2 * pair + 1])
    idx = lax.broadcasted_iota(jnp.int32, (c, 1), 0).astype(F32)
    qdec_f = jnp.exp(lgf * (idx + 1.0))
    kdec_f = jnp.exp(lgf * (c - 1.0 - idx))
    cdec_f = jnp.exp(lgf * float(c))
    qdec_b = jnp.exp(lgb * (c - idx))
    kdec_b = jnp.exp(lgb * idx)
    cdec_b = jnp.exp(lgb * float(c))
    diff = (lax.broadcasted_iota(jnp.int32, (c, c), 0)
            - lax.broadcasted_iota(jnp.int32, (c, c), 1)).astype(F32)
    dmat = [jnp.exp(jnp.where(diff >= 0, lgf_ref[2 * pair + h], -lgb_ref[2 * pair + h]) * diff)
            for h in range(2)]
    rr = lax.broadcasted_iota(jnp.int32, (LANES, LANES), 0) < HEAD_DIM
    cc = lax.broadcasted_iota(jnp.int32, (LANES, LANES), 1) < HEAD_DIM
    same_head = rr == cc
    lo = _low_half((c, LANES))

    def load(ref, n):
        return ref[0, pl.ds(pl.multiple_of(n * c, c), c), :]

    def bwd_step(t, state):
        n = n_chunks - 1 - t
        sbs_ref[n] = state
        k = load(rk_ref, n).astype(F32)
        v = load(rv_ref, n).astype(BF16)
        kv = _dot_tn((k * kdec_b).astype(BF16), v)
        return state * cdec_b + jnp.where(same_head, kv, 0.0)

    unroll = int(np.gcd(n_chunks, RET_UNROLL))
    fb_ref[0, 0] = lax.fori_loop(0, n_chunks, bwd_step, sb_ref[0, 0], unroll=unroll)

    gn = gn_ref[...]

    def fwd_step(n, state):
        q = load(rq_ref, n).astype(BF16)
        k = load(rk_ref, n).astype(BF16)
        v = load(rv_ref, n).astype(BF16)
        qf = q.astype(F32)
        o = (_dot((qf * qdec_f).astype(BF16), state.astype(BF16))
             + _dot((qf * qdec_b).astype(BF16), sbs_ref[n].astype(BF16)))
        zero = jnp.zeros_like(q)
        intra = []
        for h in range(2):
            qh = jnp.where(lo, q, zero) if h == 0 else jnp.where(lo, zero, q)
            s = _dot_nt(qh, k) * dmat[h]
            intra.append(_dot(s.astype(BF16), v))
        o = o + jnp.where(lo, intra[0], intra[1])
        def head_mean(a):
            m0 = jnp.sum(jnp.where(lo, a, 0.0), axis=-1, keepdims=True)
            m1 = jnp.sum(jnp.where(lo, 0.0, a), axis=-1, keepdims=True)
            return jnp.where(lo, m0, m1) * (1.0 / RET_DV)
        dev = o - head_mean(o)
        var = head_mean(dev * dev)
        y = dev * lax.rsqrt(var + RET_GN_EPS) * gn * load(sg_ref, n).astype(F32)
        y_ref[0, pl.ds(pl.multiple_of(n * c, c), c), :] = y.astype(y_ref.dtype)
        kv = _dot_tn((k.astype(F32) * kdec_f).astype(BF16), v)
        return state * cdec_f + jnp.where(same_head, kv, 0.0)

    ff_ref[0, 0] = lax.fori_loop(0, n_chunks, fwd_step, sf_ref[0, 0], unroll=unroll)


def _pair_states(s):
    b, nh, dk, dv = s.shape
    s = s.reshape(b, nh // 2, 2, dk, dv)
    z = jnp.zeros_like(s[:, :, 0])
    top = jnp.concatenate([s[:, :, 0], z], axis=-1)
    bot = jnp.concatenate([z, s[:, :, 1]], axis=-1)
    return jnp.concatenate([top, bot], axis=-2)


def _unpair_states(s):
    b, npair = s.shape[:2]
    a = s[:, :, :RET_DK, :RET_DV]
    d = s[:, :, RET_DK:, RET_DV:]
    return jnp.stack([a, d], axis=2).reshape(b, 2 * npair, RET_DK, RET_DV)


def _retention(rq, rk, rv, sg, gn, lg_f, lg_b, s_f, s_b):
    b, seq_len, _ = rq.shape
    c = RET_CHUNK if seq_len % RET_CHUNK == 0 else seq_len
    nc = seq_len // c
    npair = N_RET // 2
    seq_spec = pl.BlockSpec((1, seq_len, LANES), lambda i, p: (i, 0, p))
    st_spec = pl.BlockSpec((1, 1, LANES, LANES), lambda i, p: (i, p, 0, 0))
    smem = pl.BlockSpec(memory_space=pltpu.SMEM)
    st_shape = jax.ShapeDtypeStruct((b, npair, LANES, LANES), F32)
    y, fin_f, fin_b = pl.pallas_call(
        functools.partial(_ret_body, chunk=c, n_chunks=nc),
        grid=(b, npair),
        in_specs=[smem, smem, seq_spec, seq_spec, seq_spec, seq_spec,
                  pl.BlockSpec((1, LANES), lambda i, p: (0, p)), st_spec, st_spec],
        out_specs=[seq_spec, st_spec, st_spec],
        out_shape=[jax.ShapeDtypeStruct((b, seq_len, RET_VW), BF16), st_shape, st_shape],
        scratch_shapes=[pltpu.VMEM((nc, LANES, LANES), F32)],
        compiler_params=_cparams(2),
        name="retention",
    )(lg_f, lg_b, rq, rk, rv, sg, gn.reshape(1, RET_VW), _pair_states(s_f), _pair_states(s_b))
    return y, _unpair_states(fin_f), _unpair_states(fin_b)


def _staged(items, scores, probs, accumulate, ahead=2):
    n = len(items)
    queue = [scores(items[i]) for i in range(min(ahead, n))]
    pending = None
    for i in range(n):
        sc = queue.pop(0)
        if i + ahead < n:
            queue.append(scores(items[i + ahead]))
        ps = probs(items[i], sc)
        if pending is not None:
            accumulate(*pending)
        pending = (items[i], ps)
    accumulate(*pending)


def _row_probs(s_list):
    m = s_list[0].max(axis=-1, keepdims=True)
    for s in s_list[1:]:
        m = jnp.maximum(m, s.max(axis=-1, keepdims=True))
    ps = [jnp.exp2(s - m) for s in s_list]
    den = ps[0].sum(axis=-1, keepdims=True)
    for p in ps[1:]:
        den = den + p.sum(axis=-1, keepdims=True)
    return [p.astype(BF16) for p in ps], 1.0 / den


def _only_head(q, h):
    lo = _low_half(q.shape)
    zero = jnp.zeros_like(q)
    return jnp.where(lo, q, zero) if h == 0 else jnp.where(lo, zero, q)


def _attn_body(q_ref, k_ref, v_ref, o_ref, *, n_groups, kv_shared):
    tq = q_ref.shape[1]
    lo = _low_half((tq, LANES))
    outs = {}

    def lanes(j):
        return slice(j * LANES, (j + 1) * LANES)

    def scores(item):
        j, h = item
        k = k_ref[0] if kv_shared else k_ref[0, :, lanes(j)]
        return _dot_nt(_only_head(q_ref[0, :, lanes(j)].astype(BF16), h), k.astype(BF16))

    def probs(item, s):
        return _row_probs([s])

    def accumulate(item, pr):
        j, h = item
        ps, inv_den = pr
        v = v_ref[0] if kv_shared else v_ref[0, :, lanes(j)]
        outs[h] = _dot(ps[0], v.astype(BF16)) * inv_den
        if h == 1:
            o_ref[0, :, lanes(j)] = jnp.where(lo, outs[0], outs[1]).astype(o_ref.dtype)

    _staged([(j, h) for j in range(n_groups) for h in range(2)], scores, probs, accumulate)


def _col_max(x, slab=64):
    n, t = x.shape
    if n > slab and n % slab == 0:
        x = x.reshape(n // slab, slab, t).max(axis=0)
    return x.max(axis=0, keepdims=True)


def _gqa_t_body(q_ref, kn_ref, kc_ref, vtn_ref, vtc_ref, o_ref, *, n_groups, chunk):
    tq = q_ref.shape[1]
    top = lax.broadcasted_iota(jnp.int32, (LANES, tq), 0) < HEAD_DIM
    ln, lc = kn_ref.shape[1], kc_ref.shape[1]
    chunks = [(kn_ref, vtn_ref, c * chunk, chunk) for c in range(ln // chunk)] + [(kc_ref, vtc_ref, 0, lc)]
    items = [(j, h, c) for j in range(n_groups) for h in range(2) for c in range(len(chunks))]

    def scores(item):
        j, h, c = item
        k_ref, _, off, size = chunks[c]
        qh = _only_head(q_ref[0, :, j * LANES:(j + 1) * LANES], h)
        return _dot_nt(k_ref[0, off:off + size, :], qh)

    outs = {}
    state = {"m": None, "acc": None}

    def probs(item, st):
        c = item[2]
        mc = _col_max(st)
        m_new = mc if c == 0 else jnp.maximum(state["m"], mc)
        alpha = None if c == 0 else jnp.exp2(state["m"] - m_new)
        state["m"] = m_new
        return jnp.exp2(st - m_new).astype(BF16), alpha

    def accumulate(item, ps):
        j, h, c = item
        p, alpha = ps
        _, vt_ref, off, size = chunks[c]
        o = _dot(vt_ref[0, h, :, off:off + size], p)
        acc = o if c == 0 else state["acc"] * alpha + o
        state["acc"] = acc
        if c == len(chunks) - 1:
            den = acc[HEAD_DIM:HEAD_DIM + 1] if h == 0 else acc[0:1]
            outs[h] = acc * (1.0 / den)
            if h == 1:
                sl = slice(j * LANES, (j + 1) * LANES)
                o_ref[0, :, sl] = jnp.where(top, outs[0], outs[1]).T.astype(o_ref.dtype)

    _staged(items, scores, probs, accumulate)


def _vt_with_ones(v):
    vt = jnp.swapaxes(v, 1, 2)
    top = (np.arange(KV_W) < HEAD_DIM)[None, :, None]
    one = jnp.ones_like(vt)
    return jnp.stack([jnp.where(top, vt, one), jnp.where(top, one, vt)], axis=1)


def _gqa_attention_t(q, k_new, v_new, k_ctx, v_ctx, tq):
    b, lq, wq = q.shape
    ln, lc = k_new.shape[1], k_ctx.shape[1]
    chunk = GQA_KEY_CHUNK if ln % GQA_KEY_CHUNK == 0 else ln
    whole3 = lambda i, t: (i, 0, 0)
    whole4 = lambda i, t: (i, 0, 0, 0)
    return pl.pallas_call(
        functools.partial(_gqa_t_body, n_groups=wq // LANES, chunk=chunk),
        grid=(b, lq // tq),
        in_specs=[pl.BlockSpec((1, tq, wq), lambda i, t: (i, t, 0)),
                  pl.BlockSpec((1, ln, KV_W), whole3), pl.BlockSpec((1, lc, KV_W), whole3),
                  pl.BlockSpec((1, 2, KV_W, ln), whole4), pl.BlockSpec((1, 2, KV_W, lc), whole4)],
        out_specs=pl.BlockSpec((1, tq, wq), lambda i, t: (i, t, 0)),
        out_shape=jax.ShapeDtypeStruct((b, lq, wq), BF16),
        compiler_params=_cparams(2),
        name="gqa_attention_t",
    )(q, k_new, k_ctx, _vt_with_ones(v_new), _vt_with_ones(v_ctx))


def _attention(q, k, v, kv_shared, tq):
    b, lq, wq = q.shape
    lk, wk = k.shape[1:]
    return pl.pallas_call(
        functools.partial(_attn_body, n_groups=wq // LANES, kv_shared=kv_shared),
        grid=(b, lq // tq),
        in_specs=[pl.BlockSpec((1, tq, wq), lambda i, t: (i, t, 0)),
                  pl.BlockSpec((1, lk, wk), lambda i, t: (i, 0, 0)),
                  pl.BlockSpec((1, lk, wk), lambda i, t: (i, 0, 0))],
        out_specs=pl.BlockSpec((1, tq, wq), lambda i, t: (i, t, 0)),
        out_shape=jax.ShapeDtypeStruct((b, lq, wq), BF16),
        compiler_params=_cparams(2),
        name="dense_attention",
    )(q, k, v)


def _na_plan(rows):
    g = NA_ROWS_PER_STEP
    kr = min(NA_KR_MAX, rows)
    u = min(g + kr - 1, rows)
    n_steps = rows // g
    bases, vids, variants, keys = [], [], [], {}
    for s in range(n_steps):
        base = int(np.clip(s * g - kr // 2, 0, rows - u))
        r = s * g + np.arange(g)[:, None]
        krow = base + np.arange(u)[None, :]
        r0 = np.clip(r - kr // 2, 0, rows - kr)
        valid = (krow >= r0) & (krow < r0 + kr)
        dr = np.where(valid, krow - r + (NA_KR_MAX - 1), 0)
        key = (dr.tobytes(), valid.tobytes())
        if key not in keys:
            keys[key] = len(variants)
            variants.append((dr, valid))
        bases.append(base)
        vids.append(keys[key])
    return u, bases, vids, variants


def _na_bias_tables(rpb, variants, u):
    g = NA_ROWS_PER_STEP
    nh, n_dr, n_dc = rpb.shape
    cols = np.arange(GRID_W)
    c0 = np.clip(cols - NA_KC // 2, 0, GRID_W - NA_KC)
    kc = np.arange(GRID_W)[None, :]
    cvalid = (kc >= c0[:, None]) & (kc < c0[:, None] + NA_KC)
    dc = kc - cols[:, None] + (NA_KC - 1)
    sel_c = cvalid[None] & (dc[None] == np.arange(n_dc)[:, None, None])
    tables = []
    for dr, rvalid in variants:
        sel_r = rvalid[..., None] & (dr[..., None] == np.arange(n_dr))
        t = jnp.einsum("gud,hde,eck->hgcuk", jnp.asarray(sel_r, F32), rpb.astype(F32),
                       jnp.asarray(sel_c, F32), precision=lax.Precision.HIGHEST)
        ok = rvalid[:, None, :, None] & cvalid[None, :, None, :]
        tables.append(jnp.where(ok[None], t * LOG2E, MASK_VALUE).reshape(nh, g * GRID_W, u * GRID_W))
    return jnp.stack(tables)


def _na_body(base_ref, vid_ref, q_ref, k_ref, v_ref, kc_ref, vc_ref, bias_ref, o_ref, *, n_steps, tq, tk):
    lo = _low_half((tq, LANES))
    kc = kc_ref[0].astype(BF16)
    vc = vc_ref[0].astype(BF16)

    unroll = int(np.gcd(n_steps, NA_UNROLL))

    def trip(t, carry):
        items = [(u, h) for u in range(unroll) for h in range(2)]
        steps = [t * unroll + u for u in range(unroll)]
        outs = {}

        def q_rows(u):
            return pl.ds(pl.multiple_of(steps[u] * tq, tq), tq)

        def k_rows(u):
            return pl.ds(pl.multiple_of(base_ref[steps[u]] * GRID_W, GRID_W), tk)

        def scores(item):
            u, h = item
            qh = _only_head(q_ref[0, q_rows(u), :], h)
            s_win = _dot_nt(qh, k_ref[0, k_rows(u), :]) + bias_ref[vid_ref[steps[u]], h]
            return s_win, _dot_nt(qh, kc)

        def probs(item, sc):
            return _row_probs(list(sc))

        def accumulate(item, pr):
            u, h = item
            ps, inv_den = pr
            outs[h] = (_dot(ps[0], v_ref[0, k_rows(u), :]) + _dot(ps[1], vc)) * inv_den
            if h == 1:
                o_ref[0, q_rows(u), :] = jnp.where(lo, outs[0], outs[1]).astype(o_ref.dtype)

        _staged(items, scores, probs, accumulate)
        return carry

    lax.fori_loop(0, n_steps // unroll, trip, 0)


def _na_attention(q, k, v, k_ctx, v_ctx, rpb):
    b, seq_len, w = q.shape
    rows = seq_len // GRID_W
    lc = k_ctx.shape[1]
    u, bases, vids, variants = _na_plan(rows)
    bias = _na_bias_tables(rpb, variants, u)
    nv = bias.shape[0]
    tq = NA_ROWS_PER_STEP * GRID_W
    tk = u * GRID_W
    n_steps = rows // NA_ROWS_PER_STEP
    smem = pl.BlockSpec(memory_space=pltpu.SMEM)
    seq_spec = pl.BlockSpec((1, seq_len, LANES), lambda p, i: (i, 0, p))
    ctx_spec = pl.BlockSpec((1, lc, LANES), lambda p, i: (i, 0, p))
    return pl.pallas_call(
        functools.partial(_na_body, n_steps=n_steps, tq=tq, tk=tk),
        grid=(w // LANES, b),
        in_specs=[smem, smem, seq_spec, seq_spec, seq_spec, ctx_spec, ctx_spec,
                  pl.BlockSpec((nv, 2, tq, tk), lambda p, i: (0, p, 0, 0))],
        out_specs=seq_spec,
        out_shape=jax.ShapeDtypeStruct((b, seq_len, w), BF16),
        compiler_params=_cparams(2),
        name="neighbourhood_attention",
    )(jnp.asarray(bases, jnp.int32), jnp.asarray(vids, jnp.int32), q, k, v, k_ctx, v_ctx, bias)


def _na_in_body(x_ref, mod_ref, gpre_ref, w_ref, q_o, k_o, v_o):
    h = _modulated(x_ref[0], mod_ref[0], 1, gpre_ref[...]).astype(BF16)
    p = _dot(h, w_ref[...])
    q_o[0] = (p[:, :NA_W] * SCORE_SCALE).astype(q_o.dtype)
    k_o[0] = p[:, NA_W:2 * NA_W].astype(k_o.dtype)
    v_o[0] = p[:, 2 * NA_W:].astype(v_o.dtype)


def _na_in(x, mod, g_pre, w, kv_dtype):
    b, seq_len, d = x.shape
    tm = _token_tile(seq_len)
    return pl.pallas_call(
        _na_in_body,
        grid=(b, seq_len // tm),
        in_specs=[_tok_spec(tm, d), _mod_spec(mod), _const_spec((1, d)), _const_spec(w.shape)],
        out_specs=[_tok_spec(tm, NA_W)] * 3,
        out_shape=[jax.ShapeDtypeStruct((b, seq_len, NA_W), dt) for dt in (BF16, kv_dtype, kv_dtype)],
        compiler_params=_cparams(2),
        name="na_in_proj",
    )(x, mod, g_pre.reshape(1, d), w)


_Q_HEAD_ORDER = [h for j in range(Q_GROUP) for h in range(j, N_Q, Q_GROUP)]


def _permute_q_cols(w_in):
    cols = np.arange(w_in.shape[1])
    q0 = 2 * RET_W + 2 * RET_VW
    perm = np.concatenate([np.arange(h * HEAD_DIM, (h + 1) * HEAD_DIM) for h in _Q_HEAD_ORDER])
    cols[q0:q0 + GQA_W] = q0 + perm
    return w_in[:, cols]


def _permute_att_rows(w_att):
    perm = np.concatenate([np.arange(h * HEAD_DIM, (h + 1) * HEAD_DIM) for h in _Q_HEAD_ORDER])
    return w_att[perm]


def _trunk(x, mods, ctx_layers, wts):
    b, seq_len, d = x.shape
    is_ctx = ctx_layers is None
    depth = len(mods)
    flat = (lambda a: a.reshape(1, b * seq_len, a.shape[-1])) if is_ctx else (lambda a: a)
    unflat = (lambda a: a.reshape(b, seq_len, a.shape[-1])) if is_ctx else (lambda a: a)
    kv_dtype = F32 if is_ctx else BF16
    aux = []
    x = flat(x)
    for l in range(depth):
        mod = mods[l]
        i = l // 2
        x = _ffn(x, mod, 0, wts["norm_pre"][l, 0], wts["norm_post"][l, 0],
                 wts["ffn_w_in"][l][0], wts["ffn_w_out"][l][0])
        if l % 2 == 0:
            rope_tables = None if is_ctx else _rope_tables(seq_len)
            rq, rk, rv, sg, gq, gk, gv = _ab_in(x, mod, wts["norm_pre"][l, 1], wts["ab_w_in"][i],
                                               wts["gqa_q_norm"][i], wts["gqa_k_norm"][i],
                                               rope_tables, kv_dtype)
            rq, rk, rv, sg, gq, gk, gv = [unflat(a) for a in (rq, rk, rv, sg, gq, gk, gv)]
            if is_ctx:
                s_f = jnp.zeros((b, N_RET, RET_DK, RET_DV), F32)
                s_b = s_f
            else:
                ck, cv, s_f, s_b = ctx_layers[l]
                lc = ck.shape[1]
            y_ret, fin_f, fin_b = _retention(rq, rk, rv, sg, wts["ret_gn"][i], wts["lg_f"][i],
                                             wts["lg_b"][i], s_f.astype(F32), s_b.astype(F32))
            if is_ctx:
                y_att = _attention(gq, gk, gv, True, seq_len)
            else:
                y_att = _gqa_attention_t(gq, gk, gv, ck.reshape(b, lc, KV_W).astype(BF16),
                                         cv.reshape(b, lc, KV_W).astype(BF16), GQA_TQ)
            mixer = (wts["norm_post"][l, 1], [flat(y_ret), flat(y_att)],
                     [wts["ab_w_out_ret"][i], wts["ab_w_out_att"][i]])
            aux.append((gk, gv, fin_f, fin_b) if is_ctx else None)
        else:
            q, k, v = _na_in(x, mod, wts["norm_pre"][l, 1], wts["na_w_qkv"][i], kv_dtype)
            q, k, v = unflat(q), unflat(k), unflat(v)
            if is_ctx:
                y = _attention(q, k, v, False, seq_len)
                aux.append((k, v))
            else:
                ck, cv = ctx_layers[l]
                lc = ck.shape[1]
                y = _na_attention(q, k, v, ck.reshape(b, lc, NA_W).astype(BF16),
                                  cv.reshape(b, lc, NA_W).astype(BF16), wts["na_rpb"][i])
                aux.append(None)
            mixer = (wts["norm_post"][l, 1], [flat(y)], [wts["na_w_out"][i]])
        x = _ffn(x, mod, 2, wts["norm_pre"][l, 2], wts["norm_post"][l, 2],
                 wts["ffn_w_in"][l][1], wts["ffn_w_out"][l][1], mixer=mixer)
    return unflat(x), aux


def kernel(x_prompt, x_sample, cache_gqa_k, cache_gqa_v, state_ret_fwd, state_ret_bwd, cache_na_k,
           cache_na_v, c, c_ctx, mod_w, mod_b, norm_pre, norm_post, ffn_w_in, ffn_w_out, ab_w_in,
           ab_w_out, ret_decay_fwd, ret_decay_bwd, ret_gn, gqa_q_norm, gqa_k_norm, na_w_qkv, na_w_out,
           na_rpb):
    depth = mod_w.shape[0]
    d = x_prompt.shape[-1]
    n_dec = c.shape[0]
    batch, seq = x_prompt.shape[:2]

    n_cond = n_dec + 1
    pad = (-n_cond) % 8
    cond = jnp.concatenate([c, c_ctx[None, :], jnp.zeros((pad, d), F32)], axis=0)
    mod_all = _modulation(cond, mod_w, mod_b).reshape(depth, n_cond + pad, 3 * N_SUB, d)
    mods_sample = [mod_all[l, :n_dec] for l in range(depth)]
    mods_prompt = [mod_all[l, n_dec:n_dec + 1] for l in range(depth)]

    n_even = ab_w_in.shape[0]
    wts = {
        "norm_pre": norm_pre, "norm_post": norm_post,
        "ffn_w_in": [[ffn_w_in[l, s].astype(BF16) for s in range(2)] for l in range(depth)],
        "ffn_w_out": [[ffn_w_out[l, s].astype(BF16) for s in range(2)] for l in range(depth)],
        "ab_w_in": [_permute_q_cols(ab_w_in[i]).astype(BF16) for i in range(n_even)],
        "ab_w_out_ret": [ab_w_out[i, :RET_VW].astype(BF16) for i in range(n_even)],
        "ab_w_out_att": [_permute_att_rows(ab_w_out[i, RET_VW:]).astype(BF16) for i in range(n_even)],
        "lg_f": jax.nn.log_sigmoid(ret_decay_fwd.astype(F32)),
        "lg_b": jax.nn.log_sigmoid(ret_decay_bwd.astype(F32)),
        "ret_gn": ret_gn, "gqa_q_norm": gqa_q_norm, "gqa_k_norm": gqa_k_norm,
        "na_w_qkv": [na_w_qkv[i].astype(BF16) for i in range(na_w_qkv.shape[0])],
        "na_w_out": [na_w_out[i].astype(BF16) for i in range(na_w_out.shape[0])],
        "na_rpb": na_rpb,
    }

    y_prompt, ctx_aux = _trunk(x_prompt, mods_prompt, None, wts)

    ctx_layers = []
    for l in range(depth):
        i = l // 2
        if l % 2 == 0:
            ctx_layers.append((cache_gqa_k[:, i], cache_gqa_v[:, i], state_ret_fwd[:, i], state_ret_bwd[:, i]))
        else:
            ctx_layers.append((cache_na_k[:, i], cache_na_v[:, i]))
    y_sample, _ = _trunk(x_sample, mods_sample, ctx_layers, wts)

    def stack(idx, layers, tail):
        return jnp.stack([ctx_aux[l][idx].reshape((batch,) + tail) for l in layers], axis=1)

    even = range(0, depth, 2)
    odd = range(1, depth, 2)
    new_gqa_k = stack(0, even, (seq, N_KV, HEAD_DIM))
    new_gqa_v = stack(1, even, (seq, N_KV, HEAD_DIM))
    new_ret_fwd = stack(2, even, (N_RET, RET_DK, RET_DV))
    new_ret_bwd = stack(3, even, (N_RET, RET_DK, RET_DV))
    new_na_k = stack(0, odd, (seq, N_NA, HEAD_DIM))
    new_na_v = stack(1, odd, (seq, N_NA, HEAD_DIM))
    return (y_prompt, y_sample, new_gqa_k, new_gqa_v, new_ret_fwd, new_ret_bwd, new_na_k, new_na_v)
```

```python
import functools

import numpy as np
import jax
import jax.numpy as jnp
from jax import lax
from jax.experimental import pallas as pl
from jax.experimental.pallas import tpu as pltpu

F32 = jnp.float32
BF16 = jnp.bfloat16

GRID_W = 64
N_SUB = 3
HEAD_DIM = 64
N_RET = 8
RET_DK = 64
RET_DV = 64
RET_GN_EPS = 1e-5
N_Q = 8
N_KV = 2
Q_GROUP = N_Q // N_KV
ROPE_THETA = 10000.0
N_NA = 16
NA_KR_MAX = 8
NA_KC = 16
EPS = 1e-6
RET_W = N_RET * RET_DK
RET_VW = N_RET * RET_DV
GQA_W = N_Q * HEAD_DIM
KV_W = N_KV * HEAD_DIM
NA_W = N_NA * HEAD_DIM

LANES = 128
VMEM_LIMIT = 56 * 1024 * 1024
RET_CHUNK = 256
RET_UNROLL = 4
NA_ROWS_PER_STEP = 4
FFN_SUBTILES = 2
NA_UNROLL = 4
GQA_TQ = 256
GQA_KEY_CHUNK = 1024
MASK_VALUE = -1e30
LOG2E = 1.4426950408889634
SCORE_SCALE = HEAD_DIM ** -0.5 * LOG2E


def _cparams(n_grid):
    return pltpu.CompilerParams(dimension_semantics=("parallel",) * n_grid,
                                vmem_limit_bytes=VMEM_LIMIT)


def _const_spec(shape):
    nd = len(shape)
    return pl.BlockSpec(shape, lambda *_: (0,) * nd, pipeline_mode=pl.Buffered(1))


def _silu(x):
    return x * jax.nn.sigmoid(x)


def _rms(x, g):
    ms = jnp.mean(x * x, axis=-1, keepdims=True)
    return x * lax.rsqrt(ms + EPS) * g


def _modulated(x, m, sub, g_pre):
    shift = m[3 * sub:3 * sub + 1]
    scale = m[3 * sub + 1:3 * sub + 2]
    return _rms(x, g_pre) * (1.0 + scale) + shift


def _dot(a, b):
    return jnp.dot(a, b, preferred_element_type=F32)


def _dot_nt(a, b):
    return lax.dot_general(a, b, (((1,), (1,)), ((), ())), preferred_element_type=F32)


def _dot_tn(a, b):
    return lax.dot_general(a, b, (((0,), (0,)), ((), ())), preferred_element_type=F32)


def _low_half(shape):
    return lax.broadcasted_iota(jnp.int32, shape, len(shape) - 1) < HEAD_DIM


def _mod_body(c_ref, w_ref, b_ref, o_ref):
    a = _silu(c_ref[...]).astype(BF16)
    o_ref[0] = _dot(a, w_ref[0].astype(BF16)) + b_ref[0]


def _modulation(cond, mod_w, mod_b):
    depth, d, n = mod_w.shape
    rows = cond.shape[0]
    tn = 1152 if n % 1152 == 0 else n
    return pl.pallas_call(
        _mod_body,
        grid=(depth, n // tn),
        in_specs=[pl.BlockSpec((rows, d), lambda l, j: (0, 0)),
                  pl.BlockSpec((1, d, tn), lambda l, j: (l, 0, j)),
                  pl.BlockSpec((1, 1, tn), lambda l, j: (l, 0, j))],
        out_specs=pl.BlockSpec((1, rows, tn), lambda l, j: (l, 0, j)),
        out_shape=jax.ShapeDtypeStruct((depth, rows, n), F32),
        compiler_params=_cparams(2),
        name="modulation",
    )(cond, mod_w, mod_b.reshape(depth, 1, n))


def _tok_spec(tm, width):
    return pl.BlockSpec((1, tm, width), lambda b, i: (b, i, 0))


def _mod_spec(mod):
    nm, rows, d = mod.shape
    if nm == 1:
        return pl.BlockSpec((1, rows, d), lambda b, i: (0, 0, 0))
    return pl.BlockSpec((1, rows, d), lambda b, i: (b, 0, 0))


def _token_tile(seq_len):
    return 512 if seq_len % 512 == 0 else seq_len


def _ffn_body(*refs, sub, dff, n_seg, n_sub):
    x_ref, mod_ref, gpre_ref, gpost_ref, win_ref, wout_ref = refs[:6]
    rest = refs[6:]
    if n_seg:
        gmix_ref = rest[0]
        seg_refs = rest[1:1 + n_seg]
        wmix_refs = rest[1 + n_seg:1 + 2 * n_seg]
    o_ref = refs[-1]
    m = mod_ref[0]
    tm = x_ref.shape[1]
    ts = tm // n_sub
    rows = [slice(t * ts, (t + 1) * ts) for t in range(n_sub)]

    def residual(t):
        x = x_ref[0, rows[t], :]
        if n_seg:
            y = None
            for s_ref, w_ref in zip(seg_refs, wmix_refs):
                part = _dot(s_ref[0, rows[t], :].astype(BF16), w_ref[...])
                y = part if y is None else y + part
            x = x + m[5:6] * _rms(y, gmix_ref[...])
        return x

    def hidden(t, x):
        h = _modulated(x, m, sub, gpre_ref[...]).astype(BF16)
        gu = _dot(h, win_ref[...])
        return (_silu(gu[:, :dff]) * gu[:, dff:]).astype(BF16)

    def finish(t, x, a):
        y = _dot(a, wout_ref[...])
        gate = m[3 * sub + 2:3 * sub + 3]
        o_ref[0, rows[t], :] = x + (0.5 * gate) * _rms(y, gpost_ref[...])

    xs = [residual(t) for t in range(n_sub)]
    acts = [hidden(t, xs[t]) for t in range(n_sub)]
    for t in range(n_sub):
        finish(t, xs[t], acts[t])


def _ffn(x, mod, sub, g_pre, g_post, w_in, w_out, mixer=None):
    b, seq_len, d = x.shape
    dff = w_out.shape[0]
    tm = _token_tile(seq_len)
    ins = [x, mod, g_pre.reshape(1, d), g_post.reshape(1, d), w_in, w_out]
    in_specs = [_tok_spec(tm, d), _mod_spec(mod), _const_spec((1, d)), _const_spec((1, d)),
                _const_spec(w_in.shape), _const_spec(w_out.shape)]
    n_seg = 0
    if mixer is not None:
        g_mix, segs, ws = mixer
        n_seg = len(segs)
        ins += [g_mix.reshape(1, d)] + list(segs) + list(ws)
        in_specs += ([_const_spec((1, d))] + [_tok_spec(tm, s.shape[-1]) for s in segs]
                     + [_const_spec(w.shape) for w in ws])
    n_sub = FFN_SUBTILES if tm % (8 * FFN_SUBTILES) == 0 else 1
    return pl.pallas_call(
        functools.partial(_ffn_body, sub=sub, dff=dff, n_seg=n_seg, n_sub=n_sub),
        grid=(b, seq_len // tm),
        in_specs=in_specs,
        out_specs=_tok_spec(tm, d),
        out_shape=jax.ShapeDtypeStruct(x.shape, F32),
        compiler_params=_cparams(2),
        name="ffn_half" if mixer is None else "mixer_out_ffn_half",
    )(*ins)


def _head_rms(a, g, bd):
    a2 = a * a
    hi = a2.astype(BF16)
    lo = (a2 - hi.astype(F32)).astype(BF16)
    ss = _dot(hi, bd) + _dot(lo, bd)
    return a * lax.rsqrt(ss * (1.0 / HEAD_DIM) + EPS) * g


def _rope(a, cos, sin):
    quarter = HEAD_DIM // 4
    outs = []
    for j in range(a.shape[1] // LANES):
        aj = a[:, j * LANES:(j + 1) * LANES]
        up = pltpu.roll(aj, LANES - quarter, axis=1)
        dn = pltpu.roll(aj, quarter, axis=1)
        lane = lax.broadcasted_iota(jnp.int32, aj.shape, 1)
        partner = jnp.where((lane % (2 * quarter)) < quarter, up, dn)
        outs.append(aj * cos + partner * sin)
    return outs[0] if len(outs) == 1 else jnp.concatenate(outs, axis=1)


def _ab_in_body(*refs, rope, aux):
    x_ref, mod_ref, gpre_ref, w_ref, qn_ref, kn_ref, bd_ref = refs[:7]
    refs = refs[7:]
    if rope:
        cos_ref, sin_ref = refs[:2]
        refs = refs[2:]
    rq_o, rk_o, rv_o, sg_o, gq_o, gk_o, gv_o = refs[:7]
    h = _modulated(x_ref[0], mod_ref[0], 1, gpre_ref[...]).astype(BF16)
    p = _dot(h, w_ref[...])
    o = 0
    rq_o[0] = p[:, o:o + RET_W].astype(rq_o.dtype)
    o += RET_W
    rk_o[0] = (p[:, o:o + RET_W] * (RET_DK ** -0.5)).astype(rk_o.dtype)
    o += RET_W
    rv_o[0] = p[:, o:o + RET_VW].astype(rv_o.dtype)
    o += RET_VW
    sg_o[0] = _silu(p[:, o:o + RET_VW]).astype(sg_o.dtype)
    o += RET_VW
    gq = p[:, o:o + GQA_W]
    o += GQA_W
    gk = p[:, o:o + KV_W]
    o += KV_W
    gv = p[:, o:o + KV_W]
    bd = bd_ref[...]
    qh = _head_rms(gq, qn_ref[...], bd)
    kh = _head_rms(gk, kn_ref[...], bd[:KV_W, :KV_W])
    if rope:
        qh = _rope(qh, cos_ref[...], sin_ref[...])
        kh = _rope(kh, cos_ref[...], sin_ref[...])
    gq_o[0] = (qh * SCORE_SCALE).astype(gq_o.dtype)
    gk_o[0] = kh.astype(gk_o.dtype)
    gv_o[0] = gv.astype(gv_o.dtype)
    if aux:
        k5_o, v5_o = refs[7:]
        k5_o[...] = kh.reshape(k5_o.shape)
        v5_o[...] = gv.reshape(v5_o.shape)


def _ab_in(x, mod, g_pre, w, qn, kn, rope_tables, aux):
    b, seq_len, d = x.shape
    tm = _token_tile(seq_len)
    rope = rope_tables is not None
    bd = jnp.asarray(np.kron(np.eye(N_Q), np.ones((HEAD_DIM, HEAD_DIM))), BF16)
    ins = [x, mod, g_pre.reshape(1, d), w, jnp.tile(qn, N_Q).reshape(1, GQA_W),
           jnp.tile(kn, N_KV).reshape(1, KV_W), bd]
    in_specs = [_tok_spec(tm, d), _mod_spec(mod), _const_spec((1, d)), _const_spec(w.shape),
                _const_spec((1, GQA_W)), _const_spec((1, KV_W)), _const_spec(bd.shape)]
    if rope:
        ins += list(rope_tables)
        in_specs += [pl.BlockSpec((tm, LANES), lambda b, i: (i, 0))] * 2
    widths = (RET_W, RET_W, RET_VW, RET_VW, GQA_W, KV_W, KV_W)
    out_specs = [_tok_spec(tm, wd) for wd in widths]
    out_shape = [jax.ShapeDtypeStruct((b, seq_len, wd), BF16) for wd in widths]
    if aux:
        assert b == 1
        out_specs += [pl.BlockSpec((tm, N_KV, HEAD_DIM), lambda b, i: (i, 0, 0))] * 2
        out_shape += [jax.ShapeDtypeStruct((seq_len, N_KV, HEAD_DIM), F32)] * 2
    return pl.pallas_call(
        functools.partial(_ab_in_body, rope=rope, aux=aux),
        grid=(b, seq_len // tm),
        in_specs=in_specs,
        out_specs=out_specs,
        out_shape=out_shape,
        compiler_params=_cparams(2),
        name="ab_in_proj",
    )(*ins)


def _rope_tables(seq_len):
    t = jnp.arange(seq_len)
    quarter = HEAD_DIM // 4
    inv = ROPE_THETA ** (-jnp.arange(quarter, dtype=F32) / quarter)
    ang_r = (t // GRID_W).astype(F32)[:, None] * inv[None, :]
    ang_c = (t % GRID_W).astype(F32)[:, None] * inv[None, :]
    cos = jnp.concatenate([jnp.cos(ang_r)] * 2 + [jnp.cos(ang_c)] * 2, axis=1)
    sin = jnp.concatenate([-jnp.sin(ang_r), jnp.sin(ang_r), -jnp.sin(ang_c), jnp.sin(ang_c)], axis=1)
    return jnp.tile(cos, (1, LANES // HEAD_DIM)), jnp.tile(sin, (1, LANES // HEAD_DIM))


def _ret_body(lgf_ref, lgb_ref, rq_ref, rk_ref, rv_ref, sg_ref, gn_ref, sf_ref, sb_ref,
              y_ref, ff_ref, fb_ref, sbs_ref, *, chunk, n_chunks):
    pair = pl.program_id(1)
    c = chunk
    lo_row = _low_half((1, LANES))
    lgf = jnp.where(lo_row, lgf_ref[2 * pair], lgf_ref[2 * pair + 1])
    lgb = jnp.where(lo_row, lgb_ref[2 * pair], lgb_ref[2 * pair + 1])
    idx = lax.broadcasted_iota(jnp.int32, (c, 1), 0).astype(F32)
    qdec_f = jnp.exp(lgf * (idx + 1.0))
    kdec_f = jnp.exp(lgf * (c - 1.0 - idx))
    cdec_f = jnp.exp(lgf * float(c))
    qdec_b = jnp.exp(lgb * (c - idx))
    kdec_b = jnp.exp(lgb * idx)
    cdec_b = jnp.exp(lgb * float(c))
    diff = (lax.broadcasted_iota(jnp.int32, (c, c), 0)
            - lax.broadcasted_iota(jnp.int32, (c, c), 1)).astype(F32)
    dmat = [jnp.exp(jnp.where(diff >= 0, lgf_ref[2 * pair + h], -lgb_ref[2 * pair + h]) * diff)
            for h in range(2)]
    rr = lax.broadcasted_iota(jnp.int32, (LANES, LANES), 0) < HEAD_DIM
    cc = lax.broadcasted_iota(jnp.int32, (LANES, LANES), 1) < HEAD_DIM
    same_head = rr == cc
    lo = _low_half((c, LANES))

    def load(ref, n):
        return ref[0, pl.ds(pl.multiple_of(n * c, c), c), :]

    def bwd_step(t, state):
        n = n_chunks - 1 - t
        sbs_ref[n] = state
        k = load(rk_ref, n).astype(F32)
        v = load(rv_ref, n).astype(BF16)
        kv = _dot_tn((k * kdec_b).astype(BF16), v)
        return state * cdec_b + jnp.where(same_head, kv, 0.0)

    unroll = int(np.gcd(n_chunks, RET_UNROLL))
    _store_head_states(fb_ref, lax.fori_loop(0, n_chunks, bwd_step, sb_ref[0, 0], unroll=unroll))

    gn = gn_ref[...]

    def fwd_step(n, state):
        q = load(rq_ref, n).astype(BF16)
        k = load(rk_ref, n).astype(BF16)
        v = load(rv_ref, n).astype(BF16)
        qf = q.astype(F32)
        o = (_dot((qf * qdec_f).astype(BF16), state.astype(BF16))
             + _dot((qf * qdec_b).astype(BF16), sbs_ref[n].astype(BF16)))
        zero = jnp.zeros_like(q)
        intra = []
        for h in range(2):
            qh = jnp.where(lo, q, zero) if h == 0 else jnp.where(lo, zero, q)
            s = _dot_nt(qh, k) * dmat[h]
            intra.append(_dot(s.astype(BF16), v))
        o = o + jnp.where(lo, intra[0], intra[1])
        def head_mean(a):
            m0 = jnp.sum(jnp.where(lo, a, 0.0), axis=-1, keepdims=True)
            m1 = jnp.sum(jnp.where(lo, 0.0, a), axis=-1, keepdims=True)
            return jnp.where(lo, m0, m1) * (1.0 / RET_DV)
        dev = o - head_mean(o)
        var = head_mean(dev * dev)
        y = dev * lax.rsqrt(var + RET_GN_EPS) * gn * load(sg_ref, n).astype(F32)
        y_ref[0, pl.ds(pl.multiple_of(n * c, c), c), :] = y.astype(y_ref.dtype)
        kv = _dot_tn((k.astype(F32) * kdec_f).astype(BF16), v)
        return state * cdec_f + jnp.where(same_head, kv, 0.0)

    _store_head_states(ff_ref, lax.fori_loop(0, n_chunks, fwd_step, sf_ref[0, 0], unroll=unroll))


def _store_head_states(ref, state):
    ref[0, 0] = state[:RET_DK, :RET_DV]
    ref[0, 1] = state[RET_DK:, RET_DV:]


def _pair_states(s):
    b, nh, dk, dv = s.shape
    s = s.reshape(b, nh // 2, 2, dk, dv)
    z = jnp.zeros_like(s[:, :, 0])
    top = jnp.concatenate([s[:, :, 0], z], axis=-1)
    bot = jnp.concatenate([z, s[:, :, 1]], axis=-1)
    return jnp.concatenate([top, bot], axis=-2)


def _retention(rq, rk, rv, sg, gn, lg_f, lg_b, s_f, s_b):
    b, seq_len, _ = rq.shape
    c = RET_CHUNK if seq_len % RET_CHUNK == 0 else seq_len
    nc = seq_len // c
    npair = N_RET // 2
    seq_spec = pl.BlockSpec((1, seq_len, LANES), lambda i, p: (i, 0, p))
    st_spec = pl.BlockSpec((1, 1, LANES, LANES), lambda i, p: (i, p, 0, 0))
    fin_spec = pl.BlockSpec((1, 2, RET_DK, RET_DV), lambda i, p: (i, p, 0, 0))
    smem = pl.BlockSpec(memory_space=pltpu.SMEM)
    st_shape = jax.ShapeDtypeStruct((b, N_RET, RET_DK, RET_DV), F32)
    return pl.pallas_call(
        functools.partial(_ret_body, chunk=c, n_chunks=nc),
        grid=(b, npair),
        in_specs=[smem, smem, seq_spec, seq_spec, seq_spec, seq_spec,
                  pl.BlockSpec((1, LANES), lambda i, p: (0, p)), st_spec, st_spec],
        out_specs=[seq_spec, fin_spec, fin_spec],
        out_shape=[jax.ShapeDtypeStruct((b, seq_len, RET_VW), BF16), st_shape, st_shape],
        scratch_shapes=[pltpu.VMEM((nc, LANES, LANES), F32)],
        compiler_params=_cparams(2),
        name="retention",
    )(lg_f, lg_b, rq, rk, rv, sg, gn.reshape(1, RET_VW), _pair_states(s_f), _pair_states(s_b))


def _staged(items, scores, probs, accumulate, ahead=2):
    n = len(items)
    queue = [scores(items[i]) for i in range(min(ahead, n))]
    pending = None
    for i in range(n):
        sc = queue.pop(0)
        if i + ahead < n:
            queue.append(scores(items[i + ahead]))
        ps = probs(items[i], sc)
        if pending is not None:
            accumulate(*pending)
        pending = (items[i], ps)
    accumulate(*pending)


def _row_probs(s_list):
    m = s_list[0].max(axis=-1, keepdims=True)
    for s in s_list[1:]:
        m = jnp.maximum(m, s.max(axis=-1, keepdims=True))
    ps = [jnp.exp2(s - m) for s in s_list]
    den = ps[0].sum(axis=-1, keepdims=True)
    for p in ps[1:]:
        den = den + p.sum(axis=-1, keepdims=True)
    return [p.astype(BF16) for p in ps], 1.0 / den


def _only_head(q, h):
    lo = _low_half(q.shape)
    zero = jnp.zeros_like(q)
    return jnp.where(lo, q, zero) if h == 0 else jnp.where(lo, zero, q)


def _attn_body(q_ref, k_ref, v_ref, o_ref, *, n_groups, kv_shared):
    tq = q_ref.shape[1]
    lo = _low_half((tq, LANES))
    outs = {}

    def lanes(j):
        return slice(j * LANES, (j + 1) * LANES)

    def scores(item):
        j, h = item
        k = k_ref[0] if kv_shared else k_ref[0, :, lanes(j)]
        return _dot_nt(_only_head(q_ref[0, :, lanes(j)].astype(BF16), h), k.astype(BF16))

    def probs(item, s):
        return _row_probs([s])

    def accumulate(item, pr):
        j, h = item
        ps, inv_den = pr
        v = v_ref[0] if kv_shared else v_ref[0, :, lanes(j)]
        outs[h] = _dot(ps[0], v.astype(BF16)) * inv_den
        if h == 1:
            o_ref[0, :, lanes(j)] = jnp.where(lo, outs[0], outs[1]).astype(o_ref.dtype)

    _staged([(j, h) for j in range(n_groups) for h in range(2)], scores, probs, accumulate)


def _col_max(x, slab=64):
    n, t = x.shape
    if n > slab and n % slab == 0:
        x = x.reshape(n // slab, slab, t).max(axis=0)
    return x.max(axis=0, keepdims=True)


def _gqa_t_body(q_ref, kn_ref, kc_ref, vtn_ref, vtc_ref, o_ref, *, n_groups, chunk):
    tq = q_ref.shape[1]
    top = lax.broadcasted_iota(jnp.int32, (LANES, tq), 0) < HEAD_DIM
    ln, lc = kn_ref.shape[1], kc_ref.shape[1]
    chunks = [(kn_ref, vtn_ref, c * chunk, chunk) for c in range(ln // chunk)] + [(kc_ref, vtc_ref, 0, lc)]
    items = [(j, h, c) for j in range(n_groups) for h in range(2) for c in range(len(chunks))]

    def scores(item):
        j, h, c = item
        k_ref, _, off, size = chunks[c]
        qh = _only_head(q_ref[0, :, j * LANES:(j + 1) * LANES], h)
        return _dot_nt(k_ref[0, off:off + size, :], qh)

    outs = {}
    state = {"m": None, "acc": None}

    def probs(item, st):
        c = item[2]
        mc = _col_max(st)
        m_new = mc if c == 0 else jnp.maximum(state["m"], mc)
        alpha = None if c == 0 else jnp.exp2(state["m"] - m_new)
        state["m"] = m_new
        return jnp.exp2(st - m_new).astype(BF16), alpha

    def accumulate(item, ps):
        j, h, c = item
        p, alpha = ps
        _, vt_ref, off, size = chunks[c]
        o = _dot(vt_ref[0, h, :, off:off + size], p)
        acc = o if c == 0 else state["acc"] * alpha + o
        state["acc"] = acc
        if c == len(chunks) - 1:
            den = acc[HEAD_DIM:HEAD_DIM + 1] if h == 0 else acc[0:1]
            outs[h] = acc * (1.0 / den)
            if h == 1:
                sl = slice(j * LANES, (j + 1) * LANES)
                o_ref[0, :, sl] = jnp.where(top, outs[0], outs[1]).T.astype(o_ref.dtype)

    _staged(items, scores, probs, accumulate)


def _vt_with_ones(v):
    vt = jnp.swapaxes(v, 1, 2)
    top = (np.arange(KV_W) < HEAD_DIM)[None, :, None]
    one = jnp.ones_like(vt)
    return jnp.stack([jnp.where(top, vt, one), jnp.where(top, one, vt)], axis=1)


def _gqa_attention_t(q, k_new, v_new, k_ctx, v_ctx, tq):
    b, lq, wq = q.shape
    ln, lc = k_new.shape[1], k_ctx.shape[1]
    chunk = GQA_KEY_CHUNK if ln % GQA_KEY_CHUNK == 0 else ln
    whole3 = lambda i, t: (i, 0, 0)
    whole4 = lambda i, t: (i, 0, 0, 0)
    return pl.pallas_call(
        functools.partial(_gqa_t_body, n_groups=wq // LANES, chunk=chunk),
        grid=(b, lq // tq),
        in_specs=[pl.BlockSpec((1, tq, wq), lambda i, t: (i, t, 0)),
                  pl.BlockSpec((1, ln, KV_W), whole3), pl.BlockSpec((1, lc, KV_W), whole3),
                  pl.BlockSpec((1, 2, KV_W, ln), whole4), pl.BlockSpec((1, 2, KV_W, lc), whole4)],
        out_specs=pl.BlockSpec((1, tq, wq), lambda i, t: (i, t, 0)),
        out_shape=jax.ShapeDtypeStruct((b, lq, wq), BF16),
        compiler_params=_cparams(2),
        name="gqa_attention_t",
    )(q, k_new, k_ctx, _vt_with_ones(v_new), _vt_with_ones(v_ctx))


def _attention(q, k, v, kv_shared, tq):
    b, lq, wq = q.shape
    lk, wk = k.shape[1:]
    return pl.pallas_call(
        functools.partial(_attn_body, n_groups=wq // LANES, kv_shared=kv_shared),
        grid=(b, lq // tq),
        in_specs=[pl.BlockSpec((1, tq, wq), lambda i, t: (i, t, 0)),
                  pl.BlockSpec((1, lk, wk), lambda i, t: (i, 0, 0)),
                  pl.BlockSpec((1, lk, wk), lambda i, t: (i, 0, 0))],
        out_specs=pl.BlockSpec((1, tq, wq), lambda i, t: (i, t, 0)),
        out_shape=jax.ShapeDtypeStruct((b, lq, wq), BF16),
        compiler_params=_cparams(2),
        name="dense_attention",
    )(q, k, v)


def _na_plan(rows):
    g = NA_ROWS_PER_STEP
    kr = min(NA_KR_MAX, rows)
    u = min(g + kr - 1, rows)
    n_steps = rows // g
    bases, vids, variants, keys = [], [], [], {}
    for s in range(n_steps):
        base = int(np.clip(s * g - kr // 2, 0, rows - u))
        r = s * g + np.arange(g)[:, None]
        krow = base + np.arange(u)[None, :]
        r0 = np.clip(r - kr // 2, 0, rows - kr)
        valid = (krow >= r0) & (krow < r0 + kr)
        dr = np.where(valid, krow - r + (NA_KR_MAX - 1), 0)
        key = (dr.tobytes(), valid.tobytes())
        if key not in keys:
            keys[key] = len(variants)
            variants.append((dr, valid))
        bases.append(base)
        vids.append(keys[key])
    return u, bases, vids, variants


def _na_bias_tables(rpb, variants, u):
    g = NA_ROWS_PER_STEP
    nh, n_dr, n_dc = rpb.shape
    cols = np.arange(GRID_W)
    c0 = np.clip(cols - NA_KC // 2, 0, GRID_W - NA_KC)
    kc = np.arange(GRID_W)[None, :]
    cvalid = (kc >= c0[:, None]) & (kc < c0[:, None] + NA_KC)
    dc = kc - cols[:, None] + (NA_KC - 1)
    sel_c = cvalid[None] & (dc[None] == np.arange(n_dc)[:, None, None])
    tables = []
    for dr, rvalid in variants:
        sel_r = rvalid[..., None] & (dr[..., None] == np.arange(n_dr))
        t = jnp.einsum("gud,hde,eck->hgcuk", jnp.asarray(sel_r, F32), rpb.astype(F32),
                       jnp.asarray(sel_c, F32), precision=lax.Precision.HIGHEST)
        ok = rvalid[:, None, :, None] & cvalid[None, :, None, :]
        tables.append(jnp.where(ok[None], t * LOG2E, MASK_VALUE).reshape(nh, g * GRID_W, u * GRID_W))
    return jnp.stack(tables)


def _na_body(base_ref, vid_ref, q_ref, k_ref, v_ref, kc_ref, vc_ref, bias_ref, o_ref, *, n_steps, tq, tk):
    lo = _low_half((tq, LANES))
    kc = kc_ref[0].astype(BF16)
    vc = vc_ref[0].astype(BF16)

    unroll = int(np.gcd(n_steps, NA_UNROLL))

    def trip(t, carry):
        items = [(u, h) for u in range(unroll) for h in range(2)]
        steps = [t * unroll + u for u in range(unroll)]
        outs = {}

        def q_rows(u):
            return pl.ds(pl.multiple_of(steps[u] * tq, tq), tq)

        def k_rows(u):
            return pl.ds(pl.multiple_of(base_ref[steps[u]] * GRID_W, GRID_W), tk)

        def scores(item):
            u, h = item
            qh = _only_head(q_ref[0, q_rows(u), :], h)
            s_win = _dot_nt(qh, k_ref[0, k_rows(u), :]) + bias_ref[vid_ref[steps[u]], h]
            return s_win, _dot_nt(qh, kc)

        def probs(item, sc):
            return _row_probs(list(sc))

        def accumulate(item, pr):
            u, h = item
            ps, inv_den = pr
            outs[h] = (_dot(ps[0], v_ref[0, k_rows(u), :]) + _dot(ps[1], vc)) * inv_den
            if h == 1:
                o_ref[0, q_rows(u), :] = jnp.where(lo, outs[0], outs[1]).astype(o_ref.dtype)

        _staged(items, scores, probs, accumulate)
        return carry

    lax.fori_loop(0, n_steps // unroll, trip, 0)


def _na_attention(q, k, v, k_ctx, v_ctx, rpb):
    b, seq_len, w = q.shape
    rows = seq_len // GRID_W
    lc = k_ctx.shape[1]
    u, bases, vids, variants = _na_plan(rows)
    bias = _na_bias_tables(rpb, variants, u)
    nv = bias.shape[0]
    tq = NA_ROWS_PER_STEP * GRID_W
    tk = u * GRID_W
    n_steps = rows // NA_ROWS_PER_STEP
    smem = pl.BlockSpec(memory_space=pltpu.SMEM)
    seq_spec = pl.BlockSpec((1, seq_len, LANES), lambda p, i: (i, 0, p))
    ctx_spec = pl.BlockSpec((1, lc, LANES), lambda p, i: (i, 0, p))
    return pl.pallas_call(
        functools.partial(_na_body, n_steps=n_steps, tq=tq, tk=tk),
        grid=(w // LANES, b),
        in_specs=[smem, smem, seq_spec, seq_spec, seq_spec, ctx_spec, ctx_spec,
                  pl.BlockSpec((nv, 2, tq, tk), lambda p, i: (0, p, 0, 0))],
        out_specs=seq_spec,
        out_shape=jax.ShapeDtypeStruct((b, seq_len, w), BF16),
        compiler_params=_cparams(2),
        name="neighbourhood_attention",
    )(jnp.asarray(bases, jnp.int32), jnp.asarray(vids, jnp.int32), q, k, v, k_ctx, v_ctx, bias)


def _na_in_body(x_ref, mod_ref, gpre_ref, w_ref, q_o, k_o, v_o, *aux_o):
    h = _modulated(x_ref[0], mod_ref[0], 1, gpre_ref[...]).astype(BF16)
    p = _dot(h, w_ref[...])
    q_o[0] = (p[:, :NA_W] * SCORE_SCALE).astype(q_o.dtype)
    k = p[:, NA_W:2 * NA_W]
    v = p[:, 2 * NA_W:]
    k_o[0] = k.astype(k_o.dtype)
    v_o[0] = v.astype(v_o.dtype)
    if aux_o:
        k5_o, v5_o = aux_o
        k5_o[...] = k.reshape(k5_o.shape)
        v5_o[...] = v.reshape(v5_o.shape)


def _na_in(x, mod, g_pre, w, aux):
    b, seq_len, d = x.shape
    tm = _token_tile(seq_len)
    out_specs = [_tok_spec(tm, NA_W)] * 3
    out_shape = [jax.ShapeDtypeStruct((b, seq_len, NA_W), BF16)] * 3
    if aux:
        assert b == 1
        out_specs += [pl.BlockSpec((tm, N_NA, HEAD_DIM), lambda b, i: (i, 0, 0))] * 2
        out_shape += [jax.ShapeDtypeStruct((seq_len, N_NA, HEAD_DIM), F32)] * 2
    return pl.pallas_call(
        _na_in_body,
        grid=(b, seq_len // tm),
        in_specs=[_tok_spec(tm, d), _mod_spec(mod), _const_spec((1, d)), _const_spec(w.shape)],
        out_specs=out_specs,
        out_shape=out_shape,
        compiler_params=_cparams(2),
        name="na_in_proj",
    )(x, mod, g_pre.reshape(1, d), w)


_Q_HEAD_ORDER = [h for j in range(Q_GROUP) for h in range(j, N_Q, Q_GROUP)]


def _permute_q_cols(w_in):
    q0 = 2 * RET_W + 2 * RET_VW
    heads = [w_in[:, q0 + h * HEAD_DIM:q0 + (h + 1) * HEAD_DIM] for h in _Q_HEAD_ORDER]
    return jnp.concatenate([w_in[:, :q0]] + heads + [w_in[:, q0 + GQA_W:]], axis=1)


def _permute_att_rows(w_att):
    return jnp.concatenate([w_att[h * HEAD_DIM:(h + 1) * HEAD_DIM] for h in _Q_HEAD_ORDER], axis=0)


def _trunk(x, mods, ctx_layers, wts):
    b, seq_len, d = x.shape
    is_ctx = ctx_layers is None
    depth = len(mods)
    flat = (lambda a: a.reshape(1, b * seq_len, a.shape[-1])) if is_ctx else (lambda a: a)
    unflat = (lambda a: a.reshape(b, seq_len, a.shape[-1])) if is_ctx else (lambda a: a)
    aux = []
    x = flat(x)
    for l in range(depth):
        mod = mods[l]
        i = l // 2
        x = _ffn(x, mod, 0, wts["norm_pre"][l, 0], wts["norm_post"][l, 0],
                 wts["ffn_w_in"][l][0], wts["ffn_w_out"][l][0])
        if l % 2 == 0:
            rope_tables = None if is_ctx else _rope_tables(seq_len)
            proj = _ab_in(x, mod, wts["norm_pre"][l, 1], wts["ab_w_in"][i], wts["gqa_q_norm"][i],
                          wts["gqa_k_norm"][i], rope_tables, is_ctx)
            rq, rk, rv, sg, gq, gk, gv = [unflat(a) for a in proj[:7]]
            if is_ctx:
                s_f = jnp.zeros((b, N_RET, RET_DK, RET_DV), F32)
                s_b = s_f
            else:
                ck, cv, s_f, s_b = ctx_layers[l]
                lc = ck.shape[1]
            y_ret, fin_f, fin_b = _retention(rq, rk, rv, sg, wts["ret_gn"][i], wts["lg_f"][i],
                                             wts["lg_b"][i], s_f.astype(F32), s_b.astype(F32))
            if is_ctx:
                y_att = _attention(gq, gk, gv, True, seq_len)
            else:
                y_att = _gqa_attention_t(gq, gk, gv, ck.reshape(b, lc, KV_W).astype(BF16),
                                         cv.reshape(b, lc, KV_W).astype(BF16), GQA_TQ)
            mixer = (wts["norm_post"][l, 1], [flat(y_ret), flat(y_att)],
                     [wts["ab_w_out_ret"][i], wts["ab_w_out_att"][i]])
            aux.append((proj[7], proj[8], fin_f, fin_b) if is_ctx else None)
        else:
            proj = _na_in(x, mod, wts["norm_pre"][l, 1], wts["na_w_qkv"][i], is_ctx)
            q, k, v = [unflat(a) for a in proj[:3]]
            if is_ctx:
                y = _attention(q, k, v, False, seq_len)
                aux.append((proj[3], proj[4]))
            else:
                ck, cv = ctx_layers[l]
                lc = ck.shape[1]
                y = _na_attention(q, k, v, ck.reshape(b, lc, NA_W).astype(BF16),
                                  cv.reshape(b, lc, NA_W).astype(BF16), wts["na_rpb"][i])
                aux.append(None)
            mixer = (wts["norm_post"][l, 1], [flat(y)], [wts["na_w_out"][i]])
        x = _ffn(x, mod, 2, wts["norm_pre"][l, 2], wts["norm_post"][l, 2],
                 wts["ffn_w_in"][l][1], wts["ffn_w_out"][l][1], mixer=mixer)
    return unflat(x), aux


def kernel(x_prompt, x_sample, cache_gqa_k, cache_gqa_v, state_ret_fwd, state_ret_bwd, cache_na_k,
           cache_na_v, c, c_ctx, mod_w, mod_b, norm_pre, norm_post, ffn_w_in, ffn_w_out, ab_w_in,
           ab_w_out, ret_decay_fwd, ret_decay_bwd, ret_gn, gqa_q_norm, gqa_k_norm, na_w_qkv, na_w_out,
           na_rpb):
    depth = mod_w.shape[0]
    d = x_prompt.shape[-1]
    n_dec = c.shape[0]
    batch, seq = x_prompt.shape[:2]

    n_cond = n_dec + 1
    pad = (-n_cond) % 8
    cond = jnp.concatenate([c, c_ctx[None, :], jnp.zeros((pad, d), F32)], axis=0)
    mod_all = _modulation(cond, mod_w, mod_b).reshape(depth, n_cond + pad, 3 * N_SUB, d)
    mods_sample = [mod_all[l, :n_dec] for l in range(depth)]
    mods_prompt = [mod_all[l, n_dec:n_dec + 1] for l in range(depth)]

    n_even = ab_w_in.shape[0]
    wts = {
        "norm_pre": norm_pre, "norm_post": norm_post,
        "ffn_w_in": [[ffn_w_in[l, s].astype(BF16) for s in range(2)] for l in range(depth)],
        "ffn_w_out": [[ffn_w_out[l, s].astype(BF16) for s in range(2)] for l in range(depth)],
        "ab_w_in": [_permute_q_cols(ab_w_in[i]).astype(BF16) for i in range(n_even)],
        "ab_w_out_ret": [ab_w_out[i, :RET_VW].astype(BF16) for i in range(n_even)],
        "ab_w_out_att": [_permute_att_rows(ab_w_out[i, RET_VW:]).astype(BF16) for i in range(n_even)],
        "lg_f": jax.nn.log_sigmoid(ret_decay_fwd.astype(F32)),
        "lg_b": jax.nn.log_sigmoid(ret_decay_bwd.astype(F32)),
        "ret_gn": ret_gn, "gqa_q_norm": gqa_q_norm, "gqa_k_norm": gqa_k_norm,
        "na_w_qkv": [na_w_qkv[i].astype(BF16) for i in range(na_w_qkv.shape[0])],
        "na_w_out": [na_w_out[i].astype(BF16) for i in range(na_w_out.shape[0])],
        "na_rpb": na_rpb,
    }

    y_prompt, ctx_aux = _trunk(x_prompt, mods_prompt, None, wts)

    ctx_layers = []
    for l in range(depth):
        i = l // 2
        if l % 2 == 0:
            ctx_layers.append((cache_gqa_k[:, i], cache_gqa_v[:, i], state_ret_fwd[:, i], state_ret_bwd[:, i]))
        else:
            ctx_layers.append((cache_na_k[:, i], cache_na_v[:, i]))
    y_sample, _ = _trunk(x_sample, mods_sample, ctx_layers, wts)

    def stack(idx, layers, tail):
        return jnp.stack([ctx_aux[l][idx].reshape((batch,) + tail) for l in layers], axis=1)

    even = range(0, depth, 2)
    odd = range(1, depth, 2)
    new_gqa_k = stack(0, even, (seq, N_KV, HEAD_DIM))
    new_gqa_v = stack(1, even, (seq, N_KV, HEAD_DIM))
    new_ret_fwd = stack(2, even, (N_RET, RET_DK, RET_DV))
    new_ret_bwd = stack(3, even, (N_RET, RET_DK, RET_DV))
    new_na_k = stack(0, odd, (seq, N_NA, HEAD_DIM))
    new_na_v = stack(1, odd, (seq, N_NA, HEAD_DIM))
    return (y_prompt, y_sample, new_gqa_k, new_gqa_v, new_ret_fwd, new_ret_bwd, new_na_k, new_na_v)
```

```python
import functools

import numpy as np
import jax
import jax.numpy as jnp
from jax import lax
from jax.experimental import pallas as pl
from jax.experimental.pallas import tpu as pltpu

F32 = jnp.float32
BF16 = jnp.bfloat16

GRID_W = 64
N_SUB = 3
HEAD_DIM = 64
N_RET = 8
RET_DK = 64
RET_DV = 64
RET_GN_EPS = 1e-5
N_Q = 8
N_KV = 2
Q_GROUP = N_Q // N_KV
ROPE_THETA = 10000.0
N_NA = 16
NA_KR_MAX = 8
NA_KC = 16
EPS = 1e-6
RET_W = N_RET * RET_DK
RET_VW = N_RET * RET_DV
GQA_W = N_Q * HEAD_DIM
KV_W = N_KV * HEAD_DIM
NA_W = N_NA * HEAD_DIM

LANES = 128
VMEM_LIMIT = 56 * 1024 * 1024
RET_CHUNK = 256
RET_UNROLL = 4
NA_ROWS_PER_STEP = 4
FFN_SUBTILES = 2
NA_UNROLL = 8
GQA_TQ = 256
GQA_KEY_CHUNK = 1024
MASK_VALUE = -1e30
LOG2E = 1.4426950408889634
SCORE_SCALE = HEAD_DIM ** -0.5 * LOG2E


def _cparams(n_grid):
    return pltpu.CompilerParams(dimension_semantics=("parallel",) * n_grid,
                                vmem_limit_bytes=VMEM_LIMIT)


def _const_spec(shape):
    nd = len(shape)
    return pl.BlockSpec(shape, lambda *_: (0,) * nd, pipeline_mode=pl.Buffered(1))


def _silu(x):
    return x * jax.nn.sigmoid(x)


def _rms(x, g):
    ms = jnp.mean(x * x, axis=-1, keepdims=True)
    return x * lax.rsqrt(ms + EPS) * g


def _modulated(x, m, sub, g_pre):
    shift = m[3 * sub:3 * sub + 1]
    scale = m[3 * sub + 1:3 * sub + 2]
    return _rms(x, g_pre) * (1.0 + scale) + shift


def _dot(a, b):
    return jnp.dot(a, b, preferred_element_type=F32)


def _dot_nt(a, b):
    return lax.dot_general(a, b, (((1,), (1,)), ((), ())), preferred_element_type=F32)


def _dot_tn(a, b):
    return lax.dot_general(a, b, (((0,), (0,)), ((), ())), preferred_element_type=F32)


def _low_half(shape):
    return lax.broadcasted_iota(jnp.int32, shape, len(shape) - 1) < HEAD_DIM


def _mod_body(c_ref, w_ref, b_ref, o_ref):
    a = _silu(c_ref[...]).astype(BF16)
    o_ref[0] = _dot(a, w_ref[0].astype(BF16)) + b_ref[0]


def _modulation(cond, mod_w, mod_b):
    depth, d, n = mod_w.shape
    rows = cond.shape[0]
    tn = 1152 if n % 1152 == 0 else n
    return pl.pallas_call(
        _mod_body,
        grid=(depth, n // tn),
        in_specs=[pl.BlockSpec((rows, d), lambda l, j: (0, 0)),
                  pl.BlockSpec((1, d, tn), lambda l, j: (l, 0, j)),
                  pl.BlockSpec((1, 1, tn), lambda l, j: (l, 0, j))],
        out_specs=pl.BlockSpec((1, rows, tn), lambda l, j: (l, 0, j)),
        out_shape=jax.ShapeDtypeStruct((depth, rows, n), F32),
        compiler_params=_cparams(2),
        name="modulation",
    )(cond, mod_w, mod_b.reshape(depth, 1, n))


def _tok_spec(tm, width):
    return pl.BlockSpec((1, tm, width), lambda b, i: (b, i, 0))


def _mod_spec(mod):
    nm, rows, d = mod.shape
    if nm == 1:
        return pl.BlockSpec((1, rows, d), lambda b, i: (0, 0, 0))
    return pl.BlockSpec((1, rows, d), lambda b, i: (b, 0, 0))


def _token_tile(seq_len):
    return 512 if seq_len % 512 == 0 else seq_len


def _ffn_body(*refs, sub, dff, n_seg, n_sub):
    x_ref, mod_ref, gpre_ref, gpost_ref, win_ref, wout_ref = refs[:6]
    rest = refs[6:]
    if n_seg:
        gmix_ref = rest[0]
        seg_refs = rest[1:1 + n_seg]
        wmix_refs = rest[1 + n_seg:1 + 2 * n_seg]
    o_ref = refs[-1]
    m = mod_ref[0]
    tm = x_ref.shape[1]
    ts = tm // n_sub
    rows = [slice(t * ts, (t + 1) * ts) for t in range(n_sub)]

    def residual(t):
        x = x_ref[0, rows[t], :]
        if n_seg:
            y = None
            for s_ref, w_ref in zip(seg_refs, wmix_refs):
                part = _dot(s_ref[0, rows[t], :].astype(BF16), w_ref[...])
                y = part if y is None else y + part
            x = x + m[5:6] * _rms(y, gmix_ref[...])
        return x

    def hidden(t, x):
        h = _modulated(x, m, sub, gpre_ref[...]).astype(BF16)
        gu = _dot(h, win_ref[...])
        return (_silu(gu[:, :dff]) * gu[:, dff:]).astype(BF16)

    def finish(t, x, a):
        y = _dot(a, wout_ref[...])
        gate = m[3 * sub + 2:3 * sub + 3]
        o_ref[0, rows[t], :] = x + (0.5 * gate) * _rms(y, gpost_ref[...])

    xs = [residual(t) for t in range(n_sub)]
    acts = [hidden(t, xs[t]) for t in range(n_sub)]
    for t in range(n_sub):
        finish(t, xs[t], acts[t])


def _ffn(x, mod, sub, g_pre, g_post, w_in, w_out, mixer=None):
    b, seq_len, d = x.shape
    dff = w_out.shape[0]
    tm = _token_tile(seq_len)
    ins = [x, mod, g_pre.reshape(1, d), g_post.reshape(1, d), w_in, w_out]
    in_specs = [_tok_spec(tm, d), _mod_spec(mod), _const_spec((1, d)), _const_spec((1, d)),
                _const_spec(w_in.shape), _const_spec(w_out.shape)]
    n_seg = 0
    if mixer is not None:
        g_mix, segs, ws = mixer
        n_seg = len(segs)
        ins += [g_mix.reshape(1, d)] + list(segs) + list(ws)
        in_specs += ([_const_spec((1, d))] + [_tok_spec(tm, s.shape[-1]) for s in segs]
                     + [_const_spec(w.shape) for w in ws])
    n_sub = FFN_SUBTILES if tm % (8 * FFN_SUBTILES) == 0 else 1
    return pl.pallas_call(
        functools.partial(_ffn_body, sub=sub, dff=dff, n_seg=n_seg, n_sub=n_sub),
        grid=(b, seq_len // tm),
        in_specs=in_specs,
        out_specs=_tok_spec(tm, d),
        out_shape=jax.ShapeDtypeStruct(x.shape, F32),
        compiler_params=_cparams(2),
        name="ffn_half" if mixer is None else "mixer_out_ffn_half",
    )(*ins)


def _head_rms(a, g, bd):
    a2 = a * a
    hi = a2.astype(BF16)
    lo = (a2 - hi.astype(F32)).astype(BF16)
    ss = _dot(hi, bd) + _dot(lo, bd)
    return a * lax.rsqrt(ss * (1.0 / HEAD_DIM) + EPS) * g


def _rope(a, cos, sin):
    quarter = HEAD_DIM // 4
    outs = []
    for j in range(a.shape[1] // LANES):
        aj = a[:, j * LANES:(j + 1) * LANES]
        up = pltpu.roll(aj, LANES - quarter, axis=1)
        dn = pltpu.roll(aj, quarter, axis=1)
        lane = lax.broadcasted_iota(jnp.int32, aj.shape, 1)
        partner = jnp.where((lane % (2 * quarter)) < quarter, up, dn)
        outs.append(aj * cos + partner * sin)
    return outs[0] if len(outs) == 1 else jnp.concatenate(outs, axis=1)


def _ab_in_body(*refs, rope, aux):
    x_ref, mod_ref, gpre_ref, w_ref, qn_ref, kn_ref, bd_ref = refs[:7]
    refs = refs[7:]
    if rope:
        cos_ref, sin_ref = refs[:2]
        refs = refs[2:]
    rq_o, rk_o, rv_o, sg_o, gq_o, gk_o, gv_o = refs[:7]
    h = _modulated(x_ref[0], mod_ref[0], 1, gpre_ref[...]).astype(BF16)
    p = _dot(h, w_ref[...])
    o = 0
    rq_o[0] = p[:, o:o + RET_W].astype(rq_o.dtype)
    o += RET_W
    rk_o[0] = (p[:, o:o + RET_W] * (RET_DK ** -0.5)).astype(rk_o.dtype)
    o += RET_W
    rv_o[0] = p[:, o:o + RET_VW].astype(rv_o.dtype)
    o += RET_VW
    sg_o[0] = _silu(p[:, o:o + RET_VW]).astype(sg_o.dtype)
    o += RET_VW
    gq = p[:, o:o + GQA_W]
    o += GQA_W
    gk = p[:, o:o + KV_W]
    o += KV_W
    gv = p[:, o:o + KV_W]
    bd = bd_ref[...]
    qh = _head_rms(gq, qn_ref[...], bd)
    kh = _head_rms(gk, kn_ref[...], bd[:KV_W, :KV_W])
    if rope:
        qh = _rope(qh, cos_ref[...], sin_ref[...])
        kh = _rope(kh, cos_ref[...], sin_ref[...])
    gq_o[0] = (qh * SCORE_SCALE).astype(gq_o.dtype)
    gk_o[0] = kh.astype(gk_o.dtype)
    gv_o[0] = gv.astype(gv_o.dtype)
    if aux:
        k5_o, v5_o = refs[7:]
        k5_o[...] = kh.reshape(k5_o.shape)
        v5_o[...] = gv.reshape(v5_o.shape)


def _ab_in(x, mod, g_pre, w, qn, kn, rope_tables, aux):
    b, seq_len, d = x.shape
    tm = _token_tile(seq_len)
    rope = rope_tables is not None
    bd = jnp.asarray(np.kron(np.eye(N_Q), np.ones((HEAD_DIM, HEAD_DIM))), BF16)
    ins = [x, mod, g_pre.reshape(1, d), w, jnp.tile(qn, N_Q).reshape(1, GQA_W),
           jnp.tile(kn, N_KV).reshape(1, KV_W), bd]
    in_specs = [_tok_spec(tm, d), _mod_spec(mod), _const_spec((1, d)), _const_spec(w.shape),
                _const_spec((1, GQA_W)), _const_spec((1, KV_W)), _const_spec(bd.shape)]
    if rope:
        ins += list(rope_tables)
        in_specs += [pl.BlockSpec((tm, LANES), lambda b, i: (i, 0))] * 2
    widths = (RET_W, RET_W, RET_VW, RET_VW, GQA_W, KV_W, KV_W)
    out_specs = [_tok_spec(tm, wd) for wd in widths]
    out_shape = [jax.ShapeDtypeStruct((b, seq_len, wd), BF16) for wd in widths]
    if aux:
        assert b == 1
        out_specs += [pl.BlockSpec((tm, N_KV, HEAD_DIM), lambda b, i: (i, 0, 0))] * 2
        out_shape += [jax.ShapeDtypeStruct((seq_len, N_KV, HEAD_DIM), F32)] * 2
    return pl.pallas_call(
        functools.partial(_ab_in_body, rope=rope, aux=aux),
        grid=(b, seq_len // tm),
        in_specs=in_specs,
        out_specs=out_specs,
        out_shape=out_shape,
        compiler_params=_cparams(2),
        name="ab_in_proj",
    )(*ins)


def _rope_tables(seq_len):
    t = jnp.arange(seq_len)
    quarter = HEAD_DIM // 4
    inv = ROPE_THETA ** (-jnp.arange(quarter, dtype=F32) / quarter)
    ang_r = (t // GRID_W).astype(F32)[:, None] * inv[None, :]
    ang_c = (t % GRID_W).astype(F32)[:, None] * inv[None, :]
    cos = jnp.concatenate([jnp.cos(ang_r)] * 2 + [jnp.cos(ang_c)] * 2, axis=1)
    sin = jnp.concatenate([-jnp.sin(ang_r), jnp.sin(ang_r), -jnp.sin(ang_c), jnp.sin(ang_c)], axis=1)
    return jnp.tile(cos, (1, LANES // HEAD_DIM)), jnp.tile(sin, (1, LANES // HEAD_DIM))


def _ret_body(lgf_ref, lgb_ref, rq_ref, rk_ref, rv_ref, sg_ref, gn_ref, sf_ref, sb_ref,
              y_ref, ff_ref, fb_ref, sbs_ref, *, chunk, n_chunks):
    pair = pl.program_id(1)
    c = chunk
    lo_row = _low_half((1, LANES))
    lgf = jnp.where(lo_row, lgf_ref[2 * pair], lgf_ref[2 * pair + 1])
    lgb = jnp.where(lo_row, lgb_ref[2 * pair], lgb_ref[2 * pair + 1])
    idx = lax.broadcasted_iota(jnp.int32, (c, 1), 0).astype(F32)
    qdec_f = jnp.exp(lgf * (idx + 1.0))
    kdec_f = jnp.exp(lgf * (c - 1.0 - idx))
    cdec_f = jnp.exp(lgf * float(c))
    qdec_b = jnp.exp(lgb * (c - idx))
    kdec_b = jnp.exp(lgb * idx)
    cdec_b = jnp.exp(lgb * float(c))
    diff = (lax.broadcasted_iota(jnp.int32, (c, c), 0)
            - lax.broadcasted_iota(jnp.int32, (c, c), 1)).astype(F32)
    dmat = [jnp.exp(jnp.where(diff >= 0, lgf_ref[2 * pair + h], -lgb_ref[2 * pair + h]) * diff)
            for h in range(2)]
    rr = lax.broadcasted_iota(jnp.int32, (LANES, LANES), 0) < HEAD_DIM
    cc = lax.broadcasted_iota(jnp.int32, (LANES, LANES), 1) < HEAD_DIM
    same_head = rr == cc
    lo = _low_half((c, LANES))

    def load(ref, n):
        return ref[0, pl.ds(pl.multiple_of(n * c, c), c), :]

    def bwd_step(t, state):
        n = n_chunks - 1 - t
        sbs_ref[n] = state
        k = load(rk_ref, n).astype(F32)
        v = load(rv_ref, n).astype(BF16)
        kv = _dot_tn((k * kdec_b).astype(BF16), v)
        return state * cdec_b + jnp.where(same_head, kv, 0.0)

    unroll = int(np.gcd(n_chunks, RET_UNROLL))
    _store_head_states(fb_ref, lax.fori_loop(0, n_chunks, bwd_step, sb_ref[0, 0], unroll=unroll))

    gn = gn_ref[...]

    def fwd_step(n, state):
        q = load(rq_ref, n).astype(BF16)
        k = load(rk_ref, n).astype(BF16)
        v = load(rv_ref, n).astype(BF16)
        qf = q.astype(F32)
        o = (_dot((qf * qdec_f).astype(BF16), state.astype(BF16))
             + _dot((qf * qdec_b).astype(BF16), sbs_ref[n].astype(BF16)))
        zero = jnp.zeros_like(q)
        intra = []
        for h in range(2):
            qh = jnp.where(lo, q, zero) if h == 0 else jnp.where(lo, zero, q)
            s = _dot_nt(qh, k) * dmat[h]
            intra.append(_dot(s.astype(BF16), v))
        o = o + jnp.where(lo, intra[0], intra[1])
        def head_mean(a):
            m0 = jnp.sum(jnp.where(lo, a, 0.0), axis=-1, keepdims=True)
            m1 = jnp.sum(jnp.where(lo, 0.0, a), axis=-1, keepdims=True)
            return jnp.where(lo, m0, m1) * (1.0 / RET_DV)
        dev = o - head_mean(o)
        var = head_mean(dev * dev)
        y = dev * lax.rsqrt(var + RET_GN_EPS) * gn * load(sg_ref, n).astype(F32)
        y_ref[0, pl.ds(pl.multiple_of(n * c, c), c), :] = y.astype(y_ref.dtype)
        kv = _dot_tn((k.astype(F32) * kdec_f).astype(BF16), v)
        return state * cdec_f + jnp.where(same_head, kv, 0.0)

    _store_head_states(ff_ref, lax.fori_loop(0, n_chunks, fwd_step, sf_ref[0, 0], unroll=unroll))


def _store_head_states(ref, state):
    ref[0, 0] = state[:RET_DK, :RET_DV]
    ref[0, 1] = state[RET_DK:, RET_DV:]


def _pair_states(s):
    b, nh, dk, dv = s.shape
    s = s.reshape(b, nh // 2, 2, dk, dv)
    z = jnp.zeros_like(s[:, :, 0])
    top = jnp.concatenate([s[:, :, 0], z], axis=-1)
    bot = jnp.concatenate([z, s[:, :, 1]], axis=-1)
    return jnp.concatenate([top, bot], axis=-2)


def _retention(rq, rk, rv, sg, gn, lg_f, lg_b, s_f, s_b):
    b, seq_len, _ = rq.shape
    c = RET_CHUNK if seq_len % RET_CHUNK == 0 else seq_len
    nc = seq_len // c
    npair = N_RET // 2
    seq_spec = pl.BlockSpec((1, seq_len, LANES), lambda i, p: (i, 0, p))
    st_spec = pl.BlockSpec((1, 1, LANES, LANES), lambda i, p: (i, p, 0, 0))
    fin_spec = pl.BlockSpec((1, 2, RET_DK, RET_DV), lambda i, p: (i, p, 0, 0))
    smem = pl.BlockSpec(memory_space=pltpu.SMEM)
    st_shape = jax.ShapeDtypeStruct((b, N_RET, RET_DK, RET_DV), F32)
    return pl.pallas_call(
        functools.partial(_ret_body, chunk=c, n_chunks=nc),
        grid=(b, npair),
        in_specs=[smem, smem, seq_spec, seq_spec, seq_spec, seq_spec,
                  pl.BlockSpec((1, LANES), lambda i, p: (0, p)), st_spec, st_spec],
        out_specs=[seq_spec, fin_spec, fin_spec],
        out_shape=[jax.ShapeDtypeStruct((b, seq_len, RET_VW), BF16), st_shape, st_shape],
        scratch_shapes=[pltpu.VMEM((nc, LANES, LANES), F32)],
        compiler_params=_cparams(2),
        name="retention",
    )(lg_f, lg_b, rq, rk, rv, sg, gn.reshape(1, RET_VW), _pair_states(s_f), _pair_states(s_b))


def _staged(items, scores, probs, accumulate, ahead=2):
    n = len(items)
    queue = [scores(items[i]) for i in range(min(ahead, n))]
    pending = None
    for i in range(n):
        sc = queue.pop(0)
        if i + ahead < n:
            queue.append(scores(items[i + ahead]))
        ps = probs(items[i], sc)
        if pending is not None:
            accumulate(*pending)
        pending = (items[i], ps)
    accumulate(*pending)


def _row_probs(s_list):
    m = s_list[0].max(axis=-1, keepdims=True)
    for s in s_list[1:]:
        m = jnp.maximum(m, s.max(axis=-1, keepdims=True))
    ps = [jnp.exp2(s - m) for s in s_list]
    den = ps[0].sum(axis=-1, keepdims=True)
    for p in ps[1:]:
        den = den + p.sum(axis=-1, keepdims=True)
    return [p.astype(BF16) for p in ps], 1.0 / den


def _only_head(q, h):
    lo = _low_half(q.shape)
    zero = jnp.zeros_like(q)
    return jnp.where(lo, q, zero) if h == 0 else jnp.where(lo, zero, q)


def _attn_body(q_ref, k_ref, v_ref, o_ref, *, n_groups, kv_shared):
    tq = q_ref.shape[1]
    lo = _low_half((tq, LANES))
    outs = {}

    def lanes(j):
        return slice(j * LANES, (j + 1) * LANES)

    def scores(item):
        j, h = item
        k = k_ref[0] if kv_shared else k_ref[0, :, lanes(j)]
        return _dot_nt(_only_head(q_ref[0, :, lanes(j)].astype(BF16), h), k.astype(BF16))

    def probs(item, s):
        return _row_probs([s])

    def accumulate(item, pr):
        j, h = item
        ps, inv_den = pr
        v = v_ref[0] if kv_shared else v_ref[0, :, lanes(j)]
        outs[h] = _dot(ps[0], v.astype(BF16)) * inv_den
        if h == 1:
            o_ref[0, :, lanes(j)] = jnp.where(lo, outs[0], outs[1]).astype(o_ref.dtype)

    _staged([(j, h) for j in range(n_groups) for h in range(2)], scores, probs, accumulate)


def _col_max(x, slab=64):
    n, t = x.shape
    if n > slab and n % slab == 0:
        x = x.reshape(n // slab, slab, t).max(axis=0)
    return x.max(axis=0, keepdims=True)


def _gqa_t_body(q_ref, kn_ref, kc_ref, vtn_ref, vtc_ref, o_ref, *, n_groups, chunk):
    tq = q_ref.shape[1]
    top = lax.broadcasted_iota(jnp.int32, (LANES, tq), 0) < HEAD_DIM
    ln, lc = kn_ref.shape[1], kc_ref.shape[1]
    chunks = [(kn_ref, vtn_ref, c * chunk, chunk) for c in range(ln // chunk)] + [(kc_ref, vtc_ref, 0, lc)]
    items = [(j, h, c) for j in range(n_groups) for h in range(2) for c in range(len(chunks))]

    def scores(item):
        j, h, c = item
        k_ref, _, off, size = chunks[c]
        qh = _only_head(q_ref[0, :, j * LANES:(j + 1) * LANES], h)
        return _dot_nt(k_ref[0, off:off + size, :], qh)

    outs = {}
    state = {"m": None, "acc": None}

    def probs(item, st):
        c = item[2]
        mc = _col_max(st)
        m_new = mc if c == 0 else jnp.maximum(state["m"], mc)
        alpha = None if c == 0 else jnp.exp2(state["m"] - m_new)
        state["m"] = m_new
        return jnp.exp2(st - m_new).astype(BF16), alpha

    def accumulate(item, ps):
        j, h, c = item
        p, alpha = ps
        _, vt_ref, off, size = chunks[c]
        o = _dot(vt_ref[0, h, :, off:off + size], p)
        acc = o if c == 0 else state["acc"] * alpha + o
        state["acc"] = acc
        if c == len(chunks) - 1:
            den = acc[HEAD_DIM:HEAD_DIM + 1] if h == 0 else acc[0:1]
            outs[h] = acc * (1.0 / den)
            if h == 1:
                sl = slice(j * LANES, (j + 1) * LANES)
                o_ref[0, :, sl] = jnp.where(top, outs[0], outs[1]).T.astype(o_ref.dtype)

    _staged(items, scores, probs, accumulate)


def _vt_with_ones(v):
    vt = jnp.swapaxes(v, 1, 2)
    top = (np.arange(KV_W) < HEAD_DIM)[None, :, None]
    one = jnp.ones_like(vt)
    return jnp.stack([jnp.where(top, vt, one), jnp.where(top, one, vt)], axis=1)


def _gqa_attention_t(q, k_new, v_new, k_ctx, v_ctx, tq):
    b, lq, wq = q.shape
    ln, lc = k_new.shape[1], k_ctx.shape[1]
    chunk = GQA_KEY_CHUNK if ln % GQA_KEY_CHUNK == 0 else ln
    whole3 = lambda i, t: (i, 0, 0)
    whole4 = lambda i, t: (i, 0, 0, 0)
    return pl.pallas_call(
        functools.partial(_gqa_t_body, n_groups=wq // LANES, chunk=chunk),
        grid=(b, lq // tq),
        in_specs=[pl.BlockSpec((1, tq, wq), lambda i, t: (i, t, 0)),
                  pl.BlockSpec((1, ln, KV_W), whole3), pl.BlockSpec((1, lc, KV_W), whole3),
                  pl.BlockSpec((1, 2, KV_W, ln), whole4), pl.BlockSpec((1, 2, KV_W, lc), whole4)],
        out_specs=pl.BlockSpec((1, tq, wq), lambda i, t: (i, t, 0)),
        out_shape=jax.ShapeDtypeStruct((b, lq, wq), BF16),
        compiler_params=_cparams(2),
        name="gqa_attention_t",
    )(q, k_new, k_ctx, _vt_with_ones(v_new), _vt_with_ones(v_ctx))


def _attention(q, k, v, kv_shared, tq):
    b, lq, wq = q.shape
    lk, wk = k.shape[1:]
    return pl.pallas_call(
        functools.partial(_attn_body, n_groups=wq // LANES, kv_shared=kv_shared),
        grid=(b, lq // tq),
        in_specs=[pl.BlockSpec((1, tq, wq), lambda i, t: (i, t, 0)),
                  pl.BlockSpec((1, lk, wk), lambda i, t: (i, 0, 0)),
                  pl.BlockSpec((1, lk, wk), lambda i, t: (i, 0, 0))],
        out_specs=pl.BlockSpec((1, tq, wq), lambda i, t: (i, t, 0)),
        out_shape=jax.ShapeDtypeStruct((b, lq, wq), BF16),
        compiler_params=_cparams(2),
        name="dense_attention",
    )(q, k, v)


def _na_plan(rows):
    g = NA_ROWS_PER_STEP
    kr = min(NA_KR_MAX, rows)
    u = min(-(-(g + kr - 1) // 2) * 2, rows)
    n_steps = rows // g
    bases, vids, variants, keys = [], [], [], {}
    for s in range(n_steps):
        base = int(np.clip(s * g - kr // 2, 0, rows - u)) // 2 * 2
        r = s * g + np.arange(g)[:, None]
        krow = base + np.arange(u)[None, :]
        r0 = np.clip(r - kr // 2, 0, rows - kr)
        valid = (krow >= r0) & (krow < r0 + kr)
        assert (valid.sum(axis=1) == kr).all(), "key window not covered by the step's row range"
        dr = np.where(valid, krow - r + (NA_KR_MAX - 1), 0)
        key = (dr.tobytes(), valid.tobytes())
        if key not in keys:
            keys[key] = len(variants)
            variants.append((dr, valid))
        bases.append(base)
        vids.append(keys[key])
    return u, bases, vids, variants


def _na_bias_tables(rpb, variants, u):
    g = NA_ROWS_PER_STEP
    nh, n_dr, n_dc = rpb.shape
    cols = np.arange(GRID_W)
    c0 = np.clip(cols - NA_KC // 2, 0, GRID_W - NA_KC)
    kc = np.arange(GRID_W)[None, :]
    cvalid = (kc >= c0[:, None]) & (kc < c0[:, None] + NA_KC)
    dc = kc - cols[:, None] + (NA_KC - 1)
    sel_c = cvalid[None] & (dc[None] == np.arange(n_dc)[:, None, None])
    tables = []
    for dr, rvalid in variants:
        sel_r = rvalid[..., None] & (dr[..., None] == np.arange(n_dr))
        t = jnp.einsum("gud,hde,eck->hukgc", jnp.asarray(sel_r, F32), rpb.astype(F32),
                       jnp.asarray(sel_c, F32), precision=lax.Precision.HIGHEST)
        ok = rvalid.T[:, None, :, None] & cvalid.T[None, :, None, :]
        tables.append(jnp.where(ok[None], t * LOG2E, MASK_VALUE).reshape(nh, u * GRID_W, g * GRID_W))
    return jnp.stack(tables)


def _na_body(base_ref, vid_ref, q_ref, k_ref, vt_ref, kc_ref, vct_ref, bias_ref, o_ref, *, n_steps, tq, tk):
    top = lax.broadcasted_iota(jnp.int32, (LANES, tq), 0) < HEAD_DIM
    kc = kc_ref[0]
    lc = kc.shape[0]
    ones_rows = 8
    vct = jnp.concatenate([vct_ref[0], jnp.ones((ones_rows, lc), BF16)], axis=0)
    ones_win = jnp.ones((ones_rows, tk), BF16)
    unroll = int(np.gcd(n_steps, NA_UNROLL))

    def trip(t, carry):
        items = [(u, h) for u in range(unroll) for h in range(2)]
        steps = [t * unroll + u for u in range(unroll)]
        outs = {}

        def q_rows(u):
            return pl.ds(pl.multiple_of(steps[u] * tq, tq), tq)

        def k_rows(u):
            return pl.ds(pl.multiple_of(base_ref[steps[u]] * GRID_W, LANES), tk)

        def scores(item):
            u, h = item
            qh = _only_head(q_ref[0, q_rows(u), :], h)
            st_win = _dot_nt(k_ref[0, k_rows(u), :], qh) + bias_ref[vid_ref[steps[u]], h]
            return st_win, _dot_nt(kc, qh)

        def probs(item, sc):
            m = jnp.maximum(_col_max(sc[0]), _col_max(sc[1]))
            return [jnp.exp2(s - m).astype(BF16) for s in sc]

        def accumulate(item, ps):
            u, h = item
            vt = jnp.concatenate([vt_ref[0, :, k_rows(u)], ones_win], axis=0)
            acc = _dot(vt, ps[0]) + _dot(vct, ps[1])
            outs[h] = acc[:LANES] * (1.0 / acc[LANES:LANES + 1])
            if h == 1:
                o_ref[0, q_rows(u), :] = jnp.where(top, outs[0], outs[1]).T.astype(o_ref.dtype)

        _staged(items, scores, probs, accumulate)
        return carry

    lax.fori_loop(0, n_steps // unroll, trip, 0)


def _na_attention(q, k, vt, k_ctx, v_ctx, rpb):
    b, seq_len, w = q.shape
    rows = seq_len // GRID_W
    lc = k_ctx.shape[1]
    u, bases, vids, variants = _na_plan(rows)
    bias = _na_bias_tables(rpb, variants, u)
    nv = bias.shape[0]
    tq = NA_ROWS_PER_STEP * GRID_W
    tk = u * GRID_W
    n_steps = rows // NA_ROWS_PER_STEP
    smem = pl.BlockSpec(memory_space=pltpu.SMEM)
    seq_spec = pl.BlockSpec((1, seq_len, LANES), lambda p, i: (i, 0, p))
    return pl.pallas_call(
        functools.partial(_na_body, n_steps=n_steps, tq=tq, tk=tk),
        grid=(w // LANES, b),
        in_specs=[smem, smem, seq_spec, seq_spec,
                  pl.BlockSpec((1, LANES, seq_len), lambda p, i: (i, p, 0)),
                  pl.BlockSpec((1, lc, LANES), lambda p, i: (i, 0, p)),
                  pl.BlockSpec((1, LANES, lc), lambda p, i: (i, p, 0)),
                  pl.BlockSpec((nv, 2, tk, tq), lambda p, i: (0, p, 0, 0))],
        out_specs=seq_spec,
        out_shape=jax.ShapeDtypeStruct((b, seq_len, w), BF16),
        compiler_params=_cparams(2),
        name="neighbourhood_attention",
    )(jnp.asarray(bases, jnp.int32), jnp.asarray(vids, jnp.int32), q, k, vt, k_ctx,
      jnp.swapaxes(v_ctx, 1, 2), bias)


def _na_in_body(x_ref, mod_ref, gpre_ref, w_ref, q_o, k_o, v_o, *aux_o):
    h = _modulated(x_ref[0], mod_ref[0], 1, gpre_ref[...]).astype(BF16)
    p = _dot(h, w_ref[...])
    q_o[0] = (p[:, :NA_W] * SCORE_SCALE).astype(q_o.dtype)
    k = p[:, NA_W:2 * NA_W]
    v = p[:, 2 * NA_W:]
    k_o[0] = k.astype(k_o.dtype)
    v_o[0] = (v if aux_o else v.T).astype(v_o.dtype)
    if aux_o:
        k5_o, v5_o = aux_o
        k5_o[...] = k.reshape(k5_o.shape)
        v5_o[...] = v.reshape(v5_o.shape)


def _na_in(x, mod, g_pre, w, aux):
    b, seq_len, d = x.shape
    tm = _token_tile(seq_len)
    out_specs = [_tok_spec(tm, NA_W)] * 3
    out_shape = [jax.ShapeDtypeStruct((b, seq_len, NA_W), BF16)] * 3
    if aux:
        assert b == 1
        out_specs += [pl.BlockSpec((tm, N_NA, HEAD_DIM), lambda b, i: (i, 0, 0))] * 2
        out_shape += [jax.ShapeDtypeStruct((seq_len, N_NA, HEAD_DIM), F32)] * 2
    else:
        out_specs[2] = pl.BlockSpec((1, NA_W, tm), lambda b, i: (b, 0, i))
        out_shape[2] = jax.ShapeDtypeStruct((b, NA_W, seq_len), BF16)
    return pl.pallas_call(
        _na_in_body,
        grid=(b, seq_len // tm),
        in_specs=[_tok_spec(tm, d), _mod_spec(mod), _const_spec((1, d)), _const_spec(w.shape)],
        out_specs=out_specs,
        out_shape=out_shape,
        compiler_params=_cparams(2),
        name="na_in_proj",
    )(x, mod, g_pre.reshape(1, d), w)


_Q_HEAD_ORDER = [h for j in range(Q_GROUP) for h in range(j, N_Q, Q_GROUP)]


def _permute_q_cols(w_in):
    q0 = 2 * RET_W + 2 * RET_VW
    heads = [w_in[:, q0 + h * HEAD_DIM:q0 + (h + 1) * HEAD_DIM] for h in _Q_HEAD_ORDER]
    return jnp.concatenate([w_in[:, :q0]] + heads + [w_in[:, q0 + GQA_W:]], axis=1)


def _permute_att_rows(w_att):
    return jnp.concatenate([w_att[h * HEAD_DIM:(h + 1) * HEAD_DIM] for h in _Q_HEAD_ORDER], axis=0)


def _trunk(x, mods, ctx_layers, wts):
    b, seq_len, d = x.shape
    is_ctx = ctx_layers is None
    depth = len(mods)
    flat = (lambda a: a.reshape(1, b * seq_len, a.shape[-1])) if is_ctx else (lambda a: a)
    unflat = (lambda a: a.reshape(b, seq_len, a.shape[-1])) if is_ctx else (lambda a: a)
    aux = []
    x = flat(x)
    for l in range(depth):
        mod = mods[l]
        i = l // 2
        x = _ffn(x, mod, 0, wts["norm_pre"][l, 0], wts["norm_post"][l, 0],
                 wts["ffn_w_in"][l][0], wts["ffn_w_out"][l][0])
        if l % 2 == 0:
            rope_tables = None if is_ctx else _rope_tables(seq_len)
            proj = _ab_in(x, mod, wts["norm_pre"][l, 1], wts["ab_w_in"][i], wts["gqa_q_norm"][i],
                          wts["gqa_k_norm"][i], rope_tables, is_ctx)
            rq, rk, rv, sg, gq, gk, gv = [unflat(a) for a in proj[:7]]
            if is_ctx:
                s_f = jnp.zeros((b, N_RET, RET_DK, RET_DV), F32)
                s_b = s_f
            else:
                ck, cv, s_f, s_b = ctx_layers[l]
                lc = ck.shape[1]
            y_ret, fin_f, fin_b = _retention(rq, rk, rv, sg, wts["ret_gn"][i], wts["lg_f"][i],
                                             wts["lg_b"][i], s_f.astype(F32), s_b.astype(F32))
            if is_ctx:
                y_att = _attention(gq, gk, gv, True, seq_len)
            else:
                y_att = _gqa_attention_t(gq, gk, gv, ck.reshape(b, lc, KV_W).astype(BF16),
                                         cv.reshape(b, lc, KV_W).astype(BF16), GQA_TQ)
            mixer = (wts["norm_post"][l, 1], [flat(y_ret), flat(y_att)],
                     [wts["ab_w_out_ret"][i], wts["ab_w_out_att"][i]])
            aux.append((proj[7], proj[8], fin_f, fin_b) if is_ctx else None)
        else:
            proj = _na_in(x, mod, wts["norm_pre"][l, 1], wts["na_w_qkv"][i], is_ctx)
            q, k, v = [unflat(a) for a in proj[:3]]
            if is_ctx:
                y = _attention(q, k, v, False, seq_len)
                aux.append((proj[3], proj[4]))
            else:
                ck, cv = ctx_layers[l]
                lc = ck.shape[1]
                y = _na_attention(q, k, v, ck.reshape(b, lc, NA_W).astype(BF16),
                                  cv.reshape(b, lc, NA_W).astype(BF16), wts["na_rpb"][i])
                aux.append(None)
            mixer = (wts["norm_post"][l, 1], [flat(y)], [wts["na_w_out"][i]])
        x = _ffn(x, mod, 2, wts["norm_pre"][l, 2], wts["norm_post"][l, 2],
                 wts["ffn_w_in"][l][1], wts["ffn_w_out"][l][1], mixer=mixer)
    return unflat(x), aux


def kernel(x_prompt, x_sample, cache_gqa_k, cache_gqa_v, state_ret_fwd, state_ret_bwd, cache_na_k,
           cache_na_v, c, c_ctx, mod_w, mod_b, norm_pre, norm_post, ffn_w_in, ffn_w_out, ab_w_in,
           ab_w_out, ret_decay_fwd, ret_decay_bwd, ret_gn, gqa_q_norm, gqa_k_norm, na_w_qkv, na_w_out,
           na_rpb):
    depth = mod_w.shape[0]
    d = x_prompt.shape[-1]
    n_dec = c.shape[0]
    batch, seq = x_prompt.shape[:2]

    n_cond = n_dec + 1
    pad = (-n_cond) % 8
    cond = jnp.concatenate([c, c_ctx[None, :], jnp.zeros((pad, d), F32)], axis=0)
    mod_all = _modulation(cond, mod_w, mod_b).reshape(depth, n_cond + pad, 3 * N_SUB, d)
    mods_sample = [mod_all[l, :n_dec] for l in range(depth)]
    mods_prompt = [mod_all[l, n_dec:n_dec + 1] for l in range(depth)]

    n_even = ab_w_in.shape[0]
    wts = {
        "norm_pre": norm_pre, "norm_post": norm_post,
        "ffn_w_in": [[ffn_w_in[l, s].astype(BF16) for s in range(2)] for l in range(depth)],
        "ffn_w_out": [[ffn_w_out[l, s].astype(BF16) for s in range(2)] for l in range(depth)],
        "ab_w_in": [_permute_q_cols(ab_w_in[i]).astype(BF16) for i in range(n_even)],
        "ab_w_out_ret": [ab_w_out[i, :RET_VW].astype(BF16) for i in range(n_even)],
        "ab_w_out_att": [_permute_att_rows(ab_w_out[i, RET_VW:]).astype(BF16) for i in range(n_even)],
        "lg_f": jax.nn.log_sigmoid(ret_decay_fwd.astype(F32)),
        "lg_b": jax.nn.log_sigmoid(ret_decay_bwd.astype(F32)),
        "ret_gn": ret_gn, "gqa_q_norm": gqa_q_norm, "gqa_k_norm": gqa_k_norm,
        "na_w_qkv": [na_w_qkv[i].astype(BF16) for i in range(na_w_qkv.shape[0])],
        "na_w_out": [na_w_out[i].astype(BF16) for i in range(na_w_out.shape[0])],
        "na_rpb": na_rpb,
    }

    y_prompt, ctx_aux = _trunk(x_prompt, mods_prompt, None, wts)

    ctx_layers = []
    for l in range(depth):
        i = l // 2
        if l % 2 == 0:
            ctx_layers.append((cache_gqa_k[:, i], cache_gqa_v[:, i], state_ret_fwd[:, i], state_ret_bwd[:, i]))
        else:
            ctx_layers.append((cache_na_k[:, i], cache_na_v[:, i]))
    y_sample, _ = _trunk(x_sample, mods_sample, ctx_layers, wts)

    def stack(idx, layers, tail):
        return jnp.stack([ctx_aux[l][idx].reshape((batch,) + tail) for l in layers], axis=1)

    even = range(0, depth, 2)
    odd = range(1, depth, 2)
    new_gqa_k = stack(0, even, (seq, N_KV, HEAD_DIM))
    new_gqa_v = stack(1, even, (seq, N_KV, HEAD_DIM))
    new_ret_fwd = stack(2, even, (N_RET, RET_DK, RET_DV))
    new_ret_bwd = stack(3, even, (N_RET, RET_DK, RET_DV))
    new_na_k = stack(0, odd, (seq, N_NA, HEAD_DIM))
    new_na_v = stack(1, odd, (seq, N_NA, HEAD_DIM))
    return (y_prompt, y_sample, new_gqa_k, new_gqa_v, new_ret_fwd, new_ret_bwd, new_na_k, new_na_v)
```

```python
import functools

import numpy as np
import jax
import jax.numpy as jnp
from jax import lax
from jax.experimental import pallas as pl
from jax.experimental.pallas import tpu as pltpu

F32 = jnp.float32
BF16 = jnp.bfloat16

GRID_W = 64
N_SUB = 3
HEAD_DIM = 64
N_RET = 8
RET_DK = 64
RET_DV = 64
RET_GN_EPS = 1e-5
N_Q = 8
N_KV = 2
Q_GROUP = N_Q // N_KV
ROPE_THETA = 10000.0
N_NA = 16
NA_KR_MAX = 8
NA_KC = 16
EPS = 1e-6
RET_W = N_RET * RET_DK
RET_VW = N_RET * RET_DV
GQA_W = N_Q * HEAD_DIM
KV_W = N_KV * HEAD_DIM
NA_W = N_NA * HEAD_DIM

LANES = 128
VMEM_LIMIT = 56 * 1024 * 1024
RET_CHUNK = 256
RET_UNROLL = 4
NA_ROWS_PER_STEP = 4
FFN_SUBTILES = 2
NA_UNROLL = 8
GQA_TQ = 256
GQA_KEY_CHUNK = 1024
MASK_VALUE = -1e30
LOG2E = 1.4426950408889634
SCORE_SCALE = HEAD_DIM ** -0.5 * LOG2E


def _cparams(n_grid):
    return pltpu.CompilerParams(dimension_semantics=("parallel",) * n_grid,
                                vmem_limit_bytes=VMEM_LIMIT)


def _const_spec(shape):
    nd = len(shape)
    return pl.BlockSpec(shape, lambda *_: (0,) * nd, pipeline_mode=pl.Buffered(1))


def _silu(x):
    return x * jax.nn.sigmoid(x)


def _rms(x, g):
    ms = jnp.mean(x * x, axis=-1, keepdims=True)
    return x * lax.rsqrt(ms + EPS) * g


def _modulated(x, m, sub, g_pre):
    shift = m[3 * sub:3 * sub + 1]
    scale = m[3 * sub + 1:3 * sub + 2]
    return _rms(x, g_pre) * (1.0 + scale) + shift


def _dot(a, b):
    return jnp.dot(a, b, preferred_element_type=F32)


def _dot_nt(a, b):
    return lax.dot_general(a, b, (((1,), (1,)), ((), ())), preferred_element_type=F32)


def _dot_tn(a, b):
    return lax.dot_general(a, b, (((0,), (0,)), ((), ())), preferred_element_type=F32)


def _low_half(shape):
    return lax.broadcasted_iota(jnp.int32, shape, len(shape) - 1) < HEAD_DIM


def _mod_body(c_ref, w_ref, b_ref, o_ref):
    a = _silu(c_ref[...]).astype(BF16)
    o_ref[0] = _dot(a, w_ref[0].astype(BF16)) + b_ref[0]


def _modulation(cond, mod_w, mod_b):
    depth, d, n = mod_w.shape
    rows = cond.shape[0]
    tn = 1152 if n % 1152 == 0 else n
    return pl.pallas_call(
        _mod_body,
        grid=(depth, n // tn),
        in_specs=[pl.BlockSpec((rows, d), lambda l, j: (0, 0)),
                  pl.BlockSpec((1, d, tn), lambda l, j: (l, 0, j)),
                  pl.BlockSpec((1, 1, tn), lambda l, j: (l, 0, j))],
        out_specs=pl.BlockSpec((1, rows, tn), lambda l, j: (l, 0, j)),
        out_shape=jax.ShapeDtypeStruct((depth, rows, n), F32),
        compiler_params=_cparams(2),
        name="modulation",
    )(cond, mod_w, mod_b.reshape(depth, 1, n))


def _tok_spec(tm, width):
    return pl.BlockSpec((1, tm, width), lambda b, i: (b, i, 0))


def _mod_spec(mod):
    nm, rows, d = mod.shape
    if nm == 1:
        return pl.BlockSpec((1, rows, d), lambda b, i: (0, 0, 0))
    return pl.BlockSpec((1, rows, d), lambda b, i: (b, 0, 0))


def _token_tile(seq_len):
    return 512 if seq_len % 512 == 0 else seq_len


def _ffn_body(*refs, sub, dff, n_seg, n_sub):
    x_ref, mod_ref, gpre_ref, gpost_ref, win_ref, wout_ref = refs[:6]
    rest = refs[6:]
    if n_seg:
        gmix_ref = rest[0]
        seg_refs = rest[1:1 + n_seg]
        wmix_refs = rest[1 + n_seg:1 + 2 * n_seg]
    o_ref = refs[-1]
    m = mod_ref[0]
    tm = x_ref.shape[1]
    ts = tm // n_sub
    rows = [slice(t * ts, (t + 1) * ts) for t in range(n_sub)]

    def residual(t):
        x = x_ref[0, rows[t], :]
        if n_seg:
            y = None
            for s_ref, w_ref in zip(seg_refs, wmix_refs):
                part = _dot(s_ref[0, rows[t], :].astype(BF16), w_ref[...])
                y = part if y is None else y + part
            x = x + m[5:6] * _rms(y, gmix_ref[...])
        return x

    def hidden(t, x):
        h = _modulated(x, m, sub, gpre_ref[...]).astype(BF16)
        gu = _dot(h, win_ref[...])
        return (_silu(gu[:, :dff]) * gu[:, dff:]).astype(BF16)

    def finish(t, x, y):
        gate = m[3 * sub + 2:3 * sub + 3]
        o_ref[0, rows[t], :] = x + (0.5 * gate) * _rms(y, gpost_ref[...])

    xs = [residual(t) for t in range(n_sub)]
    acts = [hidden(t, xs[t]) for t in range(n_sub)]
    ys = [_dot(a, wout_ref[...]) for a in acts]
    for t in range(n_sub):
        finish(t, xs[t], ys[t])


def _ffn(x, mod, sub, g_pre, g_post, w_in, w_out, mixer=None):
    b, seq_len, d = x.shape
    dff = w_out.shape[0]
    tm = _token_tile(seq_len)
    ins = [x, mod, g_pre.reshape(1, d), g_post.reshape(1, d), w_in, w_out]
    in_specs = [_tok_spec(tm, d), _mod_spec(mod), _const_spec((1, d)), _const_spec((1, d)),
                _const_spec(w_in.shape), _const_spec(w_out.shape)]
    n_seg = 0
    if mixer is not None:
        g_mix, segs, ws = mixer
        n_seg = len(segs)
        ins += [g_mix.reshape(1, d)] + list(segs) + list(ws)
        in_specs += ([_const_spec((1, d))] + [_tok_spec(tm, s.shape[-1]) for s in segs]
                     + [_const_spec(w.shape) for w in ws])
    n_sub = FFN_SUBTILES if tm % (8 * FFN_SUBTILES) == 0 else 1
    return pl.pallas_call(
        functools.partial(_ffn_body, sub=sub, dff=dff, n_seg=n_seg, n_sub=n_sub),
        grid=(b, seq_len // tm),
        in_specs=in_specs,
        out_specs=_tok_spec(tm, d),
        out_shape=jax.ShapeDtypeStruct(x.shape, F32),
        compiler_params=_cparams(2),
        name="ffn_half" if mixer is None else "mixer_out_ffn_half",
    )(*ins)


def _head_rms(a, g, bd):
    ss = _dot((a * a).astype(BF16), bd)
    return a * lax.rsqrt(ss * (1.0 / HEAD_DIM) + EPS) * g


def _rope(a, cos, sin):
    quarter = HEAD_DIM // 4
    outs = []
    for j in range(a.shape[1] // LANES):
        aj = a[:, j * LANES:(j + 1) * LANES]
        up = pltpu.roll(aj, LANES - quarter, axis=1)
        dn = pltpu.roll(aj, quarter, axis=1)
        lane = lax.broadcasted_iota(jnp.int32, aj.shape, 1)
        partner = jnp.where((lane % (2 * quarter)) < quarter, up, dn)
        outs.append(aj * cos + partner * sin)
    return outs[0] if len(outs) == 1 else jnp.concatenate(outs, axis=1)


def _ab_in_body(*refs, rope, aux):
    x_ref, mod_ref, gpre_ref, w_ref, qn_ref, kn_ref, bd_ref = refs[:7]
    refs = refs[7:]
    if rope:
        cos_ref, sin_ref = refs[:2]
        refs = refs[2:]
    rq_o, rk_o, rv_o, sg_o, gq_o, gk_o, gv_o = refs[:7]
    h = _modulated(x_ref[0], mod_ref[0], 1, gpre_ref[...]).astype(BF16)
    p = _dot(h, w_ref[...])
    o = 0
    rq_o[0] = p[:, o:o + RET_W].astype(rq_o.dtype)
    o += RET_W
    rk_o[0] = (p[:, o:o + RET_W] * (RET_DK ** -0.5)).astype(rk_o.dtype)
    o += RET_W
    rv_o[0] = p[:, o:o + RET_VW].astype(rv_o.dtype)
    o += RET_VW
    sg_o[0] = _silu(p[:, o:o + RET_VW]).astype(sg_o.dtype)
    o += RET_VW
    gq = p[:, o:o + GQA_W]
    o += GQA_W
    gk = p[:, o:o + KV_W]
    o += KV_W
    gv = p[:, o:o + KV_W]
    bd = bd_ref[...]
    qh = _head_rms(gq, qn_ref[...], bd)
    kh = _head_rms(gk, kn_ref[...], bd[:KV_W, :KV_W])
    if rope:
        qh = _rope(qh, cos_ref[...], sin_ref[...])
        kh = _rope(kh, cos_ref[...], sin_ref[...])
    gq_o[0] = (qh * SCORE_SCALE).astype(gq_o.dtype)
    gk_o[0] = kh.astype(gk_o.dtype)
    gv_o[0] = gv.astype(gv_o.dtype)
    if aux:
        k5_o, v5_o = refs[7:]
        k5_o[...] = kh.reshape(k5_o.shape)
        v5_o[...] = gv.reshape(v5_o.shape)


def _ab_in(x, mod, g_pre, w, qn, kn, rope_tables, aux):
    b, seq_len, d = x.shape
    tm = _token_tile(seq_len)
    rope = rope_tables is not None
    bd = jnp.asarray(np.kron(np.eye(N_Q), np.ones((HEAD_DIM, HEAD_DIM))), BF16)
    ins = [x, mod, g_pre.reshape(1, d), w, jnp.tile(qn, N_Q).reshape(1, GQA_W),
           jnp.tile(kn, N_KV).reshape(1, KV_W), bd]
    in_specs = [_tok_spec(tm, d), _mod_spec(mod), _const_spec((1, d)), _const_spec(w.shape),
                _const_spec((1, GQA_W)), _const_spec((1, KV_W)), _const_spec(bd.shape)]
    if rope:
        ins += list(rope_tables)
        in_specs += [pl.BlockSpec((tm, LANES), lambda b, i: (i, 0))] * 2
    widths = (RET_W, RET_W, RET_VW, RET_VW, GQA_W, KV_W, KV_W)
    out_specs = [_tok_spec(tm, wd) for wd in widths]
    out_shape = [jax.ShapeDtypeStruct((b, seq_len, wd), BF16) for wd in widths]
    if aux:
        assert b == 1
        out_specs += [pl.BlockSpec((tm, N_KV, HEAD_DIM), lambda b, i: (i, 0, 0))] * 2
        out_shape += [jax.ShapeDtypeStruct((seq_len, N_KV, HEAD_DIM), F32)] * 2
    return pl.pallas_call(
        functools.partial(_ab_in_body, rope=rope, aux=aux),
        grid=(b, seq_len // tm),
        in_specs=in_specs,
        out_specs=out_specs,
        out_shape=out_shape,
        compiler_params=_cparams(2),
        name="ab_in_proj",
    )(*ins)


def _rope_tables(seq_len):
    t = jnp.arange(seq_len)
    quarter = HEAD_DIM // 4
    inv = ROPE_THETA ** (-jnp.arange(quarter, dtype=F32) / quarter)
    ang_r = (t // GRID_W).astype(F32)[:, None] * inv[None, :]
    ang_c = (t % GRID_W).astype(F32)[:, None] * inv[None, :]
    cos = jnp.concatenate([jnp.cos(ang_r)] * 2 + [jnp.cos(ang_c)] * 2, axis=1)
    sin = jnp.concatenate([-jnp.sin(ang_r), jnp.sin(ang_r), -jnp.sin(ang_c), jnp.sin(ang_c)], axis=1)
    return jnp.tile(cos, (1, LANES // HEAD_DIM)), jnp.tile(sin, (1, LANES // HEAD_DIM))


def _ret_body(lgf_ref, lgb_ref, rq_ref, rk_ref, rv_ref, sg_ref, gn_ref, sf_ref, sb_ref,
              y_ref, ff_ref, fb_ref, sbs_ref, *, chunk, n_chunks):
    pair = pl.program_id(1)
    c = chunk
    lo_row = _low_half((1, LANES))
    lgf = jnp.where(lo_row, lgf_ref[2 * pair], lgf_ref[2 * pair + 1])
    lgb = jnp.where(lo_row, lgb_ref[2 * pair], lgb_ref[2 * pair + 1])
    idx = lax.broadcasted_iota(jnp.int32, (c, 1), 0).astype(F32)
    qdec_f = jnp.exp(lgf * (idx + 1.0))
    kdec_f = jnp.exp(lgf * (c - 1.0 - idx))
    cdec_f = jnp.exp(lgf * float(c))
    qdec_b = jnp.exp(lgb * (c - idx))
    kdec_b = jnp.exp(lgb * idx)
    cdec_b = jnp.exp(lgb * float(c))
    diff = (lax.broadcasted_iota(jnp.int32, (c, c), 0)
            - lax.broadcasted_iota(jnp.int32, (c, c), 1)).astype(F32)
    dmat = [jnp.exp(jnp.where(diff >= 0, lgf_ref[2 * pair + h], -lgb_ref[2 * pair + h]) * diff)
            for h in range(2)]
    rr = lax.broadcasted_iota(jnp.int32, (LANES, LANES), 0) < HEAD_DIM
    cc = lax.broadcasted_iota(jnp.int32, (LANES, LANES), 1) < HEAD_DIM
    same_head = rr == cc
    lo = _low_half((c, LANES))

    def load(ref, n):
        return ref[0, pl.ds(pl.multiple_of(n * c, c), c), :]

    def bwd_step(t, state):
        n = n_chunks - 1 - t
        sbs_ref[n] = state
        k = load(rk_ref, n).astype(F32)
        v = load(rv_ref, n).astype(BF16)
        kv = _dot_tn((k * kdec_b).astype(BF16), v)
        return state * cdec_b + jnp.where(same_head, kv, 0.0)

    unroll = int(np.gcd(n_chunks, RET_UNROLL))
    _store_head_states(fb_ref, lax.fori_loop(0, n_chunks, bwd_step, sb_ref[0, 0], unroll=unroll))

    gn = gn_ref[...]

    def head_mean(a):
        m0 = jnp.sum(jnp.where(lo, a, 0.0), axis=-1, keepdims=True)
        m1 = jnp.sum(jnp.where(lo, 0.0, a), axis=-1, keepdims=True)
        return jnp.where(lo, m0, m1) * (1.0 / RET_DV)

    def fwd_trip(t, state0):
        carried = {"state": state0}

        def scores(u):
            n = t * unroll + u
            q = load(rq_ref, n).astype(BF16)
            k = load(rk_ref, n).astype(BF16)
            v = load(rv_ref, n).astype(BF16)
            kv = jnp.where(same_head, _dot_tn((k.astype(F32) * kdec_f).astype(BF16), v), 0.0)
            return n, q, v, kv, [_dot_nt(_only_head(q, h), k) for h in range(2)]

        def probs(u, sc):
            n, q, v, kv, s = sc
            return n, q, v, kv, [(s[h] * dmat[h]).astype(BF16) for h in range(2)]

        def accumulate(u, pr):
            n, q, v, kv, p = pr
            state = carried["state"]
            qf = q.astype(F32)
            o = (_dot((qf * qdec_f).astype(BF16), state.astype(BF16))
                 + _dot((qf * qdec_b).astype(BF16), sbs_ref[n].astype(BF16)))
            o = o + jnp.where(lo, _dot(p[0], v), _dot(p[1], v))
            dev = o - head_mean(o)
            var = head_mean(dev * dev)
            y = dev * lax.rsqrt(var + RET_GN_EPS) * gn * load(sg_ref, n).astype(F32)
            y_ref[0, pl.ds(pl.multiple_of(n * c, c), c), :] = y.astype(y_ref.dtype)
            carried["state"] = state * cdec_f + kv

        _staged(list(range(unroll)), scores, probs, accumulate)
        return carried["state"]

    _store_head_states(ff_ref, lax.fori_loop(0, n_chunks // unroll, fwd_trip, sf_ref[0, 0]))


def _store_head_states(ref, state):
    ref[0, 0] = state[:RET_DK, :RET_DV]
    ref[0, 1] = state[RET_DK:, RET_DV:]


def _pair_states(s):
    b, nh, dk, dv = s.shape
    s = s.reshape(b, nh // 2, 2, dk, dv)
    z = jnp.zeros_like(s[:, :, 0])
    top = jnp.concatenate([s[:, :, 0], z], axis=-1)
    bot = jnp.concatenate([z, s[:, :, 1]], axis=-1)
    return jnp.concatenate([top, bot], axis=-2)


def _retention(rq, rk, rv, sg, gn, lg_f, lg_b, s_f, s_b):
    b, seq_len, _ = rq.shape
    c = RET_CHUNK if seq_len % RET_CHUNK == 0 else seq_len
    nc = seq_len // c
    npair = N_RET // 2
    seq_spec = pl.BlockSpec((1, seq_len, LANES), lambda i, p: (i, 0, p))
    st_spec = pl.BlockSpec((1, 1, LANES, LANES), lambda i, p: (i, p, 0, 0))
    fin_spec = pl.BlockSpec((1, 2, RET_DK, RET_DV), lambda i, p: (i, p, 0, 0))
    smem = pl.BlockSpec(memory_space=pltpu.SMEM)
    st_shape = jax.ShapeDtypeStruct((b, N_RET, RET_DK, RET_DV), F32)
    return pl.pallas_call(
        functools.partial(_ret_body, chunk=c, n_chunks=nc),
        grid=(b, npair),
        in_specs=[smem, smem, seq_spec, seq_spec, seq_spec, seq_spec,
                  pl.BlockSpec((1, LANES), lambda i, p: (0, p)), st_spec, st_spec],
        out_specs=[seq_spec, fin_spec, fin_spec],
        out_shape=[jax.ShapeDtypeStruct((b, seq_len, RET_VW), BF16), st_shape, st_shape],
        scratch_shapes=[pltpu.VMEM((nc, LANES, LANES), F32)],
        compiler_params=_cparams(2),
        name="retention",
    )(lg_f, lg_b, rq, rk, rv, sg, gn.reshape(1, RET_VW), _pair_states(s_f), _pair_states(s_b))


def _staged(items, scores, probs, accumulate, ahead=2):
    n = len(items)
    queue = [scores(items[i]) for i in range(min(ahead, n))]
    pending = None
    for i in range(n):
        sc = queue.pop(0)
        if i + ahead < n:
            queue.append(scores(items[i + ahead]))
        ps = probs(items[i], sc)
        if pending is not None:
            accumulate(*pending)
        pending = (items[i], ps)
    accumulate(*pending)


def _row_probs(s_list):
    m = s_list[0].max(axis=-1, keepdims=True)
    for s in s_list[1:]:
        m = jnp.maximum(m, s.max(axis=-1, keepdims=True))
    ps = [jnp.exp2(s - m) for s in s_list]
    den = ps[0].sum(axis=-1, keepdims=True)
    for p in ps[1:]:
        den = den + p.sum(axis=-1, keepdims=True)
    return [p.astype(BF16) for p in ps], 1.0 / den


def _only_head(q, h):
    lo = _low_half(q.shape)
    zero = jnp.zeros_like(q)
    return jnp.where(lo, q, zero) if h == 0 else jnp.where(lo, zero, q)


def _attn_body(q_ref, k_ref, v_ref, o_ref, *, n_groups, kv_shared):
    tq = q_ref.shape[1]
    lo = _low_half((tq, LANES))
    outs = {}

    def lanes(j):
        return slice(j * LANES, (j + 1) * LANES)

    def scores(item):
        j, h = item
        k = k_ref[0] if kv_shared else k_ref[0, :, lanes(j)]
        return _dot_nt(_only_head(q_ref[0, :, lanes(j)].astype(BF16), h), k.astype(BF16))

    def probs(item, s):
        return _row_probs([s])

    def accumulate(item, pr):
        j, h = item
        ps, inv_den = pr
        v = v_ref[0] if kv_shared else v_ref[0, :, lanes(j)]
        outs[h] = _dot(ps[0], v.astype(BF16)) * inv_den
        if h == 1:
            o_ref[0, :, lanes(j)] = jnp.where(lo, outs[0], outs[1]).astype(o_ref.dtype)

    _staged([(j, h) for j in range(n_groups) for h in range(2)], scores, probs, accumulate)


def _col_max(x, slab=64):
    n, t = x.shape
    if n > slab and n % slab == 0:
        x = x.reshape(n // slab, slab, t).max(axis=0)
    return x.max(axis=0, keepdims=True)


def _gqa_t_body(q_ref, kn_ref, kc_ref, vtn_ref, vtc_ref, o_ref, *, n_groups, chunk):
    tq = q_ref.shape[1]
    top = lax.broadcasted_iota(jnp.int32, (LANES, tq), 0) < HEAD_DIM
    ln, lc = kn_ref.shape[1], kc_ref.shape[1]
    chunks = [(kn_ref, vtn_ref, c * chunk, chunk) for c in range(ln // chunk)] + [(kc_ref, vtc_ref, 0, lc)]
    items = [(j, h, c) for j in range(n_groups) for h in range(2) for c in range(len(chunks))]

    def scores(item):
        j, h, c = item
        k_ref, _, off, size = chunks[c]
        qh = _only_head(q_ref[0, :, j * LANES:(j + 1) * LANES], h)
        return _dot_nt(k_ref[0, off:off + size, :], qh)

    outs = {}
    state = {"m": None, "acc": None}

    def probs(item, st):
        c = item[2]
        mc = _col_max(st)
        m_new = mc if c == 0 else jnp.maximum(state["m"], mc)
        alpha = None if c == 0 else jnp.exp2(state["m"] - m_new)
        state["m"] = m_new
        return jnp.exp2(st - m_new).astype(BF16), alpha

    def accumulate(item, ps):
        j, h, c = item
        p, alpha = ps
        _, vt_ref, off, size = chunks[c]
        o = _dot(vt_ref[0, h, :, off:off + size], p)
        acc = o if c == 0 else state["acc"] * alpha + o
        state["acc"] = acc
        if c == len(chunks) - 1:
            den = acc[HEAD_DIM:HEAD_DIM + 1] if h == 0 else acc[0:1]
            outs[h] = acc * (1.0 / den)
            if h == 1:
                sl = slice(j * LANES, (j + 1) * LANES)
                o_ref[0, :, sl] = jnp.where(top, outs[0], outs[1]).T.astype(o_ref.dtype)

    _staged(items, scores, probs, accumulate)


def _vt_with_ones(v):
    vt = jnp.swapaxes(v, 1, 2)
    top = (np.arange(KV_W) < HEAD_DIM)[None, :, None]
    one = jnp.ones_like(vt)
    return jnp.stack([jnp.where(top, vt, one), jnp.where(top, one, vt)], axis=1)


def _gqa_attention_t(q, k_new, v_new, k_ctx, v_ctx, tq):
    b, lq, wq = q.shape
    ln, lc = k_new.shape[1], k_ctx.shape[1]
    chunk = GQA_KEY_CHUNK if ln % GQA_KEY_CHUNK == 0 else ln
    whole3 = lambda i, t: (i, 0, 0)
    whole4 = lambda i, t: (i, 0, 0, 0)
    return pl.pallas_call(
        functools.partial(_gqa_t_body, n_groups=wq // LANES, chunk=chunk),
        grid=(b, lq // tq),
        in_specs=[pl.BlockSpec((1, tq, wq), lambda i, t: (i, t, 0)),
                  pl.BlockSpec((1, ln, KV_W), whole3), pl.BlockSpec((1, lc, KV_W), whole3),
                  pl.BlockSpec((1, 2, KV_W, ln), whole4), pl.BlockSpec((1, 2, KV_W, lc), whole4)],
        out_specs=pl.BlockSpec((1, tq, wq), lambda i, t: (i, t, 0)),
        out_shape=jax.ShapeDtypeStruct((b, lq, wq), BF16),
        compiler_params=_cparams(2),
        name="gqa_attention_t",
    )(q, k_new, k_ctx, _vt_with_ones(v_new), _vt_with_ones(v_ctx))


def _attention(q, k, v, kv_shared, tq):
    b, lq, wq = q.shape
    lk, wk = k.shape[1:]
    return pl.pallas_call(
        functools.partial(_attn_body, n_groups=wq // LANES, kv_shared=kv_shared),
        grid=(b, lq // tq),
        in_specs=[pl.BlockSpec((1, tq, wq), lambda i, t: (i, t, 0)),
                  pl.BlockSpec((1, lk, wk), lambda i, t: (i, 0, 0)),
                  pl.BlockSpec((1, lk, wk), lambda i, t: (i, 0, 0))],
        out_specs=pl.BlockSpec((1, tq, wq), lambda i, t: (i, t, 0)),
        out_shape=jax.ShapeDtypeStruct((b, lq, wq), BF16),
        compiler_params=_cparams(2),
        name="dense_attention",
    )(q, k, v)


def _na_plan(rows):
    g = NA_ROWS_PER_STEP
    kr = min(NA_KR_MAX, rows)
    u = min(-(-(g + kr - 1) // 2) * 2, rows)
    n_steps = rows // g
    bases, vids, variants, keys = [], [], [], {}
    for s in range(n_steps):
        base = int(np.clip(s * g - kr // 2, 0, rows - u)) // 2 * 2
        r = s * g + np.arange(g)[:, None]
        krow = base + np.arange(u)[None, :]
        r0 = np.clip(r - kr // 2, 0, rows - kr)
        valid = (krow >= r0) & (krow < r0 + kr)
        assert (valid.sum(axis=1) == kr).all(), "key window not covered by the step's row range"
        dr = np.where(valid, krow - r + (NA_KR_MAX - 1), 0)
        key = (dr.tobytes(), valid.tobytes())
        if key not in keys:
            keys[key] = len(variants)
            variants.append((dr, valid))
        bases.append(base)
        vids.append(keys[key])
    return u, bases, vids, variants


def _na_bias_tables(rpb, variants, u):
    g = NA_ROWS_PER_STEP
    nh, n_dr, n_dc = rpb.shape
    cols = np.arange(GRID_W)
    c0 = np.clip(cols - NA_KC // 2, 0, GRID_W - NA_KC)
    kc = np.arange(GRID_W)[None, :]
    cvalid = (kc >= c0[:, None]) & (kc < c0[:, None] + NA_KC)
    dc = kc - cols[:, None] + (NA_KC - 1)
    sel_c = cvalid[None] & (dc[None] == np.arange(n_dc)[:, None, None])
    band = jnp.einsum("hde,eck->hdkc", rpb.astype(F32), jnp.asarray(sel_c, F32),
                      precision=lax.Precision.HIGHEST)
    band = jnp.where(cvalid.T[None, None], band * LOG2E, MASK_VALUE)
    masked = jnp.full((nh, GRID_W, GRID_W), MASK_VALUE, F32)
    tables = []
    for dr, rvalid in variants:
        key_rows = [jnp.stack([band[:, dr[q, a]] if rvalid[q, a] else masked for q in range(g)], axis=2)
                    for a in range(u)]
        tables.append(jnp.stack(key_rows, axis=1).reshape(nh, u * GRID_W, g * GRID_W))
    return jnp.stack(tables)


def _na_body(base_ref, vid_ref, q_ref, k_ref, vt_ref, kc_ref, vct_ref, bias_ref, o_ref, *, n_steps, tq, tk):
    top = lax.broadcasted_iota(jnp.int32, (LANES, tq), 0) < HEAD_DIM
    kc = kc_ref[0]
    lc = kc.shape[0]
    ones_rows = 8
    vct = jnp.concatenate([vct_ref[0], jnp.ones((ones_rows, lc), BF16)], axis=0)
    ones_win = jnp.ones((ones_rows, tk), BF16)
    unroll = int(np.gcd(n_steps, NA_UNROLL))

    def trip(t, carry):
        items = [(u, h) for u in range(unroll) for h in range(2)]
        steps = [t * unroll + u for u in range(unroll)]
        outs = {}

        def q_rows(u):
            return pl.ds(pl.multiple_of(steps[u] * tq, tq), tq)

        def k_rows(u):
            return pl.ds(pl.multiple_of(base_ref[steps[u]] * GRID_W, LANES), tk)

        def scores(item):
            u, h = item
            qh = _only_head(q_ref[0, q_rows(u), :], h)
            st_win = _dot_nt(k_ref[0, k_rows(u), :], qh) + bias_ref[vid_ref[steps[u]], h]
            return st_win, _dot_nt(kc, qh)

        def probs(item, sc):
            m = jnp.maximum(_col_max(sc[0]), _col_max(sc[1]))
            return [jnp.exp2(s - m).astype(BF16) for s in sc]

        def accumulate(item, ps):
            u, h = item
            vt = jnp.concatenate([vt_ref[0, :, k_rows(u)], ones_win], axis=0)
            acc = _dot(vt, ps[0]) + _dot(vct, ps[1])
            outs[h] = acc[:LANES] * (1.0 / acc[LANES:LANES + 1])
            if h == 1:
                o_ref[0, q_rows(u), :] = jnp.where(top, outs[0], outs[1]).T.astype(o_ref.dtype)

        _staged(items, scores, probs, accumulate)
        return carry

    lax.fori_loop(0, n_steps // unroll, trip, 0)


def _na_attention(q, k, vt, k_ctx, v_ctx, rpb):
    b, seq_len, w = q.shape
    rows = seq_len // GRID_W
    lc = k_ctx.shape[1]
    u, bases, vids, variants = _na_plan(rows)
    bias = _na_bias_tables(rpb, variants, u)
    nv = bias.shape[0]
    tq = NA_ROWS_PER_STEP * GRID_W
    tk = u * GRID_W
    n_steps = rows // NA_ROWS_PER_STEP
    smem = pl.BlockSpec(memory_space=pltpu.SMEM)
    seq_spec = pl.BlockSpec((1, seq_len, LANES), lambda p, i: (i, 0, p))
    return pl.pallas_call(
        functools.partial(_na_body, n_steps=n_steps, tq=tq, tk=tk),
        grid=(w // LANES, b),
        in_specs=[smem, smem, seq_spec, seq_spec,
                  pl.BlockSpec((1, LANES, seq_len), lambda p, i: (i, p, 0)),
                  pl.BlockSpec((1, lc, LANES), lambda p, i: (i, 0, p)),
                  pl.BlockSpec((1, LANES, lc), lambda p, i: (i, p, 0)),
                  pl.BlockSpec((nv, 2, tk, tq), lambda p, i: (0, p, 0, 0))],
        out_specs=seq_spec,
        out_shape=jax.ShapeDtypeStruct((b, seq_len, w), BF16),
        compiler_params=_cparams(2),
        name="neighbourhood_attention",
    )(jnp.asarray(bases, jnp.int32), jnp.asarray(vids, jnp.int32), q, k, vt, k_ctx,
      jnp.swapaxes(v_ctx, 1, 2), bias)


def _na_in_body(x_ref, mod_ref, gpre_ref, w_ref, q_o, k_o, v_o, *aux_o):
    h = _modulated(x_ref[0], mod_ref[0], 1, gpre_ref[...]).astype(BF16)
    p = _dot(h, w_ref[...])
    q_o[0] = (p[:, :NA_W] * SCORE_SCALE).astype(q_o.dtype)
    k = p[:, NA_W:2 * NA_W]
    v = p[:, 2 * NA_W:]
    k_o[0] = k.astype(k_o.dtype)
    v_o[0] = (v if aux_o else v.T).astype(v_o.dtype)
    if aux_o:
        k5_o, v5_o = aux_o
        k5_o[...] = k.reshape(k5_o.shape)
        v5_o[...] = v.reshape(v5_o.shape)


def _na_in(x, mod, g_pre, w, aux):
    b, seq_len, d = x.shape
    tm = _token_tile(seq_len)
    out_specs = [_tok_spec(tm, NA_W)] * 3
    out_shape = [jax.ShapeDtypeStruct((b, seq_len, NA_W), BF16)] * 3
    if aux:
        assert b == 1
        out_specs += [pl.BlockSpec((tm, N_NA, HEAD_DIM), lambda b, i: (i, 0, 0))] * 2
        out_shape += [jax.ShapeDtypeStruct((seq_len, N_NA, HEAD_DIM), F32)] * 2
    else:
        out_specs[2] = pl.BlockSpec((1, NA_W, tm), lambda b, i: (b, 0, i))
        out_shape[2] = jax.ShapeDtypeStruct((b, NA_W, seq_len), BF16)
    return pl.pallas_call(
        _na_in_body,
        grid=(b, seq_len // tm),
        in_specs=[_tok_spec(tm, d), _mod_spec(mod), _const_spec((1, d)), _const_spec(w.shape)],
        out_specs=out_specs,
        out_shape=out_shape,
        compiler_params=_cparams(2),
        name="na_in_proj",
    )(x, mod, g_pre.reshape(1, d), w)


_Q_HEAD_ORDER = [h for j in range(Q_GROUP) for h in range(j, N_Q, Q_GROUP)]


def _permute_q_cols(w_in):
    q0 = 2 * RET_W + 2 * RET_VW
    heads = [w_in[:, q0 + h * HEAD_DIM:q0 + (h + 1) * HEAD_DIM] for h in _Q_HEAD_ORDER]
    return jnp.concatenate([w_in[:, :q0]] + heads + [w_in[:, q0 + GQA_W:]], axis=1)


def _permute_att_rows(w_att):
    return jnp.concatenate([w_att[h * HEAD_DIM:(h + 1) * HEAD_DIM] for h in _Q_HEAD_ORDER], axis=0)


def _trunk(x, mods, ctx_layers, wts):
    b, seq_len, d = x.shape
    is_ctx = ctx_layers is None
    depth = len(mods)
    flat = (lambda a: a.reshape(1, b * seq_len, a.shape[-1])) if is_ctx else (lambda a: a)
    unflat = (lambda a: a.reshape(b, seq_len, a.shape[-1])) if is_ctx else (lambda a: a)
    aux = []
    x = flat(x)
    for l in range(depth):
        mod = mods[l]
        i = l // 2
        x = _ffn(x, mod, 0, wts["norm_pre"][l, 0], wts["norm_post"][l, 0],
                 wts["ffn_w_in"][l][0], wts["ffn_w_out"][l][0])
        if l % 2 == 0:
            rope_tables = None if is_ctx else _rope_tables(seq_len)
            proj = _ab_in(x, mod, wts["norm_pre"][l, 1], wts["ab_w_in"][i], wts["gqa_q_norm"][i],
                          wts["gqa_k_norm"][i], rope_tables, is_ctx)
            rq, rk, rv, sg, gq, gk, gv = [unflat(a) for a in proj[:7]]
            if is_ctx:
                s_f = jnp.zeros((b, N_RET, RET_DK, RET_DV), F32)
                s_b = s_f
            else:
                ck, cv, s_f, s_b = ctx_layers[l]
                lc = ck.shape[1]
            y_ret, fin_f, fin_b = _retention(rq, rk, rv, sg, wts["ret_gn"][i], wts["lg_f"][i],
                                             wts["lg_b"][i], s_f.astype(F32), s_b.astype(F32))
            if is_ctx:
                y_att = _attention(gq, gk, gv, True, seq_len)
            else:
                y_att = _gqa_attention_t(gq, gk, gv, ck.reshape(b, lc, KV_W).astype(BF16),
                                         cv.reshape(b, lc, KV_W).astype(BF16), GQA_TQ)
            mixer = (wts["norm_post"][l, 1], [flat(y_ret), flat(y_att)],
                     [wts["ab_w_out_ret"][i], wts["ab_w_out_att"][i]])
            aux.append((proj[7], proj[8], fin_f, fin_b) if is_ctx else None)
        else:
            proj = _na_in(x, mod, wts["norm_pre"][l, 1], wts["na_w_qkv"][i], is_ctx)
            q, k, v = [unflat(a) for a in proj[:3]]
            if is_ctx:
                y = _attention(q, k, v, False, seq_len)
                aux.append((proj[3], proj[4]))
            else:
                ck, cv = ctx_layers[l]
                lc = ck.shape[1]
                y = _na_attention(q, k, v, ck.reshape(b, lc, NA_W).astype(BF16),
                                  cv.reshape(b, lc, NA_W).astype(BF16), wts["na_rpb"][i])
                aux.append(None)
            mixer = (wts["norm_post"][l, 1], [flat(y)], [wts["na_w_out"][i]])
        x = _ffn(x, mod, 2, wts["norm_pre"][l, 2], wts["norm_post"][l, 2],
                 wts["ffn_w_in"][l][1], wts["ffn_w_out"][l][1], mixer=mixer)
    return unflat(x), aux


def kernel(x_prompt, x_sample, cache_gqa_k, cache_gqa_v, state_ret_fwd, state_ret_bwd, cache_na_k,
           cache_na_v, c, c_ctx, mod_w, mod_b, norm_pre, norm_post, ffn_w_in, ffn_w_out, ab_w_in,
           ab_w_out, ret_decay_fwd, ret_decay_bwd, ret_gn, gqa_q_norm, gqa_k_norm, na_w_qkv, na_w_out,
           na_rpb):
    depth = mod_w.shape[0]
    d = x_prompt.shape[-1]
    n_dec = c.shape[0]
    batch, seq = x_prompt.shape[:2]

    n_cond = n_dec + 1
    pad = (-n_cond) % 8
    cond = jnp.concatenate([c, c_ctx[None, :], jnp.zeros((pad, d), F32)], axis=0)
    mod_all = _modulation(cond, mod_w, mod_b).reshape(depth, n_cond + pad, 3 * N_SUB, d)
    mods_sample = [mod_all[l, :n_dec] for l in range(depth)]
    mods_prompt = [mod_all[l, n_dec:n_dec + 1] for l in range(depth)]

    n_even = ab_w_in.shape[0]
    wts = {
        "norm_pre": norm_pre, "norm_post": norm_post,
        "ffn_w_in": [[ffn_w_in[l, s].astype(BF16) for s in range(2)] for l in range(depth)],
        "ffn_w_out": [[ffn_w_out[l, s].astype(BF16) for s in range(2)] for l in range(depth)],
        "ab_w_in": [_permute_q_cols(ab_w_in[i]).astype(BF16) for i in range(n_even)],
        "ab_w_out_ret": [ab_w_out[i, :RET_VW].astype(BF16) for i in range(n_even)],
        "ab_w_out_att": [_permute_att_rows(ab_w_out[i, RET_VW:]).astype(BF16) for i in range(n_even)],
        "lg_f": jax.nn.log_sigmoid(ret_decay_fwd.astype(F32)),
        "lg_b": jax.nn.log_sigmoid(ret_decay_bwd.astype(F32)),
        "ret_gn": ret_gn, "gqa_q_norm": gqa_q_norm, "gqa_k_norm": gqa_k_norm,
        "na_w_qkv": [na_w_qkv[i].astype(BF16) for i in range(na_w_qkv.shape[0])],
        "na_w_out": [na_w_out[i].astype(BF16) for i in range(na_w_out.shape[0])],
        "na_rpb": na_rpb,
    }

    y_prompt, ctx_aux = _trunk(x_prompt, mods_prompt, None, wts)

    ctx_layers = []
    for l in range(depth):
        i = l // 2
        if l % 2 == 0:
            ctx_layers.append((cache_gqa_k[:, i], cache_gqa_v[:, i], state_ret_fwd[:, i], state_ret_bwd[:, i]))
        else:
            ctx_layers.append((cache_na_k[:, i], cache_na_v[:, i]))
    y_sample, _ = _trunk(x_sample, mods_sample, ctx_layers, wts)

    def stack(idx, layers, tail):
        return jnp.stack([ctx_aux[l][idx].reshape((batch,) + tail) for l in layers], axis=1)

    even = range(0, depth, 2)
    odd = range(1, depth, 2)
    new_gqa_k = stack(0, even, (seq, N_KV, HEAD_DIM))
    new_gqa_v = stack(1, even, (seq, N_KV, HEAD_DIM))
    new_ret_fwd = stack(2, even, (N_RET, RET_DK, RET_DV))
    new_ret_bwd = stack(3, even, (N_RET, RET_DK, RET_DV))
    new_na_k = stack(0, odd, (seq, N_NA, HEAD_DIM))
    new_na_v = stack(1, odd, (seq, N_NA, HEAD_DIM))
    return (y_prompt, y_sample, new_gqa_k, new_gqa_v, new_ret_fwd, new_ret_bwd, new_na_k, new_na_v)
```

```python
import functools

import numpy as np
import jax
import jax.numpy as jnp
from jax import lax
from jax.experimental import pallas as pl
from jax.experimental.pallas import tpu as pltpu

F32 = jnp.float32
BF16 = jnp.bfloat16

GRID_W = 64
N_SUB = 3
HEAD_DIM = 64
N_RET = 8
RET_DK = 64
RET_DV = 64
RET_GN_EPS = 1e-5
N_Q = 8
N_KV = 2
Q_GROUP = N_Q // N_KV
ROPE_THETA = 10000.0
N_NA = 16
NA_KR_MAX = 8
NA_KC = 16
EPS = 1e-6
RET_W = N_RET * RET_DK
RET_VW = N_RET * RET_DV
GQA_W = N_Q * HEAD_DIM
KV_W = N_KV * HEAD_DIM
NA_W = N_NA * HEAD_DIM

LANES = 128
VMEM_LIMIT = 56 * 1024 * 1024
RET_CHUNK = 256
RET_UNROLL = 4
NA_ROWS_PER_STEP = 4
FFN_SUBTILES = 2
NA_UNROLL = 16
GQA_TQ = 256
GQA_KEY_CHUNK = 1024
MASK_VALUE = -1e30
LOG2E = 1.4426950408889634
SCORE_SCALE = HEAD_DIM ** -0.5 * LOG2E


def _cparams(n_grid):
    return pltpu.CompilerParams(dimension_semantics=("parallel",) * n_grid,
                                vmem_limit_bytes=VMEM_LIMIT)


def _const_spec(shape):
    nd = len(shape)
    return pl.BlockSpec(shape, lambda *_: (0,) * nd, pipeline_mode=pl.Buffered(1))


def _silu(x):
    return x * jax.nn.sigmoid(x)


def _rms(x, g):
    ms = jnp.mean(x * x, axis=-1, keepdims=True)
    return x * lax.rsqrt(ms + EPS) * g


def _modulated(x, m, sub, g_pre):
    shift = m[3 * sub:3 * sub + 1]
    scale = m[3 * sub + 1:3 * sub + 2]
    return _rms(x, g_pre) * (1.0 + scale) + shift


def _dot(a, b):
    return jnp.dot(a, b, preferred_element_type=F32)


def _dot_nt(a, b):
    return lax.dot_general(a, b, (((1,), (1,)), ((), ())), preferred_element_type=F32)


def _dot_tn(a, b):
    return lax.dot_general(a, b, (((0,), (0,)), ((), ())), preferred_element_type=F32)


def _low_half(shape):
    return lax.broadcasted_iota(jnp.int32, shape, len(shape) - 1) < HEAD_DIM


def _mod_body(c_ref, w_ref, b_ref, o_ref):
    a = _silu(c_ref[...]).astype(BF16)
    o_ref[0] = _dot(a, w_ref[0].astype(BF16)) + b_ref[0]


def _modulation(cond, mod_w, mod_b):
    depth, d, n = mod_w.shape
    rows = cond.shape[0]
    tn = 1152 if n % 1152 == 0 else n
    return pl.pallas_call(
        _mod_body,
        grid=(depth, n // tn),
        in_specs=[pl.BlockSpec((rows, d), lambda l, j: (0, 0)),
                  pl.BlockSpec((1, d, tn), lambda l, j: (l, 0, j)),
                  pl.BlockSpec((1, 1, tn), lambda l, j: (l, 0, j))],
        out_specs=pl.BlockSpec((1, rows, tn), lambda l, j: (l, 0, j)),
        out_shape=jax.ShapeDtypeStruct((depth, rows, n), F32),
        compiler_params=_cparams(2),
        name="modulation",
    )(cond, mod_w, mod_b.reshape(depth, 1, n))


def _tok_spec(tm, width):
    return pl.BlockSpec((1, tm, width), lambda b, i: (b, i, 0))


def _mod_spec(mod):
    nm, rows, d = mod.shape
    if nm == 1:
        return pl.BlockSpec((1, rows, d), lambda b, i: (0, 0, 0))
    return pl.BlockSpec((1, rows, d), lambda b, i: (b, 0, 0))


def _token_tile(seq_len):
    return 512 if seq_len % 512 == 0 else seq_len


def _ffn_body(*refs, sub, dff, n_seg, n_sub):
    x_ref, mod_ref, gpre_ref, gpost_ref, win_ref, wout_ref = refs[:6]
    rest = refs[6:]
    if n_seg:
        gmix_ref = rest[0]
        seg_refs = rest[1:1 + n_seg]
        wmix_refs = rest[1 + n_seg:1 + 2 * n_seg]
    o_ref = refs[-1]
    m = mod_ref[0]
    tm = x_ref.shape[1]
    ts = tm // n_sub
    rows = [slice(t * ts, (t + 1) * ts) for t in range(n_sub)]

    def residual(t):
        x = x_ref[0, rows[t], :]
        if n_seg:
            y = None
            for s_ref, w_ref in zip(seg_refs, wmix_refs):
                part = _dot(s_ref[0, rows[t], :].astype(BF16), w_ref[...])
                y = part if y is None else y + part
            x = x + m[5:6] * _rms(y, gmix_ref[...])
        return x

    def hidden(t, x):
        h = _modulated(x, m, sub, gpre_ref[...]).astype(BF16)
        gu = _dot(h, win_ref[...])
        return (_silu(gu[:, :dff]) * gu[:, dff:]).astype(BF16)

    def finish(t, x, y):
        gate = m[3 * sub + 2:3 * sub + 3]
        o_ref[0, rows[t], :] = x + (0.5 * gate) * _rms(y, gpost_ref[...])

    xs = [residual(t) for t in range(n_sub)]
    acts = [hidden(t, xs[t]) for t in range(n_sub)]
    ys = [_dot(a, wout_ref[...]) for a in acts]
    for t in range(n_sub):
        finish(t, xs[t], ys[t])


def _ffn(x, mod, sub, g_pre, g_post, w_in, w_out, mixer=None):
    b, seq_len, d = x.shape
    dff = w_out.shape[0]
    tm = _token_tile(seq_len)
    ins = [x, mod, g_pre.reshape(1, d), g_post.reshape(1, d), w_in, w_out]
    in_specs = [_tok_spec(tm, d), _mod_spec(mod), _const_spec((1, d)), _const_spec((1, d)),
                _const_spec(w_in.shape), _const_spec(w_out.shape)]
    n_seg = 0
    if mixer is not None:
        g_mix, segs, ws = mixer
        n_seg = len(segs)
        ins += [g_mix.reshape(1, d)] + list(segs) + list(ws)
        in_specs += ([_const_spec((1, d))] + [_tok_spec(tm, s.shape[-1]) for s in segs]
                     + [_const_spec(w.shape) for w in ws])
    n_sub = FFN_SUBTILES if tm % (8 * FFN_SUBTILES) == 0 else 1
    return pl.pallas_call(
        functools.partial(_ffn_body, sub=sub, dff=dff, n_seg=n_seg, n_sub=n_sub),
        grid=(b, seq_len // tm),
        in_specs=in_specs,
        out_specs=_tok_spec(tm, d),
        out_shape=jax.ShapeDtypeStruct(x.shape, F32),
        compiler_params=_cparams(2),
        name="ffn_half" if mixer is None else "mixer_out_ffn_half",
    )(*ins)


def _head_rms(a, g, bd):
    ss = _dot((a * a).astype(BF16), bd)
    return a * lax.rsqrt(ss * (1.0 / HEAD_DIM) + EPS) * g


def _rope(a, cos, sin):
    quarter = HEAD_DIM // 4
    outs = []
    for j in range(a.shape[1] // LANES):
        aj = a[:, j * LANES:(j + 1) * LANES]
        up = pltpu.roll(aj, LANES - quarter, axis=1)
        dn = pltpu.roll(aj, quarter, axis=1)
        lane = lax.broadcasted_iota(jnp.int32, aj.shape, 1)
        partner = jnp.where((lane % (2 * quarter)) < quarter, up, dn)
        outs.append(aj * cos + partner * sin)
    return outs[0] if len(outs) == 1 else jnp.concatenate(outs, axis=1)


def _ab_in_body(*refs, rope, aux):
    x_ref, mod_ref, gpre_ref, w_ref, qn_ref, kn_ref, bd_ref = refs[:7]
    refs = refs[7:]
    if rope:
        cos_ref, sin_ref = refs[:2]
        refs = refs[2:]
    rq_o, rk_o, rv_o, sg_o, gq_o, gk_o, gv_o = refs[:7]
    h = _modulated(x_ref[0], mod_ref[0], 1, gpre_ref[...]).astype(BF16)
    p = _dot(h, w_ref[...])
    o = 0
    rq_o[0] = p[:, o:o + RET_W].astype(rq_o.dtype)
    o += RET_W
    rk_o[0] = (p[:, o:o + RET_W] * (RET_DK ** -0.5)).astype(rk_o.dtype)
    o += RET_W
    rv_o[0] = p[:, o:o + RET_VW].astype(rv_o.dtype)
    o += RET_VW
    sg_o[0] = _silu(p[:, o:o + RET_VW]).astype(sg_o.dtype)
    o += RET_VW
    gq = p[:, o:o + GQA_W]
    o += GQA_W
    gk = p[:, o:o + KV_W]
    o += KV_W
    gv = p[:, o:o + KV_W]
    bd = bd_ref[...]
    qh = _head_rms(gq, qn_ref[...], bd)
    kh = _head_rms(gk, kn_ref[...], bd[:KV_W, :KV_W])
    if rope:
        qh = _rope(qh, cos_ref[...], sin_ref[...])
        kh = _rope(kh, cos_ref[...], sin_ref[...])
    gq_o[0] = (qh * SCORE_SCALE).astype(gq_o.dtype)
    gk_o[0] = kh.astype(gk_o.dtype)
    gv_o[0] = gv.astype(gv_o.dtype)
    if aux:
        k5_o, v5_o = refs[7:]
        k5_o[...] = kh.reshape(k5_o.shape)
        v5_o[...] = gv.reshape(v5_o.shape)


def _ab_in(x, mod, g_pre, w, qn, kn, rope_tables, aux):
    b, seq_len, d = x.shape
    tm = _token_tile(seq_len)
    rope = rope_tables is not None
    bd = jnp.asarray(np.kron(np.eye(N_Q), np.ones((HEAD_DIM, HEAD_DIM))), BF16)
    ins = [x, mod, g_pre.reshape(1, d), w, jnp.tile(qn, N_Q).reshape(1, GQA_W),
           jnp.tile(kn, N_KV).reshape(1, KV_W), bd]
    in_specs = [_tok_spec(tm, d), _mod_spec(mod), _const_spec((1, d)), _const_spec(w.shape),
                _const_spec((1, GQA_W)), _const_spec((1, KV_W)), _const_spec(bd.shape)]
    if rope:
        ins += list(rope_tables)
        in_specs += [pl.BlockSpec((tm, LANES), lambda b, i: (i, 0))] * 2
    widths = (RET_W, RET_W, RET_VW, RET_VW, GQA_W, KV_W, KV_W)
    out_specs = [_tok_spec(tm, wd) for wd in widths]
    out_shape = [jax.ShapeDtypeStruct((b, seq_len, wd), BF16) for wd in widths]
    if aux:
        assert b == 1
        out_specs += [pl.BlockSpec((tm, N_KV, HEAD_DIM), lambda b, i: (i, 0, 0))] * 2
        out_shape += [jax.ShapeDtypeStruct((seq_len, N_KV, HEAD_DIM), F32)] * 2
    return pl.pallas_call(
        functools.partial(_ab_in_body, rope=rope, aux=aux),
        grid=(b, seq_len // tm),
        in_specs=in_specs,
        out_specs=out_specs,
        out_shape=out_shape,
        compiler_params=_cparams(2),
        name="ab_in_proj",
    )(*ins)


def _rope_tables(seq_len):
    t = jnp.arange(seq_len)
    quarter = HEAD_DIM // 4
    inv = ROPE_THETA ** (-jnp.arange(quarter, dtype=F32) / quarter)
    ang_r = (t // GRID_W).astype(F32)[:, None] * inv[None, :]
    ang_c = (t % GRID_W).astype(F32)[:, None] * inv[None, :]
    cos = jnp.concatenate([jnp.cos(ang_r)] * 2 + [jnp.cos(ang_c)] * 2, axis=1)
    sin = jnp.concatenate([-jnp.sin(ang_r), jnp.sin(ang_r), -jnp.sin(ang_c), jnp.sin(ang_c)], axis=1)
    return jnp.tile(cos, (1, LANES // HEAD_DIM)), jnp.tile(sin, (1, LANES // HEAD_DIM))


def _ret_body(lgf_ref, lgb_ref, rq_ref, rk_ref, rv_ref, sg_ref, gn_ref, sf_ref, sb_ref,
              y_ref, ff_ref, fb_ref, sbs_ref, *, chunk, n_chunks, pairs_per_step):
    for pp in range(pairs_per_step):
        lanes = slice(pp * LANES, (pp + 1) * LANES)
        heads = slice(2 * pp, 2 * pp + 2)
        _ret_pair(pl.program_id(1) * pairs_per_step + pp, lgf_ref, lgb_ref,
                  rq_ref.at[:, :, lanes], rk_ref.at[:, :, lanes], rv_ref.at[:, :, lanes],
                  sg_ref.at[:, :, lanes], gn_ref.at[:, lanes], sf_ref.at[:, pp:pp + 1],
                  sb_ref.at[:, pp:pp + 1], y_ref.at[:, :, lanes], ff_ref.at[:, heads],
                  fb_ref.at[:, heads], sbs_ref, chunk=chunk, n_chunks=n_chunks)


def _ret_pair(pair, lgf_ref, lgb_ref, rq_ref, rk_ref, rv_ref, sg_ref, gn_ref, sf_ref, sb_ref,
              y_ref, ff_ref, fb_ref, sbs_ref, *, chunk, n_chunks):
    c = chunk
    lo_row = _low_half((1, LANES))
    lgf = jnp.where(lo_row, lgf_ref[2 * pair], lgf_ref[2 * pair + 1])
    lgb = jnp.where(lo_row, lgb_ref[2 * pair], lgb_ref[2 * pair + 1])
    idx = lax.broadcasted_iota(jnp.int32, (c, 1), 0).astype(F32)
    qdec_f = jnp.exp(lgf * (idx + 1.0))
    kdec_f = jnp.exp(lgf * (c - 1.0 - idx))
    cdec_f = jnp.exp(lgf * float(c))
    qdec_b = jnp.exp(lgb * (c - idx))
    kdec_b = jnp.exp(lgb * idx)
    cdec_b = jnp.exp(lgb * float(c))
    diff = (lax.broadcasted_iota(jnp.int32, (c, c), 0)
            - lax.broadcasted_iota(jnp.int32, (c, c), 1)).astype(F32)
    dmat = [jnp.exp(jnp.where(diff >= 0, lgf_ref[2 * pair + h], -lgb_ref[2 * pair + h]) * diff)
            for h in range(2)]
    rr = lax.broadcasted_iota(jnp.int32, (LANES, LANES), 0) < HEAD_DIM
    cc = lax.broadcasted_iota(jnp.int32, (LANES, LANES), 1) < HEAD_DIM
    same_head = rr == cc
    lo = _low_half((c, LANES))

    def load(ref, n):
        return ref[0, pl.ds(pl.multiple_of(n * c, c), c), :]

    def bwd_step(t, state):
        n = n_chunks - 1 - t
        sbs_ref[n] = state
        k = load(rk_ref, n).astype(F32)
        v = load(rv_ref, n).astype(BF16)
        kv = _dot_tn((k * kdec_b).astype(BF16), v)
        return state * cdec_b + jnp.where(same_head, kv, 0.0)

    unroll = int(np.gcd(n_chunks, RET_UNROLL))
    _store_head_states(fb_ref, lax.fori_loop(0, n_chunks, bwd_step, sb_ref[0, 0], unroll=unroll))

    gn = gn_ref[...]

    def head_mean(a):
        m0 = jnp.sum(jnp.where(lo, a, 0.0), axis=-1, keepdims=True)
        m1 = jnp.sum(jnp.where(lo, 0.0, a), axis=-1, keepdims=True)
        return jnp.where(lo, m0, m1) * (1.0 / RET_DV)

    def fwd_trip(t, state0):
        carried = {"state": state0}

        def scores(u):
            n = t * unroll + u
            q = load(rq_ref, n).astype(BF16)
            k = load(rk_ref, n).astype(BF16)
            v = load(rv_ref, n).astype(BF16)
            kv = jnp.where(same_head, _dot_tn((k.astype(F32) * kdec_f).astype(BF16), v), 0.0)
            return n, q, v, kv, [_dot_nt(_only_head(q, h), k) for h in range(2)]

        def probs(u, sc):
            n, q, v, kv, s = sc
            return n, q, v, kv, [(s[h] * dmat[h]).astype(BF16) for h in range(2)]

        def accumulate(u, pr):
            n, q, v, kv, p = pr
            state = carried["state"]
            qf = q.astype(F32)
            o = (_dot((qf * qdec_f).astype(BF16), state.astype(BF16))
                 + _dot((qf * qdec_b).astype(BF16), sbs_ref[n].astype(BF16)))
            o = o + jnp.where(lo, _dot(p[0], v), _dot(p[1], v))
            dev = o - head_mean(o)
            var = head_mean(dev * dev)
            y = dev * lax.rsqrt(var + RET_GN_EPS) * gn * load(sg_ref, n).astype(F32)
            y_ref[0, pl.ds(pl.multiple_of(n * c, c), c), :] = y.astype(y_ref.dtype)
            carried["state"] = state * cdec_f + kv

        _staged(list(range(unroll)), scores, probs, accumulate)
        return carried["state"]

    _store_head_states(ff_ref, lax.fori_loop(0, n_chunks // unroll, fwd_trip, sf_ref[0, 0]))


def _store_head_states(ref, state):
    ref[0, 0] = state[:RET_DK, :RET_DV]
    ref[0, 1] = state[RET_DK:, RET_DV:]


def _pair_states(s):
    b, nh, dk, dv = s.shape
    s = s.reshape(b, nh // 2, 2, dk, dv)
    z = jnp.zeros_like(s[:, :, 0])
    top = jnp.concatenate([s[:, :, 0], z], axis=-1)
    bot = jnp.concatenate([z, s[:, :, 1]], axis=-1)
    return jnp.concatenate([top, bot], axis=-2)


def _retention(rq, rk, rv, sg, gn, lg_f, lg_b, s_f, s_b):
    b, seq_len, _ = rq.shape
    c = RET_CHUNK if seq_len % RET_CHUNK == 0 else seq_len
    nc = seq_len // c
    npair = N_RET // 2
    pps = npair if nc == 1 else 1
    seq_spec = pl.BlockSpec((1, seq_len, pps * LANES), lambda i, p: (i, 0, p))
    st_spec = pl.BlockSpec((1, pps, LANES, LANES), lambda i, p: (i, p, 0, 0))
    fin_spec = pl.BlockSpec((1, 2 * pps, RET_DK, RET_DV), lambda i, p: (i, p, 0, 0))
    smem = pl.BlockSpec(memory_space=pltpu.SMEM)
    st_shape = jax.ShapeDtypeStruct((b, N_RET, RET_DK, RET_DV), F32)
    return pl.pallas_call(
        functools.partial(_ret_body, chunk=c, n_chunks=nc, pairs_per_step=pps),
        grid=(b, npair // pps),
        in_specs=[smem, smem, seq_spec, seq_spec, seq_spec, seq_spec,
                  pl.BlockSpec((1, pps * LANES), lambda i, p: (0, p)), st_spec, st_spec],
        out_specs=[seq_spec, fin_spec, fin_spec],
        out_shape=[jax.ShapeDtypeStruct((b, seq_len, RET_VW), BF16), st_shape, st_shape],
        scratch_shapes=[pltpu.VMEM((nc, LANES, LANES), F32)],
        compiler_params=_cparams(2),
        name="retention",
    )(lg_f, lg_b, rq, rk, rv, sg, gn.reshape(1, RET_VW), _pair_states(s_f), _pair_states(s_b))


def _staged(items, scores, probs, accumulate, ahead=2):
    n = len(items)
    queue = [scores(items[i]) for i in range(min(ahead, n))]
    pending = None
    for i in range(n):
        sc = queue.pop(0)
        if i + ahead < n:
            queue.append(scores(items[i + ahead]))
        ps = probs(items[i], sc)
        if pending is not None:
            accumulate(*pending)
        pending = (items[i], ps)
    accumulate(*pending)


def _row_probs(s_list):
    m = s_list[0].max(axis=-1, keepdims=True)
    for s in s_list[1:]:
        m = jnp.maximum(m, s.max(axis=-1, keepdims=True))
    ps = [jnp.exp2(s - m) for s in s_list]
    den = ps[0].sum(axis=-1, keepdims=True)
    for p in ps[1:]:
        den = den + p.sum(axis=-1, keepdims=True)
    return [p.astype(BF16) for p in ps], 1.0 / den


def _only_head(q, h):
    lo = _low_half(q.shape)
    zero = jnp.zeros_like(q)
    return jnp.where(lo, q, zero) if h == 0 else jnp.where(lo, zero, q)


def _attn_body(q_ref, k_ref, v_ref, o_ref, *, n_groups, kv_shared):
    tq = q_ref.shape[1]
    lo = _low_half((tq, LANES))
    outs = {}

    def lanes(j):
        return slice(j * LANES, (j + 1) * LANES)

    def scores(item):
        j, h = item
        k = k_ref[0] if kv_shared else k_ref[0, :, lanes(j)]
        return _dot_nt(_only_head(q_ref[0, :, lanes(j)].astype(BF16), h), k.astype(BF16))

    def probs(item, s):
        return _row_probs([s])

    def accumulate(item, pr):
        j, h = item
        ps, inv_den = pr
        v = v_ref[0] if kv_shared else v_ref[0, :, lanes(j)]
        outs[h] = _dot(ps[0], v.astype(BF16)) * inv_den
        if h == 1:
            o_ref[0, :, lanes(j)] = jnp.where(lo, outs[0], outs[1]).astype(o_ref.dtype)

    _staged([(j, h) for j in range(n_groups) for h in range(2)], scores, probs, accumulate)


def _col_max(x, slab=64):
    n, t = x.shape
    if n > slab and n % slab == 0:
        x = x.reshape(n // slab, slab, t).max(axis=0)
    return x.max(axis=0, keepdims=True)


def _gqa_t_body(q_ref, kn_ref, kc_ref, vtn_ref, vtc_ref, o_ref, *, n_groups, chunk):
    tq = q_ref.shape[1]
    top = lax.broadcasted_iota(jnp.int32, (LANES, tq), 0) < HEAD_DIM
    ln, lc = kn_ref.shape[1], kc_ref.shape[1]
    chunks = [(kn_ref, vtn_ref, c * chunk, chunk) for c in range(ln // chunk)] + [(kc_ref, vtc_ref, 0, lc)]
    items = [(j, h, c) for j in range(n_groups) for h in range(2) for c in range(len(chunks))]

    def scores(item):
        j, h, c = item
        k_ref, _, off, size = chunks[c]
        qh = _only_head(q_ref[0, :, j * LANES:(j + 1) * LANES], h)
        return _dot_nt(k_ref[0, off:off + size, :], qh)

    outs = {}
    state = {"m": None, "acc": None}

    def probs(item, st):
        c = item[2]
        mc = _col_max(st)
        m_new = mc if c == 0 else jnp.maximum(state["m"], mc)
        alpha = None if c == 0 else jnp.exp2(state["m"] - m_new)
        state["m"] = m_new
        return jnp.exp2(st - m_new).astype(BF16), alpha

    def accumulate(item, ps):
        j, h, c = item
        p, alpha = ps
        _, vt_ref, off, size = chunks[c]
        o = _dot(vt_ref[0, h, :, off:off + size], p)
        acc = o if c == 0 else state["acc"] * alpha + o
        state["acc"] = acc
        if c == len(chunks) - 1:
            den = acc[HEAD_DIM:HEAD_DIM + 1] if h == 0 else acc[0:1]
            outs[h] = acc * (1.0 / den)
            if h == 1:
                sl = slice(j * LANES, (j + 1) * LANES)
                o_ref[0, :, sl] = jnp.where(top, outs[0], outs[1]).T.astype(o_ref.dtype)

    _staged(items, scores, probs, accumulate)


def _vt_with_ones(v):
    vt = jnp.swapaxes(v, 1, 2)
    top = (np.arange(KV_W) < HEAD_DIM)[None, :, None]
    one = jnp.ones_like(vt)
    return jnp.stack([jnp.where(top, vt, one), jnp.where(top, one, vt)], axis=1)


def _gqa_attention_t(q, k_new, v_new, k_ctx, v_ctx, tq):
    b, lq, wq = q.shape
    ln, lc = k_new.shape[1], k_ctx.shape[1]
    chunk = GQA_KEY_CHUNK if ln % GQA_KEY_CHUNK == 0 else ln
    whole3 = lambda i, t: (i, 0, 0)
    whole4 = lambda i, t: (i, 0, 0, 0)
    return pl.pallas_call(
        functools.partial(_gqa_t_body, n_groups=wq // LANES, chunk=chunk),
        grid=(b, lq // tq),
        in_specs=[pl.BlockSpec((1, tq, wq), lambda i, t: (i, t, 0)),
                  pl.BlockSpec((1, ln, KV_W), whole3), pl.BlockSpec((1, lc, KV_W), whole3),
                  pl.BlockSpec((1, 2, KV_W, ln), whole4), pl.BlockSpec((1, 2, KV_W, lc), whole4)],
        out_specs=pl.BlockSpec((1, tq, wq), lambda i, t: (i, t, 0)),
        out_shape=jax.ShapeDtypeStruct((b, lq, wq), BF16),
        compiler_params=_cparams(2),
        name="gqa_attention_t",
    )(q, k_new, k_ctx, _vt_with_ones(v_new), _vt_with_ones(v_ctx))


def _attention(q, k, v, kv_shared, tq):
    b, lq, wq = q.shape
    lk, wk = k.shape[1:]
    return pl.pallas_call(
        functools.partial(_attn_body, n_groups=wq // LANES, kv_shared=kv_shared),
        grid=(b, lq // tq),
        in_specs=[pl.BlockSpec((1, tq, wq), lambda i, t: (i, t, 0)),
                  pl.BlockSpec((1, lk, wk), lambda i, t: (i, 0, 0)),
                  pl.BlockSpec((1, lk, wk), lambda i, t: (i, 0, 0))],
        out_specs=pl.BlockSpec((1, tq, wq), lambda i, t: (i, t, 0)),
        out_shape=jax.ShapeDtypeStruct((b, lq, wq), BF16),
        compiler_params=_cparams(2),
        name="dense_attention",
    )(q, k, v)


def _na_plan(rows):
    g = NA_ROWS_PER_STEP
    kr = min(NA_KR_MAX, rows)
    u = min(-(-(g + kr - 1) // 2) * 2, rows)
    n_steps = rows // g
    bases, vids, variants, keys = [], [], [], {}
    for s in range(n_steps):
        base = int(np.clip(s * g - kr // 2, 0, rows - u)) // 2 * 2
        r = s * g + np.arange(g)[:, None]
        krow = base + np.arange(u)[None, :]
        r0 = np.clip(r - kr // 2, 0, rows - kr)
        valid = (krow >= r0) & (krow < r0 + kr)
        assert (valid.sum(axis=1) == kr).all(), "key window not covered by the step's row range"
        dr = np.where(valid, krow - r + (NA_KR_MAX - 1), 0)
        key = (dr.tobytes(), valid.tobytes())
        if key not in keys:
            keys[key] = len(variants)
            variants.append((dr, valid))
        bases.append(base)
        vids.append(keys[key])
    return u, bases, vids, variants


def _na_bias_tables(rpb, variants, u):
    g = NA_ROWS_PER_STEP
    nh, n_dr, n_dc = rpb.shape
    cols = np.arange(GRID_W)
    c0 = np.clip(cols - NA_KC // 2, 0, GRID_W - NA_KC)
    kc = np.arange(GRID_W)[None, :]
    cvalid = (kc >= c0[:, None]) & (kc < c0[:, None] + NA_KC)
    dc = kc - cols[:, None] + (NA_KC - 1)
    sel_c = cvalid[None] & (dc[None] == np.arange(n_dc)[:, None, None])
    band = jnp.einsum("hde,eck->hdkc", rpb.astype(F32), jnp.asarray(sel_c, F32),
                      precision=lax.Precision.HIGHEST)
    band = jnp.where(cvalid.T[None, None], band * LOG2E, MASK_VALUE)
    masked = jnp.full((nh, GRID_W, GRID_W), MASK_VALUE, F32)
    tables = []
    for dr, rvalid in variants:
        key_rows = [jnp.stack([band[:, dr[q, a]] if rvalid[q, a] else masked for q in range(g)], axis=2)
                    for a in range(u)]
        tables.append(jnp.stack(key_rows, axis=1).reshape(nh, u * GRID_W, g * GRID_W))
    return jnp.stack(tables)


def _na_body(base_ref, vid_ref, q_ref, k_ref, vt_ref, kc_ref, vct_ref, bias_ref, o_ref, *, n_steps, tq, tk):
    top = lax.broadcasted_iota(jnp.int32, (LANES, tq), 0) < HEAD_DIM
    kc = kc_ref[0]
    lc = kc.shape[0]
    ones_rows = 8
    vct = jnp.concatenate([vct_ref[0], jnp.ones((ones_rows, lc), BF16)], axis=0)
    ones_win = jnp.ones((ones_rows, tk), BF16)
    unroll = int(np.gcd(n_steps, NA_UNROLL))

    def trip(t, carry):
        items = [(u, h) for u in range(unroll) for h in range(2)]
        steps = [t * unroll + u for u in range(unroll)]
        outs = {}

        def q_rows(u):
            return pl.ds(pl.multiple_of(steps[u] * tq, tq), tq)

        def k_rows(u):
            return pl.ds(pl.multiple_of(base_ref[steps[u]] * GRID_W, LANES), tk)

        def scores(item):
            u, h = item
            qh = _only_head(q_ref[0, q_rows(u), :], h)
            st_win = _dot_nt(k_ref[0, k_rows(u), :], qh) + bias_ref[vid_ref[steps[u]], h]
            return st_win, _dot_nt(kc, qh)

        def probs(item, sc):
            m = jnp.maximum(_col_max(sc[0]), _col_max(sc[1]))
            return [jnp.exp2(s - m).astype(BF16) for s in sc]

        def accumulate(item, ps):
            u, h = item
            vt = jnp.concatenate([vt_ref[0, :, k_rows(u)], ones_win], axis=0)
            acc = _dot(vt, ps[0]) + _dot(vct, ps[1])
            outs[h] = acc[:LANES] * (1.0 / acc[LANES:LANES + 1])
            if h == 1:
                o_ref[0, q_rows(u), :] = jnp.where(top, outs[0], outs[1]).T.astype(o_ref.dtype)

        _staged(items, scores, probs, accumulate)
        return carry

    lax.fori_loop(0, n_steps // unroll, trip, 0)


def _na_attention(q, k, vt, k_ctx, v_ctx, rpb):
    b, seq_len, w = q.shape
    rows = seq_len // GRID_W
    lc = k_ctx.shape[1]
    u, bases, vids, variants = _na_plan(rows)
    bias = _na_bias_tables(rpb, variants, u)
    nv = bias.shape[0]
    tq = NA_ROWS_PER_STEP * GRID_W
    tk = u * GRID_W
    n_steps = rows // NA_ROWS_PER_STEP
    smem = pl.BlockSpec(memory_space=pltpu.SMEM)
    seq_spec = pl.BlockSpec((1, seq_len, LANES), lambda p, i: (i, 0, p))
    return pl.pallas_call(
        functools.partial(_na_body, n_steps=n_steps, tq=tq, tk=tk),
        grid=(w // LANES, b),
        in_specs=[smem, smem, seq_spec, seq_spec,
                  pl.BlockSpec((1, LANES, seq_len), lambda p, i: (i, p, 0)),
                  pl.BlockSpec((1, lc, LANES), lambda p, i: (i, 0, p)),
                  pl.BlockSpec((1, LANES, lc), lambda p, i: (i, p, 0)),
                  pl.BlockSpec((nv, 2, tk, tq), lambda p, i: (0, p, 0, 0))],
        out_specs=seq_spec,
        out_shape=jax.ShapeDtypeStruct((b, seq_len, w), BF16),
        compiler_params=_cparams(2),
        name="neighbourhood_attention",
    )(jnp.asarray(bases, jnp.int32), jnp.asarray(vids, jnp.int32), q, k, vt, k_ctx,
      jnp.swapaxes(v_ctx, 1, 2), bias)


def _na_in_body(x_ref, mod_ref, gpre_ref, w_ref, q_o, k_o, v_o, *aux_o):
    h = _modulated(x_ref[0], mod_ref[0], 1, gpre_ref[...]).astype(BF16)
    p = _dot(h, w_ref[...])
    q_o[0] = (p[:, :NA_W] * SCORE_SCALE).astype(q_o.dtype)
    k = p[:, NA_W:2 * NA_W]
    v = p[:, 2 * NA_W:]
    k_o[0] = k.astype(k_o.dtype)
    v_o[0] = (v if aux_o else v.T).astype(v_o.dtype)
    if aux_o:
        k5_o, v5_o = aux_o
        k5_o[...] = k.reshape(k5_o.shape)
        v5_o[...] = v.reshape(v5_o.shape)


def _na_in(x, mod, g_pre, w, aux):
    b, seq_len, d = x.shape
    tm = _token_tile(seq_len)
    out_specs = [_tok_spec(tm, NA_W)] * 3
    out_shape = [jax.ShapeDtypeStruct((b, seq_len, NA_W), BF16)] * 3
    if aux:
        assert b == 1
        out_specs += [pl.BlockSpec((tm, N_NA, HEAD_DIM), lambda b, i: (i, 0, 0))] * 2
        out_shape += [jax.ShapeDtypeStruct((seq_len, N_NA, HEAD_DIM), F32)] * 2
    else:
        out_specs[2] = pl.BlockSpec((1, NA_W, tm), lambda b, i: (b, 0, i))
        out_shape[2] = jax.ShapeDtypeStruct((b, NA_W, seq_len), BF16)
    return pl.pallas_call(
        _na_in_body,
        grid=(b, seq_len // tm),
        in_specs=[_tok_spec(tm, d), _mod_spec(mod), _const_spec((1, d)), _const_spec(w.shape)],
        out_specs=out_specs,
        out_shape=out_shape,
        compiler_params=_cparams(2),
        name="na_in_proj",
    )(x, mod, g_pre.reshape(1, d), w)


_Q_HEAD_ORDER = [h for j in range(Q_GROUP) for h in range(j, N_Q, Q_GROUP)]


def _permute_q_cols(w_in):
    q0 = 2 * RET_W + 2 * RET_VW
    heads = [w_in[:, q0 + h * HEAD_DIM:q0 + (h + 1) * HEAD_DIM] for h in _Q_HEAD_ORDER]
    return jnp.concatenate([w_in[:, :q0]] + heads + [w_in[:, q0 + GQA_W:]], axis=1)


def _permute_att_rows(w_att):
    return jnp.concatenate([w_att[h * HEAD_DIM:(h + 1) * HEAD_DIM] for h in _Q_HEAD_ORDER], axis=0)


def _trunk(x, mods, ctx_layers, wts):
    b, seq_len, d = x.shape
    is_ctx = ctx_layers is None
    depth = len(mods)
    flat = (lambda a: a.reshape(1, b * seq_len, a.shape[-1])) if is_ctx else (lambda a: a)
    unflat = (lambda a: a.reshape(b, seq_len, a.shape[-1])) if is_ctx else (lambda a: a)
    aux = []
    x = flat(x)
    for l in range(depth):
        mod = mods[l]
        i = l // 2
        x = _ffn(x, mod, 0, wts["norm_pre"][l, 0], wts["norm_post"][l, 0],
                 wts["ffn_w_in"][l][0], wts["ffn_w_out"][l][0])
        if l % 2 == 0:
            rope_tables = None if is_ctx else _rope_tables(seq_len)
            proj = _ab_in(x, mod, wts["norm_pre"][l, 1], wts["ab_w_in"][i], wts["gqa_q_norm"][i],
                          wts["gqa_k_norm"][i], rope_tables, is_ctx)
            rq, rk, rv, sg, gq, gk, gv = [unflat(a) for a in proj[:7]]
            if is_ctx:
                s_f = jnp.zeros((b, N_RET, RET_DK, RET_DV), F32)
                s_b = s_f
            else:
                ck, cv, s_f, s_b = ctx_layers[l]
                lc = ck.shape[1]
            y_ret, fin_f, fin_b = _retention(rq, rk, rv, sg, wts["ret_gn"][i], wts["lg_f"][i],
                                             wts["lg_b"][i], s_f.astype(F32), s_b.astype(F32))
            if is_ctx:
                y_att = _attention(gq, gk, gv, True, seq_len)
            else:
                y_att = _gqa_attention_t(gq, gk, gv, ck.reshape(b, lc, KV_W).astype(BF16),
                                         cv.reshape(b, lc, KV_W).astype(BF16), GQA_TQ)
            mixer = (wts["norm_post"][l, 1], [flat(y_ret), flat(y_att)],
                     [wts["ab_w_out_ret"][i], wts["ab_w_out_att"][i]])
            aux.append((proj[7], proj[8], fin_f, fin_b) if is_ctx else None)
        else:
            proj = _na_in(x, mod, wts["norm_pre"][l, 1], wts["na_w_qkv"][i], is_ctx)
            q, k, v = [unflat(a) for a in proj[:3]]
            if is_ctx:
                y = _attention(q, k, v, False, seq_len)
                aux.append((proj[3], proj[4]))
            else:
                ck, cv = ctx_layers[l]
                lc = ck.shape[1]
                y = _na_attention(q, k, v, ck.reshape(b, lc, NA_W).astype(BF16),
                                  cv.reshape(b, lc, NA_W).astype(BF16), wts["na_rpb"][i])
                aux.append(None)
            mixer = (wts["norm_post"][l, 1], [flat(y)], [wts["na_w_out"][i]])
        x = _ffn(x, mod, 2, wts["norm_pre"][l, 2], wts["norm_post"][l, 2],
                 wts["ffn_w_in"][l][1], wts["ffn_w_out"][l][1], mixer=mixer)
    return unflat(x), aux


def kernel(x_prompt, x_sample, cache_gqa_k, cache_gqa_v, state_ret_fwd, state_ret_bwd, cache_na_k,
           cache_na_v, c, c_ctx, mod_w, mod_b, norm_pre, norm_post, ffn_w_in, ffn_w_out, ab_w_in,
           ab_w_out, ret_decay_fwd, ret_decay_bwd, ret_gn, gqa_q_norm, gqa_k_norm, na_w_qkv, na_w_out,
           na_rpb):
    depth = mod_w.shape[0]
    d = x_prompt.shape[-1]
    n_dec = c.shape[0]
    batch, seq = x_prompt.shape[:2]

    n_cond = n_dec + 1
    pad = (-n_cond) % 8
    cond = jnp.concatenate([c, c_ctx[None, :], jnp.zeros((pad, d), F32)], axis=0)
    mod_all = _modulation(cond, mod_w, mod_b).reshape(depth, n_cond + pad, 3 * N_SUB, d)
    mods_sample = [mod_all[l, :n_dec] for l in range(depth)]
    mods_prompt = [mod_all[l, n_dec:n_dec + 1] for l in range(depth)]

    n_even = ab_w_in.shape[0]
    wts = {
        "norm_pre": norm_pre, "norm_post": norm_post,
        "ffn_w_in": [[ffn_w_in[l, s].astype(BF16) for s in range(2)] for l in range(depth)],
        "ffn_w_out": [[ffn_w_out[l, s].astype(BF16) for s in range(2)] for l in range(depth)],
        "ab_w_in": [_permute_q_cols(ab_w_in[i]).astype(BF16) for i in range(n_even)],
        "ab_w_out_ret": [ab_w_out[i, :RET_VW].astype(BF16) for i in range(n_even)],
        "ab_w_out_att": [_permute_att_rows(ab_w_out[i, RET_VW:]).astype(BF16) for i in range(n_even)],
        "lg_f": jax.nn.log_sigmoid(ret_decay_fwd.astype(F32)),
        "lg_b": jax.nn.log_sigmoid(ret_decay_bwd.astype(F32)),
        "ret_gn": ret_gn, "gqa_q_norm": gqa_q_norm, "gqa_k_norm": gqa_k_norm,
        "na_w_qkv": [na_w_qkv[i].astype(BF16) for i in range(na_w_qkv.shape[0])],
        "na_w_out": [na_w_out[i].astype(BF16) for i in range(na_w_out.shape[0])],
        "na_rpb": na_rpb,
    }

    y_prompt, ctx_aux = _trunk(x_prompt, mods_prompt, None, wts)

    ctx_layers = []
    for l in range(depth):
        i = l // 2
        if l % 2 == 0:
            ctx_layers.append((cache_gqa_k[:, i], cache_gqa_v[:, i], state_ret_fwd[:, i], state_ret_bwd[:, i]))
        else:
            ctx_layers.append((cache_na_k[:, i], cache_na_v[:, i]))
    y_sample, _ = _trunk(x_sample, mods_sample, ctx_layers, wts)

    def stack(idx, layers, tail):
        return jnp.stack([ctx_aux[l][idx].reshape((batch,) + tail) for l in layers], axis=1)

    even = range(0, depth, 2)
    odd = range(1, depth, 2)
    new_gqa_k = stack(0, even, (seq, N_KV, HEAD_DIM))
    new_gqa_v = stack(1, even, (seq, N_KV, HEAD_DIM))
    new_ret_fwd = stack(2, even, (N_RET, RET_DK, RET_DV))
    new_ret_bwd = stack(3, even, (N_RET, RET_DK, RET_DV))
    new_na_k = stack(0, odd, (seq, N_NA, HEAD_DIM))
    new_na_v = stack(1, odd, (seq, N_NA, HEAD_DIM))
    return (y_prompt, y_sample, new_gqa_k, new_gqa_v, new_ret_fwd, new_ret_bwd, new_na_k, new_na_v)
```

```python
import functools

import numpy as np
import jax
import jax.numpy as jnp
from jax import lax
from jax.experimental import pallas as pl
from jax.experimental.pallas import tpu as pltpu

F32 = jnp.float32
BF16 = jnp.bfloat16

GRID_W = 64
N_SUB = 3
HEAD_DIM = 64
N_RET = 8
RET_DK = 64
RET_DV = 64
RET_GN_EPS = 1e-5
N_Q = 8
N_KV = 2
Q_GROUP = N_Q // N_KV
ROPE_THETA = 10000.0
N_NA = 16
NA_KR_MAX = 8
NA_KC = 16
EPS = 1e-6
RET_W = N_RET * RET_DK
RET_VW = N_RET * RET_DV
GQA_W = N_Q * HEAD_DIM
KV_W = N_KV * HEAD_DIM
NA_W = N_NA * HEAD_DIM

LANES = 128
VMEM_LIMIT = 56 * 1024 * 1024
RET_CHUNK = 256
RET_UNROLL = 16
NA_ROWS_PER_STEP = 4
FFN_SUBTILES = 2
NA_UNROLL = 16
GQA_TQ = 256
GQA_KEY_CHUNK = 1024
MASK_VALUE = -1e30
LOG2E = 1.4426950408889634
SCORE_SCALE = HEAD_DIM ** -0.5 * LOG2E


def _cparams(n_grid):
    return pltpu.CompilerParams(dimension_semantics=("parallel",) * n_grid,
                                vmem_limit_bytes=VMEM_LIMIT)


def _const_spec(shape):
    nd = len(shape)
    return pl.BlockSpec(shape, lambda *_: (0,) * nd, pipeline_mode=pl.Buffered(1))


def _silu(x):
    return x * jax.nn.sigmoid(x)


def _rms(x, g):
    ms = jnp.mean(x * x, axis=-1, keepdims=True)
    return x * lax.rsqrt(ms + EPS) * g


def _modulated(x, m, sub, g_pre):
    shift = m[3 * sub:3 * sub + 1]
    scale = m[3 * sub + 1:3 * sub + 2]
    return _rms(x, g_pre) * (1.0 + scale) + shift


def _dot(a, b):
    return jnp.dot(a, b, preferred_element_type=F32)


def _dot_nt(a, b):
    return lax.dot_general(a, b, (((1,), (1,)), ((), ())), preferred_element_type=F32)


def _dot_tn(a, b):
    return lax.dot_general(a, b, (((0,), (0,)), ((), ())), preferred_element_type=F32)


def _low_half(shape):
    return lax.broadcasted_iota(jnp.int32, shape, len(shape) - 1) < HEAD_DIM


def _mod_body(c_ref, w_ref, b_ref, o_ref):
    a = _silu(c_ref[...]).astype(BF16)
    o_ref[0] = _dot(a, w_ref[0].astype(BF16)) + b_ref[0]


def _modulation(cond, mod_w, mod_b):
    depth, d, n = mod_w.shape
    rows = cond.shape[0]
    tn = 1152 if n % 1152 == 0 else n
    return pl.pallas_call(
        _mod_body,
        grid=(depth, n // tn),
        in_specs=[pl.BlockSpec((rows, d), lambda l, j: (0, 0)),
                  pl.BlockSpec((1, d, tn), lambda l, j: (l, 0, j)),
                  pl.BlockSpec((1, 1, tn), lambda l, j: (l, 0, j))],
        out_specs=pl.BlockSpec((1, rows, tn), lambda l, j: (l, 0, j)),
        out_shape=jax.ShapeDtypeStruct((depth, rows, n), F32),
        compiler_params=_cparams(2),
        name="modulation",
    )(cond, mod_w, mod_b.reshape(depth, 1, n))


def _tok_spec(tm, width):
    return pl.BlockSpec((1, tm, width), lambda b, i: (b, i, 0))


def _mod_spec(mod):
    nm, rows, d = mod.shape
    if nm == 1:
        return pl.BlockSpec((1, rows, d), lambda b, i: (0, 0, 0))
    return pl.BlockSpec((1, rows, d), lambda b, i: (b, 0, 0))


def _token_tile(seq_len):
    return 512 if seq_len % 512 == 0 else seq_len


def _ffn_body(*refs, sub, dff, n_seg, n_sub):
    x_ref, mod_ref, gpre_ref, gpost_ref, win_ref, wout_ref = refs[:6]
    rest = refs[6:]
    if n_seg:
        gmix_ref = rest[0]
        seg_refs = rest[1:1 + n_seg]
        wmix_refs = rest[1 + n_seg:1 + 2 * n_seg]
    o_ref = refs[-1]
    m = mod_ref[0]
    tm = x_ref.shape[1]
    ts = tm // n_sub
    rows = [slice(t * ts, (t + 1) * ts) for t in range(n_sub)]

    def residual(t):
        x = x_ref[0, rows[t], :]
        if n_seg:
            y = None
            for s_ref, w_ref in zip(seg_refs, wmix_refs):
                part = _dot(s_ref[0, rows[t], :].astype(BF16), w_ref[...])
                y = part if y is None else y + part
            x = x + m[5:6] * _rms(y, gmix_ref[...])
        return x

    def hidden(t, x):
        h = _modulated(x, m, sub, gpre_ref[...]).astype(BF16)
        gu = _dot(h, win_ref[...])
        return (_silu(gu[:, :dff]) * gu[:, dff:]).astype(BF16)

    def finish(t, x, y):
        gate = m[3 * sub + 2:3 * sub + 3]
        o_ref[0, rows[t], :] = x + (0.5 * gate) * _rms(y, gpost_ref[...])

    xs = [residual(t) for t in range(n_sub)]
    acts = [hidden(t, xs[t]) for t in range(n_sub)]
    ys = [_dot(a, wout_ref[...]) for a in acts]
    for t in range(n_sub):
        finish(t, xs[t], ys[t])


def _ffn(x, mod, sub, g_pre, g_post, w_in, w_out, mixer=None):
    b, seq_len, d = x.shape
    dff = w_out.shape[0]
    tm = _token_tile(seq_len)
    ins = [x, mod, g_pre.reshape(1, d), g_post.reshape(1, d), w_in, w_out]
    in_specs = [_tok_spec(tm, d), _mod_spec(mod), _const_spec((1, d)), _const_spec((1, d)),
                _const_spec(w_in.shape), _const_spec(w_out.shape)]
    n_seg = 0
    if mixer is not None:
        g_mix, segs, ws = mixer
        n_seg = len(segs)
        ins += [g_mix.reshape(1, d)] + list(segs) + list(ws)
        in_specs += ([_const_spec((1, d))] + [_tok_spec(tm, s.shape[-1]) for s in segs]
                     + [_const_spec(w.shape) for w in ws])
    n_sub = FFN_SUBTILES if tm % (8 * FFN_SUBTILES) == 0 else 1
    return pl.pallas_call(
        functools.partial(_ffn_body, sub=sub, dff=dff, n_seg=n_seg, n_sub=n_sub),
        grid=(b, seq_len // tm),
        in_specs=in_specs,
        out_specs=_tok_spec(tm, d),
        out_shape=jax.ShapeDtypeStruct(x.shape, F32),
        compiler_params=_cparams(2),
        name="ffn_half" if mixer is None else "mixer_out_ffn_half",
    )(*ins)


def _head_rms(a, g, bd):
    ss = _dot((a * a).astype(BF16), bd)
    return a * lax.rsqrt(ss * (1.0 / HEAD_DIM) + EPS) * g


def _rope(a, cos, sin):
    quarter = HEAD_DIM // 4
    outs = []
    for j in range(a.shape[1] // LANES):
        aj = a[:, j * LANES:(j + 1) * LANES]
        up = pltpu.roll(aj, LANES - quarter, axis=1)
        dn = pltpu.roll(aj, quarter, axis=1)
        lane = lax.broadcasted_iota(jnp.int32, aj.shape, 1)
        partner = jnp.where((lane % (2 * quarter)) < quarter, up, dn)
        outs.append(aj * cos + partner * sin)
    return outs[0] if len(outs) == 1 else jnp.concatenate(outs, axis=1)


def _ab_in_body(*refs, rope, aux):
    x_ref, mod_ref, gpre_ref, w_ref, qn_ref, kn_ref, bd_ref = refs[:7]
    refs = refs[7:]
    if rope:
        cos_ref, sin_ref = refs[:2]
        refs = refs[2:]
    rq_o, rk_o, rv_o, sg_o, gq_o, gk_o, gv_o = refs[:7]
    h = _modulated(x_ref[0], mod_ref[0], 1, gpre_ref[...]).astype(BF16)
    p = _dot(h, w_ref[...])
    o = 0
    rq_o[0] = p[:, o:o + RET_W].astype(rq_o.dtype)
    o += RET_W
    rk_o[0] = (p[:, o:o + RET_W] * (RET_DK ** -0.5)).astype(rk_o.dtype)
    o += RET_W
    rv_o[0] = p[:, o:o + RET_VW].astype(rv_o.dtype)
    o += RET_VW
    sg_o[0] = _silu(p[:, o:o + RET_VW]).astype(sg_o.dtype)
    o += RET_VW
    gq = p[:, o:o + GQA_W]
    o += GQA_W
    gk = p[:, o:o + KV_W]
    o += KV_W
    gv = p[:, o:o + KV_W]
    bd = bd_ref[...]
    qh = _head_rms(gq, qn_ref[...], bd)
    kh = _head_rms(gk, kn_ref[...], bd[:KV_W, :KV_W])
    if rope:
        qh = _rope(qh, cos_ref[...], sin_ref[...])
        kh = _rope(kh, cos_ref[...], sin_ref[...])
    gq_o[0] = (qh * SCORE_SCALE).astype(gq_o.dtype)
    gk_o[0] = kh.astype(gk_o.dtype)
    gv_o[0] = gv.astype(gv_o.dtype)
    if aux:
        k5_o, v5_o = refs[7:]
        k5_o[...] = kh.reshape(k5_o.shape)
        v5_o[...] = gv.reshape(v5_o.shape)


def _ab_in(x, mod, g_pre, w, qn, kn, rope_tables, aux):
    b, seq_len, d = x.shape
    tm = _token_tile(seq_len)
    rope = rope_tables is not None
    bd = jnp.asarray(np.kron(np.eye(N_Q), np.ones((HEAD_DIM, HEAD_DIM))), BF16)
    ins = [x, mod, g_pre.reshape(1, d), w, jnp.tile(qn, N_Q).reshape(1, GQA_W),
           jnp.tile(kn, N_KV).reshape(1, KV_W), bd]
    in_specs = [_tok_spec(tm, d), _mod_spec(mod), _const_spec((1, d)), _const_spec(w.shape),
                _const_spec((1, GQA_W)), _const_spec((1, KV_W)), _const_spec(bd.shape)]
    if rope:
        ins += list(rope_tables)
        in_specs += [pl.BlockSpec((tm, LANES), lambda b, i: (i, 0))] * 2
    widths = (RET_W, RET_W, RET_VW, RET_VW, GQA_W, KV_W, KV_W)
    out_specs = [_tok_spec(tm, wd) for wd in widths]
    out_shape = [jax.ShapeDtypeStruct((b, seq_len, wd), BF16) for wd in widths]
    if aux:
        assert b == 1
        out_specs += [pl.BlockSpec((tm, N_KV, HEAD_DIM), lambda b, i: (i, 0, 0))] * 2
        out_shape += [jax.ShapeDtypeStruct((seq_len, N_KV, HEAD_DIM), F32)] * 2
    return pl.pallas_call(
        functools.partial(_ab_in_body, rope=rope, aux=aux),
        grid=(b, seq_len // tm),
        in_specs=in_specs,
        out_specs=out_specs,
        out_shape=out_shape,
        compiler_params=_cparams(2),
        name="ab_in_proj",
    )(*ins)


def _rope_tables(seq_len):
    t = jnp.arange(seq_len)
    quarter = HEAD_DIM // 4
    inv = ROPE_THETA ** (-jnp.arange(quarter, dtype=F32) / quarter)
    ang_r = (t // GRID_W).astype(F32)[:, None] * inv[None, :]
    ang_c = (t % GRID_W).astype(F32)[:, None] * inv[None, :]
    cos = jnp.concatenate([jnp.cos(ang_r)] * 2 + [jnp.cos(ang_c)] * 2, axis=1)
    sin = jnp.concatenate([-jnp.sin(ang_r), jnp.sin(ang_r), -jnp.sin(ang_c), jnp.sin(ang_c)], axis=1)
    return jnp.tile(cos, (1, LANES // HEAD_DIM)), jnp.tile(sin, (1, LANES // HEAD_DIM))


def _ret_body(lgf_ref, lgb_ref, rq_ref, rk_ref, rv_ref, sg_ref, gn_ref, sf_ref, sb_ref,
              y_ref, ff_ref, fb_ref, sbs_ref, *, chunk, n_chunks, pairs_per_step):
    for pp in range(pairs_per_step):
        lanes = slice(pp * LANES, (pp + 1) * LANES)
        heads = slice(2 * pp, 2 * pp + 2)
        _ret_pair(pl.program_id(1) * pairs_per_step + pp, lgf_ref, lgb_ref,
                  rq_ref.at[:, :, lanes], rk_ref.at[:, :, lanes], rv_ref.at[:, :, lanes],
                  sg_ref.at[:, :, lanes], gn_ref.at[:, lanes], sf_ref.at[:, pp:pp + 1],
                  sb_ref.at[:, pp:pp + 1], y_ref.at[:, :, lanes], ff_ref.at[:, heads],
                  fb_ref.at[:, heads], sbs_ref, chunk=chunk, n_chunks=n_chunks)


def _ret_pair(pair, lgf_ref, lgb_ref, rq_ref, rk_ref, rv_ref, sg_ref, gn_ref, sf_ref, sb_ref,
              y_ref, ff_ref, fb_ref, sbs_ref, *, chunk, n_chunks):
    c = chunk
    lo_row = _low_half((1, LANES))
    lgf = jnp.where(lo_row, lgf_ref[2 * pair], lgf_ref[2 * pair + 1])
    lgb = jnp.where(lo_row, lgb_ref[2 * pair], lgb_ref[2 * pair + 1])
    idx = lax.broadcasted_iota(jnp.int32, (c, 1), 0).astype(F32)
    qdec_f = jnp.exp(lgf * (idx + 1.0))
    kdec_f = jnp.exp(lgf * (c - 1.0 - idx))
    cdec_f = jnp.exp(lgf * float(c))
    qdec_b = jnp.exp(lgb * (c - idx))
    kdec_b = jnp.exp(lgb * idx)
    cdec_b = jnp.exp(lgb * float(c))
    diff = (lax.broadcasted_iota(jnp.int32, (c, c), 0)
            - lax.broadcasted_iota(jnp.int32, (c, c), 1)).astype(F32)
    dmat = [jnp.exp(jnp.where(diff >= 0, lgf_ref[2 * pair + h], -lgb_ref[2 * pair + h]) * diff)
            for h in range(2)]
    rr = lax.broadcasted_iota(jnp.int32, (LANES, LANES), 0) < HEAD_DIM
    cc = lax.broadcasted_iota(jnp.int32, (LANES, LANES), 1) < HEAD_DIM
    same_head = rr == cc
    lo = _low_half((c, LANES))

    def load(ref, n):
        return ref[0, pl.ds(pl.multiple_of(n * c, c), c), :]

    def bwd_step(t, state):
        n = n_chunks - 1 - t
        sbs_ref[n] = state
        k = load(rk_ref, n).astype(F32)
        v = load(rv_ref, n).astype(BF16)
        kv = _dot_tn((k * kdec_b).astype(BF16), v)
        return state * cdec_b + jnp.where(same_head, kv, 0.0)

    unroll = int(np.gcd(n_chunks, RET_UNROLL))
    _store_head_states(fb_ref, lax.fori_loop(0, n_chunks, bwd_step, sb_ref[0, 0], unroll=unroll))

    gn = gn_ref[...]

    def head_mean(a):
        m0 = jnp.sum(jnp.where(lo, a, 0.0), axis=-1, keepdims=True)
        m1 = jnp.sum(jnp.where(lo, 0.0, a), axis=-1, keepdims=True)
        return jnp.where(lo, m0, m1) * (1.0 / RET_DV)

    def fwd_trip(t, state0):
        carried = {"state": state0}

        def scores(u):
            n = t * unroll + u
            q = load(rq_ref, n).astype(BF16)
            k = load(rk_ref, n).astype(BF16)
            v = load(rv_ref, n).astype(BF16)
            kv = jnp.where(same_head, _dot_tn((k.astype(F32) * kdec_f).astype(BF16), v), 0.0)
            return n, q, v, kv, [_dot_nt(_only_head(q, h), k) for h in range(2)]

        def probs(u, sc):
            n, q, v, kv, s = sc
            return n, q, v, kv, [(s[h] * dmat[h]).astype(BF16) for h in range(2)]

        def accumulate(u, pr):
            n, q, v, kv, p = pr
            state = carried["state"]
            qf = q.astype(F32)
            o = (_dot((qf * qdec_f).astype(BF16), state.astype(BF16))
                 + _dot((qf * qdec_b).astype(BF16), sbs_ref[n].astype(BF16)))
            o = o + jnp.where(lo, _dot(p[0], v), _dot(p[1], v))
            dev = o - head_mean(o)
            var = head_mean(dev * dev)
            y = dev * lax.rsqrt(var + RET_GN_EPS) * gn * load(sg_ref, n).astype(F32)
            y_ref[0, pl.ds(pl.multiple_of(n * c, c), c), :] = y.astype(y_ref.dtype)
            carried["state"] = state * cdec_f + kv

        _staged(list(range(unroll)), scores, probs, accumulate)
        return carried["state"]

    _store_head_states(ff_ref, lax.fori_loop(0, n_chunks // unroll, fwd_trip, sf_ref[0, 0]))


def _store_head_states(ref, state):
    ref[0, 0] = state[:RET_DK, :RET_DV]
    ref[0, 1] = state[RET_DK:, RET_DV:]


def _pair_states(s):
    b, nh, dk, dv = s.shape
    s = s.reshape(b, nh // 2, 2, dk, dv)
    z = jnp.zeros_like(s[:, :, 0])
    top = jnp.concatenate([s[:, :, 0], z], axis=-1)
    bot = jnp.concatenate([z, s[:, :, 1]], axis=-1)
    return jnp.concatenate([top, bot], axis=-2)


def _retention(rq, rk, rv, sg, gn, lg_f, lg_b, s_f, s_b):
    b, seq_len, _ = rq.shape
    c = RET_CHUNK if seq_len % RET_CHUNK == 0 else seq_len
    nc = seq_len // c
    npair = N_RET // 2
    pps = npair if nc == 1 else 1
    seq_spec = pl.BlockSpec((1, seq_len, pps * LANES), lambda i, p: (i, 0, p))
    st_spec = pl.BlockSpec((1, pps, LANES, LANES), lambda i, p: (i, p, 0, 0))
    fin_spec = pl.BlockSpec((1, 2 * pps, RET_DK, RET_DV), lambda i, p: (i, p, 0, 0))
    smem = pl.BlockSpec(memory_space=pltpu.SMEM)
    st_shape = jax.ShapeDtypeStruct((b, N_RET, RET_DK, RET_DV), F32)
    return pl.pallas_call(
        functools.partial(_ret_body, chunk=c, n_chunks=nc, pairs_per_step=pps),
        grid=(b, npair // pps),
        in_specs=[smem, smem, seq_spec, seq_spec, seq_spec, seq_spec,
                  pl.BlockSpec((1, pps * LANES), lambda i, p: (0, p)), st_spec, st_spec],
        out_specs=[seq_spec, fin_spec, fin_spec],
        out_shape=[jax.ShapeDtypeStruct((b, seq_len, RET_VW), BF16), st_shape, st_shape],
        scratch_shapes=[pltpu.VMEM((nc, LANES, LANES), F32)],
        compiler_params=_cparams(2),
        name="retention",
    )(lg_f, lg_b, rq, rk, rv, sg, gn.reshape(1, RET_VW), _pair_states(s_f), _pair_states(s_b))


def _staged(items, scores, probs, accumulate, ahead=2):
    n = len(items)
    queue = [scores(items[i]) for i in range(min(ahead, n))]
    pending = None
    for i in range(n):
        sc = queue.pop(0)
        if i + ahead < n:
            queue.append(scores(items[i + ahead]))
        ps = probs(items[i], sc)
        if pending is not None:
            accumulate(*pending)
        pending = (items[i], ps)
    accumulate(*pending)


def _row_probs(s_list):
    m = s_list[0].max(axis=-1, keepdims=True)
    for s in s_list[1:]:
        m = jnp.maximum(m, s.max(axis=-1, keepdims=True))
    ps = [jnp.exp2(s - m) for s in s_list]
    den = ps[0].sum(axis=-1, keepdims=True)
    for p in ps[1:]:
        den = den + p.sum(axis=-1, keepdims=True)
    return [p.astype(BF16) for p in ps], 1.0 / den


def _only_head(q, h):
    lo = _low_half(q.shape)
    zero = jnp.zeros_like(q)
    return jnp.where(lo, q, zero) if h == 0 else jnp.where(lo, zero, q)


def _attn_body(q_ref, k_ref, v_ref, o_ref, *, n_groups, kv_shared):
    tq = q_ref.shape[1]
    lo = _low_half((tq, LANES))
    outs = {}

    def lanes(j):
        return slice(j * LANES, (j + 1) * LANES)

    def scores(item):
        j, h = item
        k = k_ref[0] if kv_shared else k_ref[0, :, lanes(j)]
        return _dot_nt(_only_head(q_ref[0, :, lanes(j)].astype(BF16), h), k.astype(BF16))

    def probs(item, s):
        return _row_probs([s])

    def accumulate(item, pr):
        j, h = item
        ps, inv_den = pr
        v = v_ref[0] if kv_shared else v_ref[0, :, lanes(j)]
        outs[h] = _dot(ps[0], v.astype(BF16)) * inv_den
        if h == 1:
            o_ref[0, :, lanes(j)] = jnp.where(lo, outs[0], outs[1]).astype(o_ref.dtype)

    _staged([(j, h) for j in range(n_groups) for h in range(2)], scores, probs, accumulate)


def _col_max(x, slab=64):
    n, t = x.shape
    if n > slab and n % slab == 0:
        x = x.reshape(n // slab, slab, t).max(axis=0)
    return x.max(axis=0, keepdims=True)


def _gqa_t_body(q_ref, kn_ref, kc_ref, vtn_ref, vtc_ref, o_ref, *, n_groups, chunk):
    tq = q_ref.shape[1]
    top = lax.broadcasted_iota(jnp.int32, (LANES, tq), 0) < HEAD_DIM
    ln, lc = kn_ref.shape[1], kc_ref.shape[1]
    chunks = [(kn_ref, vtn_ref, c * chunk, chunk) for c in range(ln // chunk)] + [(kc_ref, vtc_ref, 0, lc)]
    items = [(j, h, c) for j in range(n_groups) for h in range(2) for c in range(len(chunks))]

    def scores(item):
        j, h, c = item
        k_ref, _, off, size = chunks[c]
        qh = _only_head(q_ref[0, :, j * LANES:(j + 1) * LANES], h)
        return _dot_nt(k_ref[0, off:off + size, :], qh)

    outs = {}
    state = {"m": None, "acc": None}

    def probs(item, st):
        c = item[2]
        mc = _col_max(st)
        m_new = mc if c == 0 else jnp.maximum(state["m"], mc)
        alpha = None if c == 0 else jnp.exp2(state["m"] - m_new)
        state["m"] = m_new
        return jnp.exp2(st - m_new).astype(BF16), alpha

    def accumulate(item, ps):
        j, h, c = item
        p, alpha = ps
        _, vt_ref, off, size = chunks[c]
        o = _dot(vt_ref[0, h, :, off:off + size], p)
        acc = o if c == 0 else state["acc"] * alpha + o
        state["acc"] = acc
        if c == len(chunks) - 1:
            den = acc[HEAD_DIM:HEAD_DIM + 1] if h == 0 else acc[0:1]
            outs[h] = acc * (1.0 / den)
            if h == 1:
                sl = slice(j * LANES, (j + 1) * LANES)
                o_ref[0, :, sl] = jnp.where(top, outs[0], outs[1]).T.astype(o_ref.dtype)

    _staged(items, scores, probs, accumulate)


def _vt_with_ones(v):
    vt = jnp.swapaxes(v, 1, 2)
    top = (np.arange(KV_W) < HEAD_DIM)[None, :, None]
    one = jnp.ones_like(vt)
    return jnp.stack([jnp.where(top, vt, one), jnp.where(top, one, vt)], axis=1)


def _gqa_attention_t(q, k_new, v_new, k_ctx, v_ctx, tq):
    b, lq, wq = q.shape
    ln, lc = k_new.shape[1], k_ctx.shape[1]
    chunk = GQA_KEY_CHUNK if ln % GQA_KEY_CHUNK == 0 else ln
    whole3 = lambda i, t: (i, 0, 0)
    whole4 = lambda i, t: (i, 0, 0, 0)
    return pl.pallas_call(
        functools.partial(_gqa_t_body, n_groups=wq // LANES, chunk=chunk),
        grid=(b, lq // tq),
        in_specs=[pl.BlockSpec((1, tq, wq), lambda i, t: (i, t, 0)),
                  pl.BlockSpec((1, ln, KV_W), whole3), pl.BlockSpec((1, lc, KV_W), whole3),
                  pl.BlockSpec((1, 2, KV_W, ln), whole4), pl.BlockSpec((1, 2, KV_W, lc), whole4)],
        out_specs=pl.BlockSpec((1, tq, wq), lambda i, t: (i, t, 0)),
        out_shape=jax.ShapeDtypeStruct((b, lq, wq), BF16),
        compiler_params=_cparams(2),
        name="gqa_attention_t",
    )(q, k_new, k_ctx, _vt_with_ones(v_new), _vt_with_ones(v_ctx))


def _attention(q, k, v, kv_shared, tq):
    b, lq, wq = q.shape
    lk, wk = k.shape[1:]
    return pl.pallas_call(
        functools.partial(_attn_body, n_groups=wq // LANES, kv_shared=kv_shared),
        grid=(b, lq // tq),
        in_specs=[pl.BlockSpec((1, tq, wq), lambda i, t: (i, t, 0)),
                  pl.BlockSpec((1, lk, wk), lambda i, t: (i, 0, 0)),
                  pl.BlockSpec((1, lk, wk), lambda i, t: (i, 0, 0))],
        out_specs=pl.BlockSpec((1, tq, wq), lambda i, t: (i, t, 0)),
        out_shape=jax.ShapeDtypeStruct((b, lq, wq), BF16),
        compiler_params=_cparams(2),
        name="dense_attention",
    )(q, k, v)


def _na_plan(rows):
    g = NA_ROWS_PER_STEP
    kr = min(NA_KR_MAX, rows)
    u = min(-(-(g + kr - 1) // 2) * 2, rows)
    n_steps = rows // g
    bases, vids, variants, keys = [], [], [], {}
    for s in range(n_steps):
        base = int(np.clip(s * g - kr // 2, 0, rows - u)) // 2 * 2
        r = s * g + np.arange(g)[:, None]
        krow = base + np.arange(u)[None, :]
        r0 = np.clip(r - kr // 2, 0, rows - kr)
        valid = (krow >= r0) & (krow < r0 + kr)
        assert (valid.sum(axis=1) == kr).all(), "key window not covered by the step's row range"
        dr = np.where(valid, krow - r + (NA_KR_MAX - 1), 0)
        key = (dr.tobytes(), valid.tobytes())
        if key not in keys:
            keys[key] = len(variants)
            variants.append((dr, valid))
        bases.append(base)
        vids.append(keys[key])
    return u, bases, vids, variants


def _na_bias_tables(rpb, variants, u):
    g = NA_ROWS_PER_STEP
    nh, n_dr, n_dc = rpb.shape
    cols = np.arange(GRID_W)
    c0 = np.clip(cols - NA_KC // 2, 0, GRID_W - NA_KC)
    kc = np.arange(GRID_W)[None, :]
    cvalid = (kc >= c0[:, None]) & (kc < c0[:, None] + NA_KC)
    dc = kc - cols[:, None] + (NA_KC - 1)
    sel_c = cvalid[None] & (dc[None] == np.arange(n_dc)[:, None, None])
    band = jnp.einsum("hde,eck->hdkc", rpb.astype(F32), jnp.asarray(sel_c, F32),
                      precision=lax.Precision.HIGHEST)
    band = jnp.where(cvalid.T[None, None], band * LOG2E, MASK_VALUE)
    masked = jnp.full((nh, GRID_W, GRID_W), MASK_VALUE, F32)
    tables = []
    for dr, rvalid in variants:
        key_rows = [jnp.stack([band[:, dr[q, a]] if rvalid[q, a] else masked for q in range(g)], axis=2)
                    for a in range(u)]
        tables.append(jnp.stack(key_rows, axis=1).reshape(nh, u * GRID_W, g * GRID_W))
    return jnp.stack(tables)


def _na_body(base_ref, vid_ref, q_ref, k_ref, vt_ref, kc_ref, vct_ref, bias_ref, o_ref, *, n_steps, tq, tk):
    top = lax.broadcasted_iota(jnp.int32, (LANES, tq), 0) < HEAD_DIM
    kc = kc_ref[0]
    lc = kc.shape[0]
    ones_rows = 8
    vct = jnp.concatenate([vct_ref[0], jnp.ones((ones_rows, lc), BF16)], axis=0)
    ones_win = jnp.ones((ones_rows, tk), BF16)
    unroll = int(np.gcd(n_steps, NA_UNROLL))

    def trip(t, carry):
        items = [(u, h) for u in range(unroll) for h in range(2)]
        steps = [t * unroll + u for u in range(unroll)]
        outs = {}

        def q_rows(u):
            return pl.ds(pl.multiple_of(steps[u] * tq, tq), tq)

        def k_rows(u):
            return pl.ds(pl.multiple_of(base_ref[steps[u]] * GRID_W, LANES), tk)

        def scores(item):
            u, h = item
            qh = _only_head(q_ref[0, q_rows(u), :], h)
            st_win = _dot_nt(k_ref[0, k_rows(u), :], qh) + bias_ref[vid_ref[steps[u]], h]
            return st_win, _dot_nt(kc, qh)

        def probs(item, sc):
            m = jnp.maximum(_col_max(sc[0]), _col_max(sc[1]))
            return [jnp.exp2(s - m).astype(BF16) for s in sc]

        def accumulate(item, ps):
            u, h = item
            vt = jnp.concatenate([vt_ref[0, :, k_rows(u)], ones_win], axis=0)
            acc = _dot(vt, ps[0]) + _dot(vct, ps[1])
            outs[h] = acc[:LANES] * (1.0 / acc[LANES:LANES + 1])
            if h == 1:
                o_ref[0, q_rows(u), :] = jnp.where(top, outs[0], outs[1]).T.astype(o_ref.dtype)

        _staged(items, scores, probs, accumulate)
        return carry

    lax.fori_loop(0, n_steps // unroll, trip, 0)


def _na_attention(q, k, vt, k_ctx, v_ctx, rpb):
    b, seq_len, w = q.shape
    rows = seq_len // GRID_W
    lc = k_ctx.shape[1]
    u, bases, vids, variants = _na_plan(rows)
    bias = _na_bias_tables(rpb, variants, u)
    nv = bias.shape[0]
    tq = NA_ROWS_PER_STEP * GRID_W
    tk = u * GRID_W
    n_steps = rows // NA_ROWS_PER_STEP
    smem = pl.BlockSpec(memory_space=pltpu.SMEM)
    seq_spec = pl.BlockSpec((1, seq_len, LANES), lambda p, i: (i, 0, p))
    return pl.pallas_call(
        functools.partial(_na_body, n_steps=n_steps, tq=tq, tk=tk),
        grid=(w // LANES, b),
        in_specs=[smem, smem, seq_spec, seq_spec,
                  pl.BlockSpec((1, LANES, seq_len), lambda p, i: (i, p, 0)),
                  pl.BlockSpec((1, lc, LANES), lambda p, i: (i, 0, p)),
                  pl.BlockSpec((1, LANES, lc), lambda p, i: (i, p, 0)),
                  pl.BlockSpec((nv, 2, tk, tq), lambda p, i: (0, p, 0, 0))],
        out_specs=seq_spec,
        out_shape=jax.ShapeDtypeStruct((b, seq_len, w), BF16),
        compiler_params=_cparams(2),
        name="neighbourhood_attention",
    )(jnp.asarray(bases, jnp.int32), jnp.asarray(vids, jnp.int32), q, k, vt, k_ctx,
      jnp.swapaxes(v_ctx, 1, 2), bias)


def _na_in_body(x_ref, mod_ref, gpre_ref, w_ref, q_o, k_o, v_o, *aux_o):
    h = _modulated(x_ref[0], mod_ref[0], 1, gpre_ref[...]).astype(BF16)
    p = _dot(h, w_ref[...])
    q_o[0] = (p[:, :NA_W] * SCORE_SCALE).astype(q_o.dtype)
    k = p[:, NA_W:2 * NA_W]
    v = p[:, 2 * NA_W:]
    k_o[0] = k.astype(k_o.dtype)
    v_o[0] = (v if aux_o else v.T).astype(v_o.dtype)
    if aux_o:
        k5_o, v5_o = aux_o
        k5_o[...] = k.reshape(k5_o.shape)
        v5_o[...] = v.reshape(v5_o.shape)


def _na_in(x, mod, g_pre, w, aux):
    b, seq_len, d = x.shape
    tm = _token_tile(seq_len)
    out_specs = [_tok_spec(tm, NA_W)] * 3
    out_shape = [jax.ShapeDtypeStruct((b, seq_len, NA_W), BF16)] * 3
    if aux:
        assert b == 1
        out_specs += [pl.BlockSpec((tm, N_NA, HEAD_DIM), lambda b, i: (i, 0, 0))] * 2
        out_shape += [jax.ShapeDtypeStruct((seq_len, N_NA, HEAD_DIM), F32)] * 2
    else:
        out_specs[2] = pl.BlockSpec((1, NA_W, tm), lambda b, i: (b, 0, i))
        out_shape[2] = jax.ShapeDtypeStruct((b, NA_W, seq_len), BF16)
    return pl.pallas_call(
        _na_in_body,
        grid=(b, seq_len // tm),
        in_specs=[_tok_spec(tm, d), _mod_spec(mod), _const_spec((1, d)), _const_spec(w.shape)],
        out_specs=out_specs,
        out_shape=out_shape,
        compiler_params=_cparams(2),
        name="na_in_proj",
    )(x, mod, g_pre.reshape(1, d), w)


_Q_HEAD_ORDER = [h for j in range(Q_GROUP) for h in range(j, N_Q, Q_GROUP)]


def _permute_q_cols(w_in):
    q0 = 2 * RET_W + 2 * RET_VW
    heads = [w_in[:, q0 + h * HEAD_DIM:q0 + (h + 1) * HEAD_DIM] for h in _Q_HEAD_ORDER]
    return jnp.concatenate([w_in[:, :q0]] + heads + [w_in[:, q0 + GQA_W:]], axis=1)


def _permute_att_rows(w_att):
    return jnp.concatenate([w_att[h * HEAD_DIM:(h + 1) * HEAD_DIM] for h in _Q_HEAD_ORDER], axis=0)


def _trunk(x, mods, ctx_layers, wts):
    b, seq_len, d = x.shape
    is_ctx = ctx_layers is None
    depth = len(mods)
    flat = (lambda a: a.reshape(1, b * seq_len, a.shape[-1])) if is_ctx else (lambda a: a)
    unflat = (lambda a: a.reshape(b, seq_len, a.shape[-1])) if is_ctx else (lambda a: a)
    aux = []
    x = flat(x)
    for l in range(depth):
        mod = mods[l]
        i = l // 2
        x = _ffn(x, mod, 0, wts["norm_pre"][l, 0], wts["norm_post"][l, 0],
                 wts["ffn_w_in"][l][0], wts["ffn_w_out"][l][0])
        if l % 2 == 0:
            rope_tables = None if is_ctx else _rope_tables(seq_len)
            proj = _ab_in(x, mod, wts["norm_pre"][l, 1], wts["ab_w_in"][i], wts["gqa_q_norm"][i],
                          wts["gqa_k_norm"][i], rope_tables, is_ctx)
            rq, rk, rv, sg, gq, gk, gv = [unflat(a) for a in proj[:7]]
            if is_ctx:
                s_f = jnp.zeros((b, N_RET, RET_DK, RET_DV), F32)
                s_b = s_f
            else:
                ck, cv, s_f, s_b = ctx_layers[l]
                lc = ck.shape[1]
            y_ret, fin_f, fin_b = _retention(rq, rk, rv, sg, wts["ret_gn"][i], wts["lg_f"][i],
                                             wts["lg_b"][i], s_f.astype(F32), s_b.astype(F32))
            if is_ctx:
                y_att = _attention(gq, gk, gv, True, seq_len)
            else:
                y_att = _gqa_attention_t(gq, gk, gv, ck.reshape(b, lc, KV_W).astype(BF16),
                                         cv.reshape(b, lc, KV_W).astype(BF16), GQA_TQ)
            mixer = (wts["norm_post"][l, 1], [flat(y_ret), flat(y_att)],
                     [wts["ab_w_out_ret"][i], wts["ab_w_out_att"][i]])
            aux.append((proj[7], proj[8], fin_f, fin_b) if is_ctx else None)
        else:
            proj = _na_in(x, mod, wts["norm_pre"][l, 1], wts["na_w_qkv"][i], is_ctx)
            q, k, v = [unflat(a) for a in proj[:3]]
            if is_ctx:
                y = _attention(q, k, v, False, seq_len)
                aux.append((proj[3], proj[4]))
            else:
                ck, cv = ctx_layers[l]
                lc = ck.shape[1]
                y = _na_attention(q, k, v, ck.reshape(b, lc, NA_W).astype(BF16),
                                  cv.reshape(b, lc, NA_W).astype(BF16), wts["na_rpb"][i])
                aux.append(None)
            mixer = (wts["norm_post"][l, 1], [flat(y)], [wts["na_w_out"][i]])
        x = _ffn(x, mod, 2, wts["norm_pre"][l, 2], wts["norm_post"][l, 2],
                 wts["ffn_w_in"][l][1], wts["ffn_w_out"][l][1], mixer=mixer)
    return unflat(x), aux


def kernel(x_prompt, x_sample, cache_gqa_k, cache_gqa_v, state_ret_fwd, state_ret_bwd, cache_na_k,
           cache_na_v, c, c_ctx, mod_w, mod_b, norm_pre, norm_post, ffn_w_in, ffn_w_out, ab_w_in,
           ab_w_out, ret_decay_fwd, ret_decay_bwd, ret_gn, gqa_q_norm, gqa_k_norm, na_w_qkv, na_w_out,
           na_rpb):
    depth = mod_w.shape[0]
    d = x_prompt.shape[-1]
    n_dec = c.shape[0]
    batch, seq = x_prompt.shape[:2]

    n_cond = n_dec + 1
    pad = (-n_cond) % 8
    cond = jnp.concatenate([c, c_ctx[None, :], jnp.zeros((pad, d), F32)], axis=0)
    mod_all = _modulation(cond, mod_w, mod_b).reshape(depth, n_cond + pad, 3 * N_SUB, d)
    mods_sample = [mod_all[l, :n_dec] for l in range(depth)]
    mods_prompt = [mod_all[l, n_dec:n_dec + 1] for l in range(depth)]

    n_even = ab_w_in.shape[0]
    wts = {
        "norm_pre": norm_pre, "norm_post": norm_post,
        "ffn_w_in": [[ffn_w_in[l, s].astype(BF16) for s in range(2)] for l in range(depth)],
        "ffn_w_out": [[ffn_w_out[l, s].astype(BF16) for s in range(2)] for l in range(depth)],
        "ab_w_in": [_permute_q_cols(ab_w_in[i]).astype(BF16) for i in range(n_even)],
        "ab_w_out_ret": [ab_w_out[i, :RET_VW].astype(BF16) for i in range(n_even)],
        "ab_w_out_att": [_permute_att_rows(ab_w_out[i, RET_VW:]).astype(BF16) for i in range(n_even)],
        "lg_f": jax.nn.log_sigmoid(ret_decay_fwd.astype(F32)),
        "lg_b": jax.nn.log_sigmoid(ret_decay_bwd.astype(F32)),
        "ret_gn": ret_gn, "gqa_q_norm": gqa_q_norm, "gqa_k_norm": gqa_k_norm,
        "na_w_qkv": [na_w_qkv[i].astype(BF16) for i in range(na_w_qkv.shape[0])],
        "na_w_out": [na_w_out[i].astype(BF16) for i in range(na_w_out.shape[0])],
        "na_rpb": na_rpb,
    }

    y_prompt, ctx_aux = _trunk(x_prompt, mods_prompt, None, wts)

    ctx_layers = []
    for l in range(depth):
        i = l // 2
        if l % 2 == 0:
            ctx_layers.append((cache_gqa_k[:, i], cache_gqa_v[:, i], state_ret_fwd[:, i], state_ret_bwd[:, i]))
        else:
            ctx_layers.append((cache_na_k[:, i], cache_na_v[:, i]))
    y_sample, _ = _trunk(x_sample, mods_sample, ctx_layers, wts)

    def stack(idx, layers, tail):
        return jnp.stack([ctx_aux[l][idx].reshape((batch,) + tail) for l in layers], axis=1)

    even = range(0, depth, 2)
    odd = range(1, depth, 2)
    new_gqa_k = stack(0, even, (seq, N_KV, HEAD_DIM))
    new_gqa_v = stack(1, even, (seq, N_KV, HEAD_DIM))
    new_ret_fwd = stack(2, even, (N_RET, RET_DK, RET_DV))
    new_ret_bwd = stack(3, even, (N_RET, RET_DK, RET_DV))
    new_na_k = stack(0, odd, (seq, N_NA, HEAD_DIM))
    new_na_v = stack(1, odd, (seq, N_NA, HEAD_DIM))
    return (y_prompt, y_sample, new_gqa_k, new_gqa_v, new_ret_fwd, new_ret_bwd, new_na_k, new_na_v)
```

```python
import functools

import numpy as np
import jax
import jax.numpy as jnp
from jax import lax
from jax.experimental import pallas as pl
from jax.experimental.pallas import tpu as pltpu

F32 = jnp.float32
BF16 = jnp.bfloat16

GRID_W = 64
N_SUB = 3
HEAD_DIM = 64
N_RET = 8
RET_DK = 64
RET_DV = 64
RET_GN_EPS = 1e-5
N_Q = 8
N_KV = 2
Q_GROUP = N_Q // N_KV
ROPE_THETA = 10000.0
N_NA = 16
NA_KR_MAX = 8
NA_KC = 16
EPS = 1e-6
RET_W = N_RET * RET_DK
RET_VW = N_RET * RET_DV
GQA_W = N_Q * HEAD_DIM
KV_W = N_KV * HEAD_DIM
NA_W = N_NA * HEAD_DIM

LANES = 128
VMEM_LIMIT = 56 * 1024 * 1024
RET_CHUNK = 256
RET_UNROLL = 16
NA_ROWS_PER_STEP = 4
FFN_HIDDEN_CHUNK = 1536
FFN_SUBTILES = 2
NA_UNROLL = 16
GQA_TQ = 256
GQA_KEY_CHUNK = 1024
MASK_VALUE = -1e30
LOG2E = 1.4426950408889634
SCORE_SCALE = HEAD_DIM ** -0.5 * LOG2E


def _cparams(n_grid):
    return pltpu.CompilerParams(dimension_semantics=("parallel",) * n_grid,
                                vmem_limit_bytes=VMEM_LIMIT)


def _const_spec(shape):
    nd = len(shape)
    return pl.BlockSpec(shape, lambda *_: (0,) * nd, pipeline_mode=pl.Buffered(1))


def _silu(x):
    return x * jax.nn.sigmoid(x)


def _rms(x, g):
    ms = jnp.mean(x * x, axis=-1, keepdims=True)
    return x * lax.rsqrt(ms + EPS) * g


def _modulated(x, m, sub, g_pre):
    shift = m[3 * sub:3 * sub + 1]
    scale = m[3 * sub + 1:3 * sub + 2]
    return _rms(x, g_pre) * (1.0 + scale) + shift


def _dot(a, b):
    return jnp.dot(a, b, preferred_element_type=F32)


def _dot_nt(a, b):
    return lax.dot_general(a, b, (((1,), (1,)), ((), ())), preferred_element_type=F32)


def _dot_tn(a, b):
    return lax.dot_general(a, b, (((0,), (0,)), ((), ())), preferred_element_type=F32)


def _low_half(shape):
    return lax.broadcasted_iota(jnp.int32, shape, len(shape) - 1) < HEAD_DIM


def _mod_body(c_ref, w_ref, b_ref, o_ref):
    a = _silu(c_ref[...]).astype(BF16)
    o_ref[0] = _dot(a, w_ref[0].astype(BF16)) + b_ref[0]


def _modulation(cond, mod_w, mod_b):
    depth, d, n = mod_w.shape
    rows = cond.shape[0]
    tn = 1152 if n % 1152 == 0 else n
    return pl.pallas_call(
        _mod_body,
        grid=(depth, n // tn),
        in_specs=[pl.BlockSpec((rows, d), lambda l, j: (0, 0)),
                  pl.BlockSpec((1, d, tn), lambda l, j: (l, 0, j)),
                  pl.BlockSpec((1, 1, tn), lambda l, j: (l, 0, j))],
        out_specs=pl.BlockSpec((1, rows, tn), lambda l, j: (l, 0, j)),
        out_shape=jax.ShapeDtypeStruct((depth, rows, n), F32),
        compiler_params=_cparams(2),
        name="modulation",
    )(cond, mod_w, mod_b.reshape(depth, 1, n))


def _tok_spec(tm, width):
    return pl.BlockSpec((1, tm, width), lambda b, i: (b, i, 0))


def _mod_spec(mod):
    nm, rows, d = mod.shape
    if nm == 1:
        return pl.BlockSpec((1, rows, d), lambda b, i: (0, 0, 0))
    return pl.BlockSpec((1, rows, d), lambda b, i: (b, 0, 0))


def _token_tile(seq_len):
    return 512 if seq_len % 512 == 0 else seq_len


def _ff_chunks(dff, width=FFN_HIDDEN_CHUNK):
    edges = list(range(0, dff, width)) + [dff]
    return list(zip(edges[:-1], edges[1:]))


def _ffn_body(*refs, sub, dff, n_seg, n_sub):
    x_ref, mod_ref, gpre_ref, gpost_ref, win_ref, wout_ref = refs[:6]
    rest = refs[6:]
    if n_seg:
        gmix_ref = rest[0]
        seg_refs = rest[1:1 + n_seg]
        wmix_refs = rest[1 + n_seg:1 + 2 * n_seg]
    o_ref = refs[-1]
    m = mod_ref[0]
    tm = x_ref.shape[1]
    ts = tm // n_sub
    rows = [slice(t * ts, (t + 1) * ts) for t in range(n_sub)]

    def residual(t):
        x = x_ref[0, rows[t], :]
        if n_seg:
            y = None
            for s_ref, w_ref in zip(seg_refs, wmix_refs):
                part = _dot(s_ref[0, rows[t], :].astype(BF16), w_ref[...])
                y = part if y is None else y + part
            x = x + m[5:6] * _rms(y, gmix_ref[...])
        return x

    def hidden(t, x):
        h = _modulated(x, m, sub, gpre_ref[...]).astype(BF16)
        parts = []
        for c0, c1 in _ff_chunks(dff):
            g = _dot(h, win_ref[:, c0:c1])
            u = _dot(h, win_ref[:, dff + c0:dff + c1])
            parts.append((_silu(g) * u).astype(BF16))
        return jnp.concatenate(parts, axis=1)

    def finish(t, x, y):
        gate = m[3 * sub + 2:3 * sub + 3]
        o_ref[0, rows[t], :] = x + (0.5 * gate) * _rms(y, gpost_ref[...])

    xs = [residual(t) for t in range(n_sub)]
    acts = [hidden(t, xs[t]) for t in range(n_sub)]
    ys = [_dot(a, wout_ref[...]) for a in acts]
    for t in range(n_sub):
        finish(t, xs[t], ys[t])


def _ffn(x, mod, sub, g_pre, g_post, w_in, w_out, mixer=None):
    b, seq_len, d = x.shape
    dff = w_out.shape[0]
    tm = _token_tile(seq_len)
    ins = [x, mod, g_pre.reshape(1, d), g_post.reshape(1, d), w_in, w_out]
    in_specs = [_tok_spec(tm, d), _mod_spec(mod), _const_spec((1, d)), _const_spec((1, d)),
                _const_spec(w_in.shape), _const_spec(w_out.shape)]
    n_seg = 0
    if mixer is not None:
        g_mix, segs, ws = mixer
        n_seg = len(segs)
        ins += [g_mix.reshape(1, d)] + list(segs) + list(ws)
        in_specs += ([_const_spec((1, d))] + [_tok_spec(tm, s.shape[-1]) for s in segs]
                     + [_const_spec(w.shape) for w in ws])
    n_sub = FFN_SUBTILES if tm % (8 * FFN_SUBTILES) == 0 else 1
    return pl.pallas_call(
        functools.partial(_ffn_body, sub=sub, dff=dff, n_seg=n_seg, n_sub=n_sub),
        grid=(b, seq_len // tm),
        in_specs=in_specs,
        out_specs=_tok_spec(tm, d),
        out_shape=jax.ShapeDtypeStruct(x.shape, F32),
        compiler_params=_cparams(2),
        name="ffn_half" if mixer is None else "mixer_out_ffn_half",
    )(*ins)


def _head_rms(a, g, bd):
    ss = _dot((a * a).astype(BF16), bd)
    return a * lax.rsqrt(ss * (1.0 / HEAD_DIM) + EPS) * g


def _rope(a, cos, sin):
    quarter = HEAD_DIM // 4
    outs = []
    for j in range(a.shape[1] // LANES):
        aj = a[:, j * LANES:(j + 1) * LANES]
        up = pltpu.roll(aj, LANES - quarter, axis=1)
        dn = pltpu.roll(aj, quarter, axis=1)
        lane = lax.broadcasted_iota(jnp.int32, aj.shape, 1)
        partner = jnp.where((lane % (2 * quarter)) < quarter, up, dn)
        outs.append(aj * cos + partner * sin)
    return outs[0] if len(outs) == 1 else jnp.concatenate(outs, axis=1)


def _ab_in_body(*refs, rope, aux):
    x_ref, mod_ref, gpre_ref, w_ref, qn_ref, kn_ref, bd_ref = refs[:7]
    refs = refs[7:]
    if rope:
        cos_ref, sin_ref = refs[:2]
        refs = refs[2:]
    rq_o, rk_o, rv_o, sg_o, gq_o, gk_o, gv_o = refs[:7]
    h = _modulated(x_ref[0], mod_ref[0], 1, gpre_ref[...]).astype(BF16)
    p = _dot(h, w_ref[...])
    o = 0
    rq_o[0] = p[:, o:o + RET_W].astype(rq_o.dtype)
    o += RET_W
    rk_o[0] = (p[:, o:o + RET_W] * (RET_DK ** -0.5)).astype(rk_o.dtype)
    o += RET_W
    rv_o[0] = p[:, o:o + RET_VW].astype(rv_o.dtype)
    o += RET_VW
    sg_o[0] = _silu(p[:, o:o + RET_VW]).astype(sg_o.dtype)
    o += RET_VW
    gq = p[:, o:o + GQA_W]
    o += GQA_W
    gk = p[:, o:o + KV_W]
    o += KV_W
    gv = p[:, o:o + KV_W]
    bd = bd_ref[...]
    qh = _head_rms(gq, qn_ref[...], bd)
    kh = _head_rms(gk, kn_ref[...], bd[:KV_W, :KV_W])
    if rope:
        qh = _rope(qh, cos_ref[...], sin_ref[...])
        kh = _rope(kh, cos_ref[...], sin_ref[...])
    gq_o[0] = (qh * SCORE_SCALE).astype(gq_o.dtype)
    gk_o[0] = kh.astype(gk_o.dtype)
    gv_o[0] = gv.astype(gv_o.dtype)
    if aux:
        k5_o, v5_o = refs[7:]
        k5_o[...] = kh.reshape(k5_o.shape)
        v5_o[...] = gv.reshape(v5_o.shape)


def _ab_in(x, mod, g_pre, w, qn, kn, rope_tables, aux):
    b, seq_len, d = x.shape
    tm = _token_tile(seq_len)
    rope = rope_tables is not None
    bd = jnp.asarray(np.kron(np.eye(N_Q), np.ones((HEAD_DIM, HEAD_DIM))), BF16)
    ins = [x, mod, g_pre.reshape(1, d), w, jnp.tile(qn, N_Q).reshape(1, GQA_W),
           jnp.tile(kn, N_KV).reshape(1, KV_W), bd]
    in_specs = [_tok_spec(tm, d), _mod_spec(mod), _const_spec((1, d)), _const_spec(w.shape),
                _const_spec((1, GQA_W)), _const_spec((1, KV_W)), _const_spec(bd.shape)]
    if rope:
        ins += list(rope_tables)
        in_specs += [pl.BlockSpec((tm, LANES), lambda b, i: (i, 0))] * 2
    widths = (RET_W, RET_W, RET_VW, RET_VW, GQA_W, KV_W, KV_W)
    out_specs = [_tok_spec(tm, wd) for wd in widths]
    out_shape = [jax.ShapeDtypeStruct((b, seq_len, wd), BF16) for wd in widths]
    if aux:
        assert b == 1
        out_specs += [pl.BlockSpec((tm, N_KV, HEAD_DIM), lambda b, i: (i, 0, 0))] * 2
        out_shape += [jax.ShapeDtypeStruct((seq_len, N_KV, HEAD_DIM), F32)] * 2
    return pl.pallas_call(
        functools.partial(_ab_in_body, rope=rope, aux=aux),
        grid=(b, seq_len // tm),
        in_specs=in_specs,
        out_specs=out_specs,
        out_shape=out_shape,
        compiler_params=_cparams(2),
        name="ab_in_proj",
    )(*ins)


def _rope_tables(seq_len):
    t = jnp.arange(seq_len)
    quarter = HEAD_DIM // 4
    inv = ROPE_THETA ** (-jnp.arange(quarter, dtype=F32) / quarter)
    ang_r = (t // GRID_W).astype(F32)[:, None] * inv[None, :]
    ang_c = (t % GRID_W).astype(F32)[:, None] * inv[None, :]
    cos = jnp.concatenate([jnp.cos(ang_r)] * 2 + [jnp.cos(ang_c)] * 2, axis=1)
    sin = jnp.concatenate([-jnp.sin(ang_r), jnp.sin(ang_r), -jnp.sin(ang_c), jnp.sin(ang_c)], axis=1)
    return jnp.tile(cos, (1, LANES // HEAD_DIM)), jnp.tile(sin, (1, LANES // HEAD_DIM))


def _ret_body(lgf_ref, lgb_ref, rq_ref, rk_ref, rv_ref, sg_ref, gn_ref, sf_ref, sb_ref,
              y_ref, ff_ref, fb_ref, sbs_ref, *, chunk, n_chunks, pairs_per_step):
    for pp in range(pairs_per_step):
        lanes = slice(pp * LANES, (pp + 1) * LANES)
        heads = slice(2 * pp, 2 * pp + 2)
        _ret_pair(pl.program_id(1) * pairs_per_step + pp, lgf_ref, lgb_ref,
                  rq_ref.at[:, :, lanes], rk_ref.at[:, :, lanes], rv_ref.at[:, :, lanes],
                  sg_ref.at[:, :, lanes], gn_ref.at[:, lanes], sf_ref.at[:, pp:pp + 1],
                  sb_ref.at[:, pp:pp + 1], y_ref.at[:, :, lanes], ff_ref.at[:, heads],
                  fb_ref.at[:, heads], sbs_ref, chunk=chunk, n_chunks=n_chunks)


def _ret_pair(pair, lgf_ref, lgb_ref, rq_ref, rk_ref, rv_ref, sg_ref, gn_ref, sf_ref, sb_ref,
              y_ref, ff_ref, fb_ref, sbs_ref, *, chunk, n_chunks):
    c = chunk
    lo_row = _low_half((1, LANES))
    lgf = jnp.where(lo_row, lgf_ref[2 * pair], lgf_ref[2 * pair + 1])
    lgb = jnp.where(lo_row, lgb_ref[2 * pair], lgb_ref[2 * pair + 1])
    idx = lax.broadcasted_iota(jnp.int32, (c, 1), 0).astype(F32)
    qdec_f = jnp.exp(lgf * (idx + 1.0))
    kdec_f = jnp.exp(lgf * (c - 1.0 - idx))
    cdec_f = jnp.exp(lgf * float(c))
    qdec_b = jnp.exp(lgb * (c - idx))
    kdec_b = jnp.exp(lgb * idx)
    cdec_b = jnp.exp(lgb * float(c))
    diff = (lax.broadcasted_iota(jnp.int32, (c, c), 0)
            - lax.broadcasted_iota(jnp.int32, (c, c), 1)).astype(F32)
    dmat = [jnp.exp(jnp.where(diff >= 0, lgf_ref[2 * pair + h], -lgb_ref[2 * pair + h]) * diff)
            for h in range(2)]
    rr = lax.broadcasted_iota(jnp.int32, (LANES, LANES), 0) < HEAD_DIM
    cc = lax.broadcasted_iota(jnp.int32, (LANES, LANES), 1) < HEAD_DIM
    same_head = rr == cc
    lo = _low_half((c, LANES))

    def load(ref, n):
        return ref[0, pl.ds(pl.multiple_of(n * c, c), c), :]

    def bwd_step(t, state):
        n = n_chunks - 1 - t
        sbs_ref[n] = state
        k = load(rk_ref, n).astype(F32)
        v = load(rv_ref, n).astype(BF16)
        kv = _dot_tn((k * kdec_b).astype(BF16), v)
        return state * cdec_b + jnp.where(same_head, kv, 0.0)

    unroll = int(np.gcd(n_chunks, RET_UNROLL))
    _store_head_states(fb_ref, lax.fori_loop(0, n_chunks, bwd_step, sb_ref[0, 0], unroll=unroll))

    gn = gn_ref[...]

    def head_mean(a):
        m0 = jnp.sum(jnp.where(lo, a, 0.0), axis=-1, keepdims=True)
        m1 = jnp.sum(jnp.where(lo, 0.0, a), axis=-1, keepdims=True)
        return jnp.where(lo, m0, m1) * (1.0 / RET_DV)

    def fwd_trip(t, state0):
        carried = {"state": state0}

        def scores(u):
            n = t * unroll + u
            q = load(rq_ref, n).astype(BF16)
            k = load(rk_ref, n).astype(BF16)
            v = load(rv_ref, n).astype(BF16)
            kv = jnp.where(same_head, _dot_tn((k.astype(F32) * kdec_f).astype(BF16), v), 0.0)
            return n, q, v, kv, [_dot_nt(_only_head(q, h), k) for h in range(2)]

        def probs(u, sc):
            n, q, v, kv, s = sc
            return n, q, v, kv, [(s[h] * dmat[h]).astype(BF16) for h in range(2)]

        def accumulate(u, pr):
            n, q, v, kv, p = pr
            state = carried["state"]
            qf = q.astype(F32)
            o = (_dot((qf * qdec_f).astype(BF16), state.astype(BF16))
                 + _dot((qf * qdec_b).astype(BF16), sbs_ref[n].astype(BF16)))
            o = o + jnp.where(lo, _dot(p[0], v), _dot(p[1], v))
            dev = o - head_mean(o)
            var = head_mean(dev * dev)
            y = dev * lax.rsqrt(var + RET_GN_EPS) * gn * load(sg_ref, n).astype(F32)
            y_ref[0, pl.ds(pl.multiple_of(n * c, c), c), :] = y.astype(y_ref.dtype)
            carried["state"] = state * cdec_f + kv

        _staged(list(range(unroll)), scores, probs, accumulate)
        return carried["state"]

    _store_head_states(ff_ref, lax.fori_loop(0, n_chunks // unroll, fwd_trip, sf_ref[0, 0]))


def _store_head_states(ref, state):
    ref[0, 0] = state[:RET_DK, :RET_DV]
    ref[0, 1] = state[RET_DK:, RET_DV:]


def _pair_states(s):
    b, nh, dk, dv = s.shape
    s = s.reshape(b, nh // 2, 2, dk, dv)
    z = jnp.zeros_like(s[:, :, 0])
    top = jnp.concatenate([s[:, :, 0], z], axis=-1)
    bot = jnp.concatenate([z, s[:, :, 1]], axis=-1)
    return jnp.concatenate([top, bot], axis=-2)


def _retention(rq, rk, rv, sg, gn, lg_f, lg_b, s_f, s_b):
    b, seq_len, _ = rq.shape
    c = RET_CHUNK if seq_len % RET_CHUNK == 0 else seq_len
    nc = seq_len // c
    npair = N_RET // 2
    pps = npair if nc == 1 else 1
    seq_spec = pl.BlockSpec((1, seq_len, pps * LANES), lambda i, p: (i, 0, p))
    st_spec = pl.BlockSpec((1, pps, LANES, LANES), lambda i, p: (i, p, 0, 0))
    fin_spec = pl.BlockSpec((1, 2 * pps, RET_DK, RET_DV), lambda i, p: (i, p, 0, 0))
    smem = pl.BlockSpec(memory_space=pltpu.SMEM)
    st_shape = jax.ShapeDtypeStruct((b, N_RET, RET_DK, RET_DV), F32)
    return pl.pallas_call(
        functools.partial(_ret_body, chunk=c, n_chunks=nc, pairs_per_step=pps),
        grid=(b, npair // pps),
        in_specs=[smem, smem, seq_spec, seq_spec, seq_spec, seq_spec,
                  pl.BlockSpec((1, pps * LANES), lambda i, p: (0, p)), st_spec, st_spec],
        out_specs=[seq_spec, fin_spec, fin_spec],
        out_shape=[jax.ShapeDtypeStruct((b, seq_len, RET_VW), BF16), st_shape, st_shape],
        scratch_shapes=[pltpu.VMEM((nc, LANES, LANES), F32)],
        compiler_params=_cparams(2),
        name="retention",
    )(lg_f, lg_b, rq, rk, rv, sg, gn.reshape(1, RET_VW), _pair_states(s_f), _pair_states(s_b))


def _staged(items, scores, probs, accumulate, ahead=2):
    n = len(items)
    queue = [scores(items[i]) for i in range(min(ahead, n))]
    pending = None
    for i in range(n):
        sc = queue.pop(0)
        if i + ahead < n:
            queue.append(scores(items[i + ahead]))
        ps = probs(items[i], sc)
        if pending is not None:
            accumulate(*pending)
        pending = (items[i], ps)
    accumulate(*pending)


def _row_probs(s_list):
    m = s_list[0].max(axis=-1, keepdims=True)
    for s in s_list[1:]:
        m = jnp.maximum(m, s.max(axis=-1, keepdims=True))
    ps = [jnp.exp2(s - m) for s in s_list]
    den = ps[0].sum(axis=-1, keepdims=True)
    for p in ps[1:]:
        den = den + p.sum(axis=-1, keepdims=True)
    return [p.astype(BF16) for p in ps], 1.0 / den


def _only_head(q, h):
    lo = _low_half(q.shape)
    zero = jnp.zeros_like(q)
    return jnp.where(lo, q, zero) if h == 0 else jnp.where(lo, zero, q)


def _attn_body(q_ref, k_ref, v_ref, o_ref, *, n_groups, kv_shared):
    tq = q_ref.shape[1]
    lo = _low_half((tq, LANES))
    outs = {}

    def lanes(j):
        return slice(j * LANES, (j + 1) * LANES)

    def scores(item):
        j, h = item
        k = k_ref[0] if kv_shared else k_ref[0, :, lanes(j)]
        return _dot_nt(_only_head(q_ref[0, :, lanes(j)].astype(BF16), h), k.astype(BF16))

    def probs(item, s):
        return _row_probs([s])

    def accumulate(item, pr):
        j, h = item
        ps, inv_den = pr
        v = v_ref[0] if kv_shared else v_ref[0, :, lanes(j)]
        outs[h] = _dot(ps[0], v.astype(BF16)) * inv_den
        if h == 1:
            o_ref[0, :, lanes(j)] = jnp.where(lo, outs[0], outs[1]).astype(o_ref.dtype)

    _staged([(j, h) for j in range(n_groups) for h in range(2)], scores, probs, accumulate)


def _col_max(x, slab=64):
    n, t = x.shape
    if n > slab and n % slab == 0:
        x = x.reshape(n // slab, slab, t).max(axis=0)
    return x.max(axis=0, keepdims=True)


def _gqa_t_body(q_ref, kn_ref, kc_ref, vtn_ref, vtc_ref, o_ref, *, n_groups, chunk):
    tq = q_ref.shape[1]
    top = lax.broadcasted_iota(jnp.int32, (LANES, tq), 0) < HEAD_DIM
    ln, lc = kn_ref.shape[1], kc_ref.shape[1]
    chunks = [(kn_ref, vtn_ref, c * chunk, chunk) for c in range(ln // chunk)] + [(kc_ref, vtc_ref, 0, lc)]
    items = [(j, h, c) for j in range(n_groups) for h in range(2) for c in range(len(chunks))]

    def scores(item):
        j, h, c = item
        k_ref, _, off, size = chunks[c]
        qh = _only_head(q_ref[0, :, j * LANES:(j + 1) * LANES], h)
        return _dot_nt(k_ref[0, off:off + size, :], qh)

    outs = {}
    state = {"m": None, "acc": None}

    def probs(item, st):
        c = item[2]
        mc = _col_max(st)
        m_new = mc if c == 0 else jnp.maximum(state["m"], mc)
        alpha = None if c == 0 else jnp.exp2(state["m"] - m_new)
        state["m"] = m_new
        return jnp.exp2(st - m_new).astype(BF16), alpha

    def accumulate(item, ps):
        j, h, c = item
        p, alpha = ps
        _, vt_ref, off, size = chunks[c]
        o = _dot(vt_ref[0, h, :, off:off + size], p)
        acc = o if c == 0 else state["acc"] * alpha + o
        state["acc"] = acc
        if c == len(chunks) - 1:
            den = acc[HEAD_DIM:HEAD_DIM + 1] if h == 0 else acc[0:1]
            outs[h] = acc * (1.0 / den)
            if h == 1:
                sl = slice(j * LANES, (j + 1) * LANES)
                o_ref[0, :, sl] = jnp.where(top, outs[0], outs[1]).T.astype(o_ref.dtype)

    _staged(items, scores, probs, accumulate)


def _vt_with_ones(v):
    vt = jnp.swapaxes(v, 1, 2)
    top = (np.arange(KV_W) < HEAD_DIM)[None, :, None]
    one = jnp.ones_like(vt)
    return jnp.stack([jnp.where(top, vt, one), jnp.where(top, one, vt)], axis=1)


def _gqa_attention_t(q, k_new, v_new, k_ctx, v_ctx, tq):
    b, lq, wq = q.shape
    ln, lc = k_new.shape[1], k_ctx.shape[1]
    chunk = GQA_KEY_CHUNK if ln % GQA_KEY_CHUNK == 0 else ln
    whole3 = lambda i, t: (i, 0, 0)
    whole4 = lambda i, t: (i, 0, 0, 0)
    return pl.pallas_call(
        functools.partial(_gqa_t_body, n_groups=wq // LANES, chunk=chunk),
        grid=(b, lq // tq),
        in_specs=[pl.BlockSpec((1, tq, wq), lambda i, t: (i, t, 0)),
                  pl.BlockSpec((1, ln, KV_W), whole3), pl.BlockSpec((1, lc, KV_W), whole3),
                  pl.BlockSpec((1, 2, KV_W, ln), whole4), pl.BlockSpec((1, 2, KV_W, lc), whole4)],
        out_specs=pl.BlockSpec((1, tq, wq), lambda i, t: (i, t, 0)),
        out_shape=jax.ShapeDtypeStruct((b, lq, wq), BF16),
        compiler_params=_cparams(2),
        name="gqa_attention_t",
    )(q, k_new, k_ctx, _vt_with_ones(v_new), _vt_with_ones(v_ctx))


def _attention(q, k, v, kv_shared, tq):
    b, lq, wq = q.shape
    lk, wk = k.shape[1:]
    return pl.pallas_call(
        functools.partial(_attn_body, n_groups=wq // LANES, kv_shared=kv_shared),
        grid=(b, lq // tq),
        in_specs=[pl.BlockSpec((1, tq, wq), lambda i, t: (i, t, 0)),
                  pl.BlockSpec((1, lk, wk), lambda i, t: (i, 0, 0)),
                  pl.BlockSpec((1, lk, wk), lambda i, t: (i, 0, 0))],
        out_specs=pl.BlockSpec((1, tq, wq), lambda i, t: (i, t, 0)),
        out_shape=jax.ShapeDtypeStruct((b, lq, wq), BF16),
        compiler_params=_cparams(2),
        name="dense_attention",
    )(q, k, v)


def _na_plan(rows):
    g = NA_ROWS_PER_STEP
    kr = min(NA_KR_MAX, rows)
    u = min(-(-(g + kr - 1) // 2) * 2, rows)
    n_steps = rows // g
    bases, vids, variants, keys = [], [], [], {}
    for s in range(n_steps):
        base = int(np.clip(s * g - kr // 2, 0, rows - u)) // 2 * 2
        r = s * g + np.arange(g)[:, None]
        krow = base + np.arange(u)[None, :]
        r0 = np.clip(r - kr // 2, 0, rows - kr)
        valid = (krow >= r0) & (krow < r0 + kr)
        assert (valid.sum(axis=1) == kr).all(), "key window not covered by the step's row range"
        dr = np.where(valid, krow - r + (NA_KR_MAX - 1), 0)
        key = (dr.tobytes(), valid.tobytes())
        if key not in keys:
            keys[key] = len(variants)
            variants.append((dr, valid))
        bases.append(base)
        vids.append(keys[key])
    return u, bases, vids, variants


def _na_bias_tables(rpb, variants, u):
    g = NA_ROWS_PER_STEP
    nh, n_dr, n_dc = rpb.shape
    cols = np.arange(GRID_W)
    c0 = np.clip(cols - NA_KC // 2, 0, GRID_W - NA_KC)
    kc = np.arange(GRID_W)[None, :]
    cvalid = (kc >= c0[:, None]) & (kc < c0[:, None] + NA_KC)
    dc = kc - cols[:, None] + (NA_KC - 1)
    sel_c = cvalid[None] & (dc[None] == np.arange(n_dc)[:, None, None])
    band = jnp.einsum("hde,eck->hdkc", rpb.astype(F32), jnp.asarray(sel_c, F32),
                      precision=lax.Precision.HIGHEST)
    band = jnp.where(cvalid.T[None, None], band * LOG2E, MASK_VALUE)
    masked = jnp.full((nh, GRID_W, GRID_W), MASK_VALUE, F32)
    tables = []
    for dr, rvalid in variants:
        key_rows = [jnp.stack([band[:, dr[q, a]] if rvalid[q, a] else masked for q in range(g)], axis=2)
                    for a in range(u)]
        tables.append(jnp.stack(key_rows, axis=1).reshape(nh, u * GRID_W, g * GRID_W))
    return jnp.stack(tables)


def _na_body(base_ref, vid_ref, q_ref, k_ref, vt_ref, kc_ref, vct_ref, bias_ref, o_ref, *, n_steps, tq, tk):
    top = lax.broadcasted_iota(jnp.int32, (LANES, tq), 0) < HEAD_DIM
    kc = kc_ref[0]
    lc = kc.shape[0]
    ones_rows = 8
    vct = jnp.concatenate([vct_ref[0], jnp.ones((ones_rows, lc), BF16)], axis=0)
    ones_win = jnp.ones((ones_rows, tk), BF16)
    unroll = int(np.gcd(n_steps, NA_UNROLL))

    def trip(t, carry):
        items = [(u, h) for u in range(unroll) for h in range(2)]
        steps = [t * unroll + u for u in range(unroll)]
        outs = {}

        def q_rows(u):
            return pl.ds(pl.multiple_of(steps[u] * tq, tq), tq)

        def k_rows(u):
            return pl.ds(pl.multiple_of(base_ref[steps[u]] * GRID_W, LANES), tk)

        def scores(item):
            u, h = item
            qh = _only_head(q_ref[0, q_rows(u), :], h)
            st_win = _dot_nt(k_ref[0, k_rows(u), :], qh) + bias_ref[vid_ref[steps[u]], h]
            return st_win, _dot_nt(kc, qh)

        def probs(item, sc):
            m = jnp.maximum(_col_max(sc[0]), _col_max(sc[1]))
            return [jnp.exp2(s - m).astype(BF16) for s in sc]

        def accumulate(item, ps):
            u, h = item
            vt = jnp.concatenate([vt_ref[0, :, k_rows(u)], ones_win], axis=0)
            acc = _dot(vt, ps[0]) + _dot(vct, ps[1])
            outs[h] = acc[:LANES] * (1.0 / acc[LANES:LANES + 1])
            if h == 1:
                o_ref[0, q_rows(u), :] = jnp.where(top, outs[0], outs[1]).T.astype(o_ref.dtype)

        _staged(items, scores, probs, accumulate)
        return carry

    lax.fori_loop(0, n_steps // unroll, trip, 0)


def _na_attention(q, k, vt, k_ctx, v_ctx, rpb):
    b, seq_len, w = q.shape
    rows = seq_len // GRID_W
    lc = k_ctx.shape[1]
    u, bases, vids, variants = _na_plan(rows)
    bias = _na_bias_tables(rpb, variants, u)
    nv = bias.shape[0]
    tq = NA_ROWS_PER_STEP * GRID_W
    tk = u * GRID_W
    n_steps = rows // NA_ROWS_PER_STEP
    smem = pl.BlockSpec(memory_space=pltpu.SMEM)
    seq_spec = pl.BlockSpec((1, seq_len, LANES), lambda p, i: (i, 0, p))
    return pl.pallas_call(
        functools.partial(_na_body, n_steps=n_steps, tq=tq, tk=tk),
        grid=(w // LANES, b),
        in_specs=[smem, smem, seq_spec, seq_spec,
                  pl.BlockSpec((1, LANES, seq_len), lambda p, i: (i, p, 0)),
                  pl.BlockSpec((1, lc, LANES), lambda p, i: (i, 0, p)),
                  pl.BlockSpec((1, LANES, lc), lambda p, i: (i, p, 0)),
                  pl.BlockSpec((nv, 2, tk, tq), lambda p, i: (0, p, 0, 0))],
        out_specs=seq_spec,
        out_shape=jax.ShapeDtypeStruct((b, seq_len, w), BF16),
        compiler_params=_cparams(2),
        name="neighbourhood_attention",
    )(jnp.asarray(bases, jnp.int32), jnp.asarray(vids, jnp.int32), q, k, vt, k_ctx,
      jnp.swapaxes(v_ctx, 1, 2), bias)


def _na_in_body(x_ref, mod_ref, gpre_ref, w_ref, q_o, k_o, v_o, *aux_o):
    h = _modulated(x_ref[0], mod_ref[0], 1, gpre_ref[...]).astype(BF16)
    p = _dot(h, w_ref[...])
    q_o[0] = (p[:, :NA_W] * SCORE_SCALE).astype(q_o.dtype)
    k = p[:, NA_W:2 * NA_W]
    v = p[:, 2 * NA_W:]
    k_o[0] = k.astype(k_o.dtype)
    v_o[0] = (v if aux_o else v.T).astype(v_o.dtype)
    if aux_o:
        k5_o, v5_o = aux_o
        k5_o[...] = k.reshape(k5_o.shape)
        v5_o[...] = v.reshape(v5_o.shape)


def _na_in(x, mod, g_pre, w, aux):
    b, seq_len, d = x.shape
    tm = _token_tile(seq_len)
    out_specs = [_tok_spec(tm, NA_W)] * 3
    out_shape = [jax.ShapeDtypeStruct((b, seq_len, NA_W), BF16)] * 3
    if aux:
        assert b == 1
        out_specs += [pl.BlockSpec((tm, N_NA, HEAD_DIM), lambda b, i: (i, 0, 0))] * 2
        out_shape += [jax.ShapeDtypeStruct((seq_len, N_NA, HEAD_DIM), F32)] * 2
    else:
        out_specs[2] = pl.BlockSpec((1, NA_W, tm), lambda b, i: (b, 0, i))
        out_shape[2] = jax.ShapeDtypeStruct((b, NA_W, seq_len), BF16)
    return pl.pallas_call(
        _na_in_body,
        grid=(b, seq_len // tm),
        in_specs=[_tok_spec(tm, d), _mod_spec(mod), _const_spec((1, d)), _const_spec(w.shape)],
        out_specs=out_specs,
        out_shape=out_shape,
        compiler_params=_cparams(2),
        name="na_in_proj",
    )(x, mod, g_pre.reshape(1, d), w)


_Q_HEAD_ORDER = [h for j in range(Q_GROUP) for h in range(j, N_Q, Q_GROUP)]


def _permute_q_cols(w_in):
    q0 = 2 * RET_W + 2 * RET_VW
    heads = [w_in[:, q0 + h * HEAD_DIM:q0 + (h + 1) * HEAD_DIM] for h in _Q_HEAD_ORDER]
    return jnp.concatenate([w_in[:, :q0]] + heads + [w_in[:, q0 + GQA_W:]], axis=1)


def _permute_att_rows(w_att):
    return jnp.concatenate([w_att[h * HEAD_DIM:(h + 1) * HEAD_DIM] for h in _Q_HEAD_ORDER], axis=0)


def _trunk(x, mods, ctx_layers, wts):
    b, seq_len, d = x.shape
    is_ctx = ctx_layers is None
    depth = len(mods)
    flat = (lambda a: a.reshape(1, b * seq_len, a.shape[-1])) if is_ctx else (lambda a: a)
    unflat = (lambda a: a.reshape(b, seq_len, a.shape[-1])) if is_ctx else (lambda a: a)
    aux = []
    x = flat(x)
    for l in range(depth):
        mod = mods[l]
        i = l // 2
        x = _ffn(x, mod, 0, wts["norm_pre"][l, 0], wts["norm_post"][l, 0],
                 wts["ffn_w_in"][l][0], wts["ffn_w_out"][l][0])
        if l % 2 == 0:
            rope_tables = None if is_ctx else _rope_tables(seq_len)
            proj = _ab_in(x, mod, wts["norm_pre"][l, 1], wts["ab_w_in"][i], wts["gqa_q_norm"][i],
                          wts["gqa_k_norm"][i], rope_tables, is_ctx)
            rq, rk, rv, sg, gq, gk, gv = [unflat(a) for a in proj[:7]]
            if is_ctx:
                s_f = jnp.zeros((b, N_RET, RET_DK, RET_DV), F32)
                s_b = s_f
            else:
                ck, cv, s_f, s_b = ctx_layers[l]
                lc = ck.shape[1]
            y_ret, fin_f, fin_b = _retention(rq, rk, rv, sg, wts["ret_gn"][i], wts["lg_f"][i],
                                             wts["lg_b"][i], s_f.astype(F32), s_b.astype(F32))
            if is_ctx:
                y_att = _attention(gq, gk, gv, True, seq_len)
            else:
                y_att = _gqa_attention_t(gq, gk, gv, ck.reshape(b, lc, KV_W).astype(BF16),
                                         cv.reshape(b, lc, KV_W).astype(BF16), GQA_TQ)
            mixer = (wts["norm_post"][l, 1], [flat(y_ret), flat(y_att)],
                     [wts["ab_w_out_ret"][i], wts["ab_w_out_att"][i]])
            aux.append((proj[7], proj[8], fin_f, fin_b) if is_ctx else None)
        else:
            proj = _na_in(x, mod, wts["norm_pre"][l, 1], wts["na_w_qkv"][i], is_ctx)
            q, k, v = [unflat(a) for a in proj[:3]]
            if is_ctx:
                y = _attention(q, k, v, False, seq_len)
                aux.append((proj[3], proj[4]))
            else:
                ck, cv = ctx_layers[l]
                lc = ck.shape[1]
                y = _na_attention(q, k, v, ck.reshape(b, lc, NA_W).astype(BF16),
                                  cv.reshape(b, lc, NA_W).astype(BF16), wts["na_rpb"][i])
                aux.append(None)
            mixer = (wts["norm_post"][l, 1], [flat(y)], [wts["na_w_out"][i]])
        x = _ffn(x, mod, 2, wts["norm_pre"][l, 2], wts["norm_post"][l, 2],
                 wts["ffn_w_in"][l][1], wts["ffn_w_out"][l][1], mixer=mixer)
    return unflat(x), aux


def kernel(x_prompt, x_sample, cache_gqa_k, cache_gqa_v, state_ret_fwd, state_ret_bwd, cache_na_k,
           cache_na_v, c, c_ctx, mod_w, mod_b, norm_pre, norm_post, ffn_w_in, ffn_w_out, ab_w_in,
           ab_w_out, ret_decay_fwd, ret_decay_bwd, ret_gn, gqa_q_norm, gqa_k_norm, na_w_qkv, na_w_out,
           na_rpb):
    depth = mod_w.shape[0]
    d = x_prompt.shape[-1]
    n_dec = c.shape[0]
    batch, seq = x_prompt.shape[:2]

    n_cond = n_dec + 1
    pad = (-n_cond) % 8
    cond = jnp.concatenate([c, c_ctx[None, :], jnp.zeros((pad, d), F32)], axis=0)
    mod_all = _modulation(cond, mod_w, mod_b).reshape(depth, n_cond + pad, 3 * N_SUB, d)
    mods_sample = [mod_all[l, :n_dec] for l in range(depth)]
    mods_prompt = [mod_all[l, n_dec:n_dec + 1] for l in range(depth)]

    n_even = ab_w_in.shape[0]
    wts = {
        "norm_pre": norm_pre, "norm_post": norm_post,
        "ffn_w_in": [[ffn_w_in[l, s].astype(BF16) for s in range(2)] for l in range(depth)],
        "ffn_w_out": [[ffn_w_out[l, s].astype(BF16) for s in range(2)] for l in range(depth)],
        "ab_w_in": [_permute_q_cols(ab_w_in[i]).astype(BF16) for i in range(n_even)],
        "ab_w_out_ret": [ab_w_out[i, :RET_VW].astype(BF16) for i in range(n_even)],
        "ab_w_out_att": [_permute_att_rows(ab_w_out[i, RET_VW:]).astype(BF16) for i in range(n_even)],
        "lg_f": jax.nn.log_sigmoid(ret_decay_fwd.astype(F32)),
        "lg_b": jax.nn.log_sigmoid(ret_decay_bwd.astype(F32)),
        "ret_gn": ret_gn, "gqa_q_norm": gqa_q_norm, "gqa_k_norm": gqa_k_norm,
        "na_w_qkv": [na_w_qkv[i].astype(BF16) for i in range(na_w_qkv.shape[0])],
        "na_w_out": [na_w_out[i].astype(BF16) for i in range(na_w_out.shape[0])],
        "na_rpb": na_rpb,
    }

    y_prompt, ctx_aux = _trunk(x_prompt, mods_prompt, None, wts)

    ctx_layers = []
    for l in range(depth):
        i = l // 2
        if l % 2 == 0:
            ctx_layers.append((cache_gqa_k[:, i], cache_gqa_v[:, i], state_ret_fwd[:, i], state_ret_bwd[:, i]))
        else:
            ctx_layers.append((cache_na_k[:, i], cache_na_v[:, i]))
    y_sample, _ = _trunk(x_sample, mods_sample, ctx_layers, wts)

    def stack(idx, layers, tail):
        return jnp.stack([ctx_aux[l][idx].reshape((batch,) + tail) for l in layers], axis=1)

    even = range(0, depth, 2)
    odd = range(1, depth, 2)
    new_gqa_k = stack(0, even, (seq, N_KV, HEAD_DIM))
    new_gqa_v = stack(1, even, (seq, N_KV, HEAD_DIM))
    new_ret_fwd = stack(2, even, (N_RET, RET_DK, RET_DV))
    new_ret_bwd = stack(3, even, (N_RET, RET_DK, RET_DV))
    new_na_k = stack(0, odd, (seq, N_NA, HEAD_DIM))
    new_na_v = stack(1, odd, (seq, N_NA, HEAD_DIM))
    return (y_prompt, y_sample, new_gqa_k, new_gqa_v, new_ret_fwd, new_ret_bwd, new_na_k, new_na_v)
```
